```python
import math
import jax, jax.numpy as jnp
from jax import lax
import numpy as np

D_MODEL = 2048
BATCH = 4
SEQ = 4096
DEPTH = 1

MIX_WIDTH = D_MODEL
DIFF_WIDTH = MIX_WIDTH // 2
MLA_WIDTH = MIX_WIDTH - DIFF_WIDTH

DH_DIFF = 64
DV_DIFF = 2 * DH_DIFF
H_DIFF = DIFF_WIDTH // DV_DIFF

QK_NOPE = 128
QK_ROPE = 64
QK_HEAD = QK_NOPE + QK_ROPE
V_MLA = 128
H_MLA = MLA_WIDTH // V_MLA
Q_LORA = 512
KV_LORA = 256
ROPE_THETA = 10000.0

D_FF = ((8 * D_MODEL + 3 * 256 - 1) // (3 * 256)) * 256

REL_BUCKETS = 32
REL_MAX_DIST = 128
Q_BLOCK = 128
EPS = 1e-6

Q_DIFF_COLS = H_DIFF * 2 * DH_DIFF
K_DIFF_COLS = H_DIFF * 2 * DH_DIFF
V_DIFF_COLS = H_DIFF * DV_DIFF
IN_COLS = Q_DIFF_COLS + K_DIFF_COLS + V_DIFF_COLS + Q_LORA + KV_LORA + QK_ROPE

kernel_name = "hybrid_diffattn_mla_adaln_encoder"


def rms_norm(x, g):
    xf = x.astype(jnp.float32)
    y = xf * lax.rsqrt(jnp.mean(xf * xf, axis=-1, keepdims=True) + EPS)
    return (y * g.astype(jnp.float32)).astype(x.dtype)


def t5_bucket(rel):
    nb = REL_BUCKETS // 2
    max_exact = nb // 2
    base = jnp.where(rel > 0, nb, 0)
    n = jnp.abs(rel)
    nf = jnp.maximum(n, 1).astype(jnp.float32)
    large = max_exact + (jnp.log(nf / max_exact) / math.log(REL_MAX_DIST / max_exact)
                         * (nb - max_exact)).astype(jnp.int32)
    large = jnp.minimum(large, nb - 1)
    return base + jnp.where(n < max_exact, n, large)


def rope_tables(seq):
    pos = jnp.arange(seq, dtype=jnp.float32)
    inv = 1.0 / (ROPE_THETA ** (jnp.arange(0, QK_ROPE, 2, dtype=jnp.float32) / QK_ROPE))
    ang = pos[:, None] * inv[None, :]
    return jnp.cos(ang)[:, None, :], jnp.sin(ang)[:, None, :]


def rope_tail(x, cos, sin):
    nope, pe = x[..., :QK_NOPE], x[..., QK_NOPE:]
    half = QK_ROPE // 2
    x1, x2 = pe[..., :half], pe[..., half:]
    c, s = cos.astype(x.dtype), sin.astype(x.dtype)
    return jnp.concatenate([nope, x1 * c - x2 * s, x2 * c + x1 * s], axis=-1)


def to_blocks(t):
    b, s = t.shape[:2]
    return jnp.moveaxis(t.reshape(b, s // Q_BLOCK, Q_BLOCK, *t.shape[2:]), 1, 0)


def from_blocks(t):
    t = jnp.moveaxis(t, 0, 1)
    return t.reshape(t.shape[0], t.shape[1] * t.shape[2], *t.shape[3:])


def differential_attention(q, k, v, lam, rel_bias):
    s = q.shape[1]
    scale = DH_DIFF ** -0.5
    k_pos = jnp.arange(s, dtype=jnp.int32)
    lam32 = lam.astype(jnp.float32)

    def block(args):
        q_blk, i = args
        logits = jnp.einsum('bqhcd,bkhcd->bchqk', q_blk, k).astype(jnp.float32) * scale
        q_pos = i * Q_BLOCK + jnp.arange(Q_BLOCK, dtype=jnp.int32)
        bias = rel_bias[t5_bucket(k_pos[None, :] - q_pos[:, None])]
        logits = logits + jnp.transpose(bias, (2, 0, 1)).astype(jnp.float32)
        p = jax.nn.softmax(logits, axis=-1)
        a = (p[:, 0] - lam32 * p[:, 1]).astype(v.dtype)
        return jnp.einsum('bhqk,bkhe->bqhe', a, v)

    out = lax.map(block, (to_blocks(q), jnp.arange(s // Q_BLOCK, dtype=jnp.int32)))
    return from_blocks(out)


def latent_attention(q, k, v):
    scale = QK_HEAD ** -0.5

    def block(q_blk):
        logits = jnp.einsum('bqhd,bkhd->bhqk', q_blk, k).astype(jnp.float32) * scale
        p = jax.nn.softmax(logits, axis=-1).astype(v.dtype)
        return jnp.einsum('bhqk,bkhd->bqhd', p, v)

    return from_blocks(lax.map(block, to_blocks(q)))


def setup_inputs(seed: int = 0) -> dict:
    key = jax.random.key(seed)
    ks = jax.random.split(key, 24)
    f32 = jnp.float32
    L = DEPTH

    def w(k, shape, fan_in, scale=1.0):
        return jax.random.normal(k, shape, f32) * (scale * fan_in ** -0.5)

    def gain(k, shape):
        return 1.0 + 0.02 * jax.random.normal(k, shape, f32)

    return {
        "x": jax.random.normal(ks[0], (BATCH, SEQ, D_MODEL), f32),
        "c": jax.random.normal(ks[1], (BATCH, D_MODEL), f32),
        "rel_bias": 0.5 * jax.random.normal(ks[2], (REL_BUCKETS, H_DIFF), f32),
        "w_ada": w(ks[3], (L, D_MODEL, 6 * D_MODEL), D_MODEL, 0.5),
        "b_ada": 0.1 * jax.random.normal(ks[4], (L, 6 * D_MODEL), f32),
        "g_norm1": gain(ks[5], (L, D_MODEL)),
        "w_in": w(ks[6], (L, D_MODEL, IN_COLS), D_MODEL),
        "g_q_diff": gain(ks[7], (L, DH_DIFF)),
        "g_k_diff": gain(ks[8], (L, DH_DIFF)),
        "lambda_vecs": 0.1 * jax.random.normal(ks[9], (L, 4, DH_DIFF), f32),
        "g_subln": gain(ks[10], (L, DV_DIFF)),
        "g_q_a": gain(ks[11], (L, Q_LORA)),
        "w_q_b": w(ks[12], (L, Q_LORA, H_MLA * QK_HEAD), Q_LORA),
        "g_kv_a": gain(ks[13], (L, KV_LORA)),
        "w_kv_b": w(ks[14], (L, KV_LORA, H_MLA * (QK_NOPE + V_MLA)), KV_LORA),
        "g_q_mla": gain(ks[15], (L, QK_HEAD)),
        "g_k_mla": gain(ks[16], (L, QK_HEAD)),
        "w_out": w(ks[17], (L, MIX_WIDTH, D_MODEL), MIX_WIDTH),
        "g_norm2": gain(ks[18], (L, D_MODEL)),
        "w_gate": w(ks[19], (L, D_MODEL, D_FF), D_MODEL),
        "w_up": w(ks[20], (L, D_MODEL, D_FF), D_MODEL),
        "w_down": w(ks[21], (L, D_FF, D_MODEL), D_FF),
    }


def reference(x, c, rel_bias, w_ada, b_ada, g_norm1, w_in, g_q_diff, g_k_diff, lambda_vecs,
              g_subln, g_q_a, w_q_b, g_kv_a, w_kv_b, g_q_mla, g_k_mla, w_out,
              g_norm2, w_gate, w_up, w_down):
    b, s, _ = x.shape
    cos, sin = rope_tables(s)
    c_act = jax.nn.silu(c)

    for l in range(DEPTH):
        lambda_init = 0.8 - 0.6 * math.exp(-0.3 * l)
        mod = c_act @ w_ada[l] + b_ada[l]
        sh1, sc1, gt1, sh2, sc2, gt2 = [m[:, None, :] for m in jnp.split(mod, 6, axis=-1)]

        h = rms_norm(x, g_norm1[l]) * (1 + sc1) + sh1
        proj = h @ w_in[l]
        o = 0
        q_d = proj[..., o:o + Q_DIFF_COLS].reshape(b, s, H_DIFF, 2, DH_DIFF); o += Q_DIFF_COLS
        k_d = proj[..., o:o + K_DIFF_COLS].reshape(b, s, H_DIFF, 2, DH_DIFF); o += K_DIFF_COLS
        v_d = proj[..., o:o + V_DIFF_COLS].reshape(b, s, H_DIFF, DV_DIFF); o += V_DIFF_COLS
        cq = proj[..., o:o + Q_LORA]; o += Q_LORA
        ckv = proj[..., o:o + KV_LORA]; o += KV_LORA
        k_pe = proj[..., o:o + QK_ROPE]

        lv = lambda_vecs[l]
        lam = (jnp.exp(jnp.sum(lv[0] * lv[1])) - jnp.exp(jnp.sum(lv[2] * lv[3]))
               + lambda_init)
        q_d = rms_norm(q_d, g_q_diff[l])
        k_d = rms_norm(k_d, g_k_diff[l])
        a_out = differential_attention(q_d, k_d, v_d, lam, rel_bias)
        a_out = (rms_norm(a_out, g_subln[l]) * (1.0 - lambda_init)).reshape(b, s, DIFF_WIDTH)

        q_m = (rms_norm(cq, g_q_a[l]) @ w_q_b[l]).reshape(b, s, H_MLA, QK_HEAD)
        kv = (rms_norm(ckv, g_kv_a[l]) @ w_kv_b[l]).reshape(b, s, H_MLA, QK_NOPE + V_MLA)
        k_nope, v_m = kv[..., :QK_NOPE], kv[..., QK_NOPE:]
        k_m = jnp.concatenate(
            [k_nope, jnp.broadcast_to(k_pe[:, :, None, :], (b, s, H_MLA, QK_ROPE))], axis=-1)
        q_m = rope_tail(rms_norm(q_m, g_q_mla[l]), cos, sin)
        k_m = rope_tail(rms_norm(k_m, g_k_mla[l]), cos, sin)
        b_out = latent_attention(q_m, k_m, v_m).reshape(b, s, MLA_WIDTH)

        mix = jnp.concatenate([a_out, b_out], axis=-1) @ w_out[l]
        x = x + gt1 * mix

        h2 = rms_norm(x, g_norm2[l]) * (1 + sc2) + sh2
        ffn = (jax.nn.silu(h2 @ w_gate[l]) * (h2 @ w_up[l])) @ w_down[l]
        x = x + gt2 * ffn

    return x
```

```python
import functools
import math

import jax
import jax.numpy as jnp
from jax import lax
from jax.experimental import pallas as pl
from jax.experimental.pallas import tpu as pltpu

F32 = jnp.float32
BF16 = jnp.bfloat16

DH_DIFF = 64
DV_DIFF = 2 * DH_DIFF
QK_NOPE = 128
QK_ROPE = 64
QK_HEAD = QK_NOPE + QK_ROPE
V_MLA = 128
Q_LORA = 512
KV_LORA = 256
ROPE_THETA = 10000.0
REL_BUCKETS = 32
REL_MAX_DIST = 128
EPS = 1e-6
LOG2E = 1.4426950408889634

LANES = 128
VMEM_LIMIT_BYTES = 56 * 1024 * 1024

ATT_T = 256
ATT_NC = 2
assert ATT_T + 1 >= REL_MAX_DIST


def _cparams(sem):
    return pltpu.CompilerParams(dimension_semantics=sem, vmem_limit_bytes=VMEM_LIMIT_BYTES)


def _const_spec(shape):
    nd = len(shape)
    return pl.BlockSpec(shape, lambda *_: (0,) * nd, pipeline_mode=pl.Buffered(1))


def _ada_kernel(c_ref, w_ref, b_ref, o_ref):
    c = c_ref[...]
    ca = c * jax.nn.sigmoid(c)
    o_ref[...] = jnp.dot(ca, w_ref[...], preferred_element_type=F32,
                         precision=lax.Precision.HIGHEST) + b_ref[...]


def _ada(c_pad, w, b, tn=1024):
    m, d = c_pad.shape
    n = w.shape[1]
    return pl.pallas_call(
        _ada_kernel,
        grid=(n // tn,),
        in_specs=[pl.BlockSpec((m, d), lambda j: (0, 0)),
                  pl.BlockSpec((d, tn), lambda j: (0, j)),
                  pl.BlockSpec((1, tn), lambda j: (0, j))],
        out_specs=pl.BlockSpec((m, tn), lambda j: (0, j)),
        out_shape=jax.ShapeDtypeStruct((m, n), F32),
        compiler_params=_cparams(("arbitrary",)),
        name="ada",
    )(c_pad, w, b)


def _modulated_norm(x, g, sc, sh):
    ms = jnp.mean(x * x, axis=-1, keepdims=True)
    return (x * lax.rsqrt(ms + EPS) * g) * (1.0 + sc) + sh


def _half_lane_norm(blk, g2, lo_mask, out_scale):
    sq = blk * blk
    s_lo = jnp.sum(jnp.where(lo_mask, sq, 0.0), axis=-1, keepdims=True)
    s_hi = jnp.sum(jnp.where(lo_mask, 0.0, sq), axis=-1, keepdims=True)
    inv = jnp.where(lo_mask, lax.rsqrt(s_lo * (1.0 / DH_DIFF) + EPS),
                    lax.rsqrt(s_hi * (1.0 / DH_DIFF) + EPS))
    return blk * inv * (g2 * out_scale)


def _in_proj_kernel(x_ref, sc_ref, sh_ref, g1_ref, wqkv_ref, wlat_ref, gqd_ref, gkd_ref,
                    gqa_ref, wqb_ref, gkva_ref, wkvb_ref, gqn_ref, gq1_ref, gq2_ref,
                    gkn_ref, gkp_ref, gkps_ref, cosq_ref, sinq_ref, cosk_ref, sink_ref,
                    qd_ref, kd_ref, vd_ref, qn_ref, qx1_ref, qx2_ref, kn_ref, kp_ref, vm_ref,
                    *, n_heads):
    h = _modulated_norm(x_ref[0], g1_ref[...], sc_ref[0], sh_ref[0]).astype(BF16)
    tm = h.shape[0]
    wd = n_heads * LANES
    lane = lax.broadcasted_iota(jnp.int32, (tm, LANES), 1)
    lo_mask = lane < DH_DIFF

    qscale = DH_DIFF ** -0.5 * LOG2E
    qd = jnp.dot(h, wqkv_ref[:, 0:wd], preferred_element_type=F32)
    for hh in range(n_heads):
        sl = slice(hh * LANES, (hh + 1) * LANES)
        qd_ref[0, :, sl] = _half_lane_norm(qd[:, sl], gqd_ref[...], lo_mask, qscale).astype(BF16)
    kd = jnp.dot(h, wqkv_ref[:, wd:2 * wd], preferred_element_type=F32)
    for hh in range(n_heads):
        sl = slice(hh * LANES, (hh + 1) * LANES)
        kd_ref[0, :, sl] = _half_lane_norm(kd[:, sl], gkd_ref[...], lo_mask, 1.0).astype(BF16)
    vd_ref[0] = jnp.dot(h, wqkv_ref[:, 2 * wd:3 * wd], preferred_element_type=F32).astype(BF16)

    lat = jnp.dot(h, wlat_ref[...], preferred_element_type=F32)
    cq = lat[:, 0:Q_LORA]
    ckv = lat[:, Q_LORA:Q_LORA + KV_LORA]
    o = Q_LORA + KV_LORA
    kpe2 = lat[:, o:o + LANES]
    kpes2 = lat[:, o + LANES:o + 2 * LANES]

    def rms(v, g):
        return v * lax.rsqrt(jnp.mean(v * v, axis=-1, keepdims=True) + EPS) * g

    kv = jnp.dot(rms(ckv, gkva_ref[...]).astype(BF16), wkvb_ref[...], preferred_element_type=F32)
    vm_ref[0] = kv[:, wd:2 * wd].astype(BF16)
    ss_pe = jnp.sum(jnp.where(lo_mask, kpe2 * kpe2, 0.0), axis=-1, keepdims=True)
    kr2 = kpe2 * gkp_ref[...] * cosk_ref[...] + kpes2 * gkps_ref[...] * sink_ref[...]
    r_heads = []
    for hh in range(n_heads):
        sl = slice(hh * LANES, (hh + 1) * LANES)
        kn = kv[:, sl]
        ss = jnp.sum(kn * kn, axis=-1, keepdims=True) + ss_pe
        r = lax.rsqrt(ss * (1.0 / QK_HEAD) + EPS)
        r_heads.append(r)
        kn_ref[0, :, sl] = (kn * r * gkn_ref[...]).astype(BF16)
    for p in range(n_heads // 2):
        rr = jnp.where(lo_mask, r_heads[2 * p], r_heads[2 * p + 1])
        kp_ref[0, :, p * LANES:(p + 1) * LANES] = (kr2 * rr).astype(BF16)

    qm = jnp.dot(rms(cq, gqa_ref[...]).astype(BF16), wqb_ref[...], preferred_element_type=F32)
    half = QK_ROPE // 2
    wx = n_heads * half
    x1 = qm[:, wd:wd + wx]
    x2 = qm[:, wd + wx:wd + 2 * wx]
    lane_x = lax.broadcasted_iota(jnp.int32, (tm, wx), 1) // half
    sq_x = x1 * x1 + x2 * x2
    mscale = QK_HEAD ** -0.5 * LOG2E
    inv_x = jnp.zeros((tm, wx), F32)
    for hh in range(n_heads):
        sl = slice(hh * LANES, (hh + 1) * LANES)
        qn = qm[:, sl]
        ss = (jnp.sum(qn * qn, axis=-1, keepdims=True)
              + jnp.sum(jnp.where(lane_x == hh, sq_x, 0.0), axis=-1, keepdims=True))
        r = lax.rsqrt(ss * (1.0 / QK_HEAD) + EPS) * mscale
        qn_ref[0, :, sl] = (qn * r * gqn_ref[...]).astype(BF16)
        inv_x = jnp.where(lane_x == hh, r, inv_x)
    a1 = x1 * gq1_ref[...]
    a2 = x2 * gq2_ref[...]
    cq_t = cosq_ref[...]
    sq_t = sinq_ref[...]
    qx1_ref[0] = ((a1 * cq_t - a2 * sq_t) * inv_x).astype(BF16)
    qx2_ref[0] = ((a2 * cq_t + a1 * sq_t) * inv_x).astype(BF16)


def _in_proj(x, sc1, sh1, g1, wqkv, wlat, gqd, gkd, gqa, wqb, gkva, wkvb, gqn, gq1, gq2,
             gkn, gkp, gkps, cosq, sinq, cosk, sink, *, n_heads, tm):
    b, s, d = x.shape
    wd = n_heads * LANES
    wx = n_heads * (QK_ROPE // 2)
    row = lambda w: pl.BlockSpec((1, tm, w), lambda bi, i: (bi, i, 0))
    mod = pl.BlockSpec((1, 1, d), lambda bi, i: (bi, 0, 0))
    tab = lambda w: pl.BlockSpec((tm, w), lambda bi, i: (i, 0))
    consts = [g1, wqkv, wlat, gqd, gkd, gqa, wqb, gkva, wkvb, gqn, gq1, gq2, gkn, gkp, gkps]
    out_widths = [wd, wd, wd, wd, wx, wx, wd, wd // 2, wd]
    return pl.pallas_call(
        functools.partial(_in_proj_kernel, n_heads=n_heads),
        grid=(b, s // tm),
        in_specs=[row(d), mod, mod] + [_const_spec(a.shape) for a in consts]
                 + [tab(wx), tab(wx), tab(LANES), tab(LANES)],
        out_specs=[row(w) for w in out_widths],
        out_shape=[jax.ShapeDtypeStruct((b, s, w), BF16) for w in out_widths],
        compiler_params=_cparams(("parallel", "parallel")),
        name="in_proj",
    )(x, sc1, sh1, *consts, cosq, sinq, cosk, sink)


def _t5_bucket(rel):
    nb = REL_BUCKETS // 2
    max_exact = nb // 2
    base = jnp.where(rel > 0, nb, 0)
    n = jnp.abs(rel)
    nf = jnp.maximum(n, 1).astype(jnp.float32)
    large = max_exact + (jnp.log(nf / max_exact) / math.log(REL_MAX_DIST / max_exact)
                         * (nb - max_exact)).astype(jnp.int32)
    large = jnp.minimum(large, nb - 1)
    return base + jnp.where(n < max_exact, n, large)


def _bias_kernel(rb_ref, bucket_ref, o_ref):
    hh = pl.program_id(0)
    t = bucket_ref.shape[-1]
    far_left = rb_ref[REL_BUCKETS // 2 - 1, hh]
    far_right = rb_ref[REL_BUCKETS - 1, hh]
    o_ref[0, 0] = jnp.full((t, t), far_left, F32) * LOG2E
    o_ref[0, 4] = jnp.full((t, t), far_right, F32) * LOG2E
    for d in range(3):
        bt = bucket_ref[d]
        acc = jnp.zeros((t, t), F32)
        for bk in range(REL_BUCKETS):
            acc = jnp.where(bt == bk, rb_ref[bk, hh], acc)
        o_ref[0, d + 1] = acc * LOG2E


def _bias_tiles(rel_bias, t):
    n_heads = rel_bias.shape[1]
    j = jnp.arange(t, dtype=jnp.int32)[:, None]
    i = jnp.arange(t, dtype=jnp.int32)[None, :]
    buckets = jnp.stack([_t5_bucket(dd * t + j - i) for dd in (-1, 0, 1)])
    return pl.pallas_call(
        _bias_kernel,
        grid=(n_heads,),
        in_specs=[pl.BlockSpec(memory_space=pltpu.SMEM),
                  pl.BlockSpec((3, t, t), lambda hh: (0, 0, 0))],
        out_specs=pl.BlockSpec((1, 5, t, t), lambda hh: (hh, 0, 0, 0)),
        out_shape=jax.ShapeDtypeStruct((n_heads, 5, t, t), F32),
        compiler_params=_cparams(("arbitrary",)),
        name="rel_bias_tiles",
    )(rel_bias, buckets)


def _softmax_step(s, offset, vt, m_ref, l_ref, acc_ref, idx):
    m_old = m_ref[idx]
    m_new = jnp.maximum(m_old, jnp.max(s, axis=0, keepdims=True) + offset)
    alpha = jnp.exp2(m_old - m_new)
    p = jnp.exp2(s - (m_new - offset))
    l_ref[idx] = alpha * l_ref[idx] + jnp.sum(p, axis=0, keepdims=True)
    acc_ref[idx] = alpha * acc_ref[idx] + jnp.dot(vt, p.astype(BF16), preferred_element_type=F32)
    m_ref[idx] = m_new


def _diff_attn_kernel(rb_ref, lamv_ref, gsub_ref, qt_ref, k_ref, vt_ref, bias_ref, o_ref,
                      m_ref, l_ref, acc_ref, *, lambda_init):
    hh = pl.program_id(1)
    qi = pl.program_id(2)
    t = qt_ref.shape[-1]
    nkt = vt_ref.shape[2]
    tk = vt_ref.shape[-1]
    nc = tk // t

    qt = qt_ref[0, 0]
    row = lax.broadcasted_iota(jnp.int32, qt.shape, 0)
    zero = jnp.zeros_like(qt)
    qts = (jnp.where(row < DH_DIFF, qt, zero), jnp.where(row < DH_DIFF, zero, qt))

    m_ref[...] = jnp.full(m_ref.shape, -jnp.inf, F32)
    l_ref[...] = jnp.zeros(l_ref.shape, F32)
    acc_ref[...] = jnp.zeros(acc_ref.shape, F32)

    def far_tile(kj, offset):
        k_t = k_ref[0, pl.ds(pl.multiple_of(kj * tk, tk), tk), :]
        vt = vt_ref[0, 0, kj]
        for mp in range(2):
            s = jnp.dot(k_t, qts[mp], preferred_element_type=F32)
            _softmax_step(s, offset, vt, m_ref, l_ref, acc_ref, mp)

    def near_tile(kj):
        k_t = k_ref[0, pl.ds(pl.multiple_of(kj * tk, tk), tk), :]
        vt = vt_ref[0, 0, kj]
        bias = jnp.concatenate(
            [bias_ref[0, jnp.clip(kj * nc + c - qi, -2, 2) + 2] for c in range(nc)], axis=0)
        for mp in range(2):
            s = jnp.dot(k_t, qts[mp], preferred_element_type=F32) + bias
            _softmax_step(s, 0.0, vt, m_ref, l_ref, acc_ref, mp)

    lo = jnp.maximum((qi - 1) // nc, 0)
    hi = jnp.minimum((qi + 2 + nc - 1) // nc, nkt)
    off_left = rb_ref[REL_BUCKETS // 2 - 1, hh] * LOG2E
    off_right = rb_ref[REL_BUCKETS - 1, hh] * LOG2E

    def left_body(kj, carry):
        far_tile(kj, off_left)
        return carry

    def near_body(kj, carry):
        near_tile(kj)
        return carry

    def right_body(kj, carry):
        far_tile(kj, off_right)
        return carry

    lax.fori_loop(0, lo, left_body, 0)
    lax.fori_loop(lo, hi, near_body, 0)
    lax.fori_loop(hi, nkt, right_body, 0)

    lv = lamv_ref[...]
    lam = (jnp.exp(jnp.sum(lv[0:1] * lv[1:2], axis=-1, keepdims=True))
           - jnp.exp(jnp.sum(lv[2:3] * lv[3:4], axis=-1, keepdims=True)) + lambda_init)
    o = acc_ref[0] / l_ref[0] - lam * (acc_ref[1] / l_ref[1])
    ms = jnp.mean(o * o, axis=0, keepdims=True)
    y = o * lax.rsqrt(ms + EPS) * (gsub_ref[...] * (1.0 - lambda_init))
    o_ref[0] = y.T.astype(BF16)


def _diff_attn(rel_bias, lamv, gsub_col, qt, k, vt, bias, *, lambda_init):
    b, n_heads, _, s = qt.shape
    t = ATT_T
    nkt, tk = vt.shape[2], vt.shape[4]
    return pl.pallas_call(
        functools.partial(_diff_attn_kernel, lambda_init=lambda_init),
        grid=(b, n_heads, s // t),
        in_specs=[pl.BlockSpec(memory_space=pltpu.SMEM),
                  pl.BlockSpec(lamv.shape, lambda bi, hh, qi: (0, 0)),
                  pl.BlockSpec(gsub_col.shape, lambda bi, hh, qi: (0, 0)),
                  pl.BlockSpec((1, 1, LANES, t), lambda bi, hh, qi: (bi, hh, 0, qi)),
                  pl.BlockSpec((1, s, LANES), lambda bi, hh, qi: (bi, 0, hh)),
                  pl.BlockSpec((1, 1, nkt, DV_DIFF, tk), lambda bi, hh, qi: (bi, hh, 0, 0, 0)),
                  pl.BlockSpec((1, 5, t, t), lambda bi, hh, qi: (hh, 0, 0, 0))],
        out_specs=pl.BlockSpec((1, t, DV_DIFF), lambda bi, hh, qi: (bi, qi, hh)),
        out_shape=jax.ShapeDtypeStruct((b, s, n_heads * DV_DIFF), BF16),
        scratch_shapes=[pltpu.VMEM((2, 1, t), F32), pltpu.VMEM((2, 1, t), F32),
                        pltpu.VMEM((2, DV_DIFF, t), F32)],
        compiler_params=_cparams(("parallel", "parallel", "parallel")),
        name="diff_attn",
    )(rel_bias, lamv, gsub_col, qt, k, vt, bias)


def _mla_attn_kernel(qt_ref, k_ref, vt_ref, o_ref, m_ref, l_ref, acc_ref):
    nkt = vt_ref.shape[2]
    tk = vt_ref.shape[-1]
    qt = qt_ref[0, 0]

    m_ref[...] = jnp.full(m_ref.shape, -jnp.inf, F32)
    l_ref[...] = jnp.zeros(l_ref.shape, F32)
    acc_ref[...] = jnp.zeros(acc_ref.shape, F32)

    def body(kj, carry):
        k_t = k_ref[0, 0, pl.ds(pl.multiple_of(kj * tk, tk), tk), :]
        s = jnp.dot(k_t, qt, preferred_element_type=F32)
        _softmax_step(s, 0.0, vt_ref[0, 0, kj], m_ref, l_ref, acc_ref, 0)
        return carry

    lax.fori_loop(0, nkt, body, 0)
    o_ref[0] = (acc_ref[0] / l_ref[0]).T.astype(BF16)


def _mla_attn(qt, k, vt):
    b, n_heads, _, s = qt.shape
    t = ATT_T
    nkt, tk = vt.shape[2], vt.shape[4]
    return pl.pallas_call(
        _mla_attn_kernel,
        grid=(b, n_heads, s // t),
        in_specs=[pl.BlockSpec((1, 1, QK_HEAD, t), lambda bi, hh, qi: (bi, hh, 0, qi)),
                  pl.BlockSpec((1, 1, s, QK_HEAD), lambda bi, hh, qi: (bi, hh, 0, 0)),
                  pl.BlockSpec((1, 1, nkt, V_MLA, tk), lambda bi, hh, qi: (bi, hh, 0, 0, 0))],
        out_specs=pl.BlockSpec((1, t, V_MLA), lambda bi, hh, qi: (bi, qi, hh)),
        out_shape=jax.ShapeDtypeStruct((b, s, n_heads * V_MLA), BF16),
        scratch_shapes=[pltpu.VMEM((1, 1, t), F32), pltpu.VMEM((1, 1, t), F32),
                        pltpu.VMEM((1, V_MLA, t), F32)],
        compiler_params=_cparams(("parallel", "parallel", "parallel")),
        name="mla_attn",
    )(qt, k, vt)


def _out_proj_kernel(x_ref, gt_ref, a_ref, b_ref, wa_ref, wb_ref, o_ref):
    mix = (jnp.dot(a_ref[0], wa_ref[...], preferred_element_type=F32)
           + jnp.dot(b_ref[0], wb_ref[...], preferred_element_type=F32))
    o_ref[0] = x_ref[0] + gt_ref[0] * mix


def _out_proj(x, gt1, a, bb, wa, wb, *, tm):
    b, s, d = x.shape
    row = lambda w: pl.BlockSpec((1, tm, w), lambda bi, i: (bi, i, 0))
    return pl.pallas_call(
        _out_proj_kernel,
        grid=(b, s // tm),
        in_specs=[row(d), pl.BlockSpec((1, 1, d), lambda bi, i: (bi, 0, 0)),
                  row(a.shape[-1]), row(bb.shape[-1]), _const_spec(wa.shape), _const_spec(wb.shape)],
        out_specs=row(d),
        out_shape=jax.ShapeDtypeStruct((b, s, d), F32),
        compiler_params=_cparams(("parallel", "parallel")),
        name="out_proj",
    )(x, gt1, a, bb, wa, wb)


def _ffn_kernel(x_ref, sc_ref, sh_ref, gt_ref, g2_ref, wg_ref, wu_ref, wd_ref, o_ref, h_ref, acc_ref):
    j = pl.program_id(2)

    @pl.when(j == 0)
    def _():
        h_ref[...] = _modulated_norm(x_ref[0], g2_ref[...], sc_ref[0], sh_ref[0]).astype(BF16)
        acc_ref[...] = jnp.zeros(acc_ref.shape, F32)

    h = h_ref[...]
    g = jnp.dot(h, wg_ref[...], preferred_element_type=F32)
    u = jnp.dot(h, wu_ref[...], preferred_element_type=F32)
    act = (g * jax.nn.sigmoid(g) * u).astype(BF16)
    acc_ref[...] += jnp.dot(act, wd_ref[...], preferred_element_type=F32)

    @pl.when(j == pl.num_programs(2) - 1)
    def _():
        o_ref[0] = x_ref[0] + gt_ref[0] * acc_ref[...]


def _ffn(x, sc2, sh2, gt2, g2, wg, wu, wd, *, tm, tf):
    b, s, d = x.shape
    dff = wg.shape[1]
    row = pl.BlockSpec((1, tm, d), lambda bi, i, j: (bi, i, 0))
    mod = pl.BlockSpec((1, 1, d), lambda bi, i, j: (bi, 0, 0))
    return pl.pallas_call(
        _ffn_kernel,
        grid=(b, s // tm, dff // tf),
        in_specs=[row, mod, mod, mod, pl.BlockSpec((1, d), lambda bi, i, j: (0, 0)),
                  pl.BlockSpec((d, tf), lambda bi, i, j: (0, j)),
                  pl.BlockSpec((d, tf), lambda bi, i, j: (0, j)),
                  pl.BlockSpec((tf, d), lambda bi, i, j: (j, 0))],
        out_specs=row,
        out_shape=jax.ShapeDtypeStruct((b, s, d), F32),
        scratch_shapes=[pltpu.VMEM((tm, d), BF16), pltpu.VMEM((tm, d), F32)],
        compiler_params=_cparams(("parallel", "parallel", "arbitrary")),
        name="ffn",
    )(x, sc2, sh2, gt2, g2, wg, wu, wd)


def _rope_tables(s):
    pos = jnp.arange(s, dtype=jnp.float32)
    inv = 1.0 / (ROPE_THETA ** (jnp.arange(0, QK_ROPE, 2, dtype=jnp.float32) / QK_ROPE))
    ang = pos[:, None] * inv[None, :]
    return jnp.cos(ang), jnp.sin(ang)


def _pick_tile(n, pref):
    t = min(pref, n)
    while n % t:
        t //= 2
    return t


def kernel(x, c, rel_bias, w_ada, b_ada, g_norm1, w_in, g_q_diff, g_k_diff, lambda_vecs, g_subln, g_q_a, w_q_b, g_kv_a, w_kv_b, g_q_mla, g_k_mla, w_out, g_norm2, w_gate, w_up, w_down):
    b, s, d = x.shape
    depth = w_ada.shape[0]
    diff_width = d // 2
    n_hd = diff_width // DV_DIFF
    n_hm = (d - diff_width) // V_MLA
    assert n_hd == n_hm and n_hd % 2 == 0
    n_heads = n_hd
    wd = n_heads * LANES
    half = QK_ROPE // 2
    t = ATT_T
    assert s % t == 0
    tk = _pick_tile(s, ATT_NC * t)
    nkt = s // tk
    tm = _pick_tile(s, 512)

    cos, sin = _rope_tables(s)
    cosq, sinq = jnp.tile(cos, (1, n_heads)), jnp.tile(sin, (1, n_heads))
    cosk = jnp.tile(cos, (1, 4))
    sink = jnp.tile(jnp.concatenate([-sin, sin], axis=1), (1, 2))

    c_pad = jnp.pad(c, ((0, (-b) % 8), (0, 0)))

    for l in range(depth):
        lambda_init = 0.8 - 0.6 * math.exp(-0.3 * l)
        mod = _ada(c_pad, w_ada[l], b_ada[l][None, :])[:b]
        sh1, sc1, gt1, sh2, sc2, gt2 = [m[:, None, :] for m in jnp.split(mod, 6, axis=-1)]

        wi = w_in[l]
        o = 3 * wd + Q_LORA + KV_LORA
        w_kpe = wi[:, o:o + QK_ROPE]
        w_kpe_sw = jnp.concatenate([w_kpe[:, half:], w_kpe[:, :half]], axis=1)
        wqkv = wi[:, :3 * wd].astype(BF16)
        wlat = jnp.concatenate([wi[:, 3 * wd:o], w_kpe, w_kpe, w_kpe_sw, w_kpe_sw], axis=1).astype(BF16)
        wq = w_q_b[l].reshape(Q_LORA, n_heads, QK_HEAD)
        wqb = jnp.concatenate([wq[:, :, :QK_NOPE].reshape(Q_LORA, -1),
                               wq[:, :, QK_NOPE:QK_NOPE + half].reshape(Q_LORA, -1),
                               wq[:, :, QK_NOPE + half:].reshape(Q_LORA, -1)], axis=1).astype(BF16)
        wkv = w_kv_b[l].reshape(KV_LORA, n_heads, QK_NOPE + V_MLA)
        wkvb = jnp.concatenate([wkv[:, :, :QK_NOPE].reshape(KV_LORA, -1),
                                wkv[:, :, QK_NOPE:].reshape(KV_LORA, -1)], axis=1).astype(BF16)
        gq, gk = g_q_mla[l], g_k_mla[l]
        gk_pe = gk[QK_NOPE:]
        gk_pe_sw = jnp.concatenate([gk_pe[half:], gk_pe[:half]])

        qd, kd, vd, qn, qx1, qx2, kn, kp, vm = _in_proj(
            x, sc1, sh1, g_norm1[l][None, :], wqkv, wlat,
            jnp.tile(g_q_diff[l], 2)[None, :], jnp.tile(g_k_diff[l], 2)[None, :],
            g_q_a[l][None, :], wqb, g_kv_a[l][None, :], wkvb,
            gq[None, :QK_NOPE], jnp.tile(gq[QK_NOPE:QK_NOPE + half], n_heads)[None, :],
            jnp.tile(gq[QK_NOPE + half:], n_heads)[None, :],
            gk[None, :QK_NOPE], jnp.tile(gk_pe, 2)[None, :], jnp.tile(gk_pe_sw, 2)[None, :],
            cosq, sinq, cosk, sink, n_heads=n_heads, tm=tm)

        def heads_t(a, w):
            return a.reshape(b, s, n_heads, w).transpose(0, 2, 3, 1)

        def value_t(a, w):
            return a.reshape(b, nkt, tk, n_heads, w).transpose(0, 3, 1, 4, 2)

        qt_d = heads_t(qd, LANES)
        vt_d = value_t(vd, DV_DIFF)
        qt_m = jnp.concatenate([heads_t(qn, QK_NOPE), heads_t(qx1, half), heads_t(qx2, half)], axis=2)
        k_m = jnp.concatenate([kn.reshape(b, s, n_heads, QK_NOPE), kp.reshape(b, s, n_heads, QK_ROPE)],
                              axis=-1).transpose(0, 2, 1, 3)
        vt_m = value_t(vm, V_MLA)

        bias = _bias_tiles(rel_bias, t)
        a_out = _diff_attn(rel_bias, lambda_vecs[l], g_subln[l][:, None], qt_d, kd, vt_d, bias,
                           lambda_init=lambda_init)
        b_out = _mla_attn(qt_m, k_m, vt_m)

        wo = w_out[l].astype(BF16)
        x = _out_proj(x, gt1, a_out, b_out, wo[:diff_width], wo[diff_width:], tm=tm)
        x = _ffn(x, sc2, sh2, gt2, g_norm2[l][None, :], w_gate[l].astype(BF16), w_up[l].astype(BF16),
                 w_down[l].astype(BF16), tm=tm, tf=_pick_tile(w_gate.shape[-1], 512))
    return x
```

```python
import functools
import math

import jax
import jax.numpy as jnp
from jax import lax
from jax.experimental import pallas as pl
from jax.experimental.pallas import tpu as pltpu

F32 = jnp.float32
BF16 = jnp.bfloat16

DH_DIFF = 64
DV_DIFF = 2 * DH_DIFF
QK_NOPE = 128
QK_ROPE = 64
QK_HEAD = QK_NOPE + QK_ROPE
V_MLA = 128
Q_LORA = 512
KV_LORA = 256
ROPE_THETA = 10000.0
REL_BUCKETS = 32
REL_MAX_DIST = 128
EPS = 1e-6
LOG2E = 1.4426950408889634

LANES = 128
VMEM_LIMIT_BYTES = 56 * 1024 * 1024

ATT_TK = 512
DIFF_TQ = 512
MLA_TQ = 512


def _cparams(sem):
    return pltpu.CompilerParams(dimension_semantics=sem, vmem_limit_bytes=VMEM_LIMIT_BYTES)


def _const_spec(shape):
    nd = len(shape)
    return pl.BlockSpec(shape, lambda *_: (0,) * nd, pipeline_mode=pl.Buffered(1))


def _ada_kernel(c_ref, w_ref, b_ref, o_ref):
    c = c_ref[...]
    ca = c * jax.nn.sigmoid(c)
    o_ref[...] = jnp.dot(ca, w_ref[...], preferred_element_type=F32,
                         precision=lax.Precision.HIGHEST) + b_ref[...]


def _ada(c_pad, w, b, tn=1024):
    m, d = c_pad.shape
    n = w.shape[1]
    return pl.pallas_call(
        _ada_kernel,
        grid=(n // tn,),
        in_specs=[pl.BlockSpec((m, d), lambda j: (0, 0)),
                  pl.BlockSpec((d, tn), lambda j: (0, j)),
                  pl.BlockSpec((1, tn), lambda j: (0, j))],
        out_specs=pl.BlockSpec((m, tn), lambda j: (0, j)),
        out_shape=jax.ShapeDtypeStruct((m, n), F32),
        compiler_params=_cparams(("arbitrary",)),
        name="ada",
    )(c_pad, w, b)


def _modulated_norm(x, g, sc, sh):
    ms = jnp.mean(x * x, axis=-1, keepdims=True)
    return (x * lax.rsqrt(ms + EPS) * g) * (1.0 + sc) + sh


def _half_lane_norm(blk, g2, lo_mask, out_scale):
    sq = blk * blk
    s_lo = jnp.sum(jnp.where(lo_mask, sq, 0.0), axis=-1, keepdims=True)
    s_hi = jnp.sum(jnp.where(lo_mask, 0.0, sq), axis=-1, keepdims=True)
    inv = jnp.where(lo_mask, lax.rsqrt(s_lo * (1.0 / DH_DIFF) + EPS),
                    lax.rsqrt(s_hi * (1.0 / DH_DIFF) + EPS))
    return blk * inv * (g2 * out_scale)


def _in_proj_kernel(x_ref, sc_ref, sh_ref, g1_ref, wqkv_ref, wlat_ref, gqd_ref, gkd_ref,
                    gqa_ref, wqb_ref, gkva_ref, wkvb_ref, gqn_ref, gq1_ref, gq2_ref,
                    gkn_ref, gkp_ref, gkps_ref, cosq_ref, sinq_ref, cosk_ref, sink_ref,
                    qd_ref, kd_ref, vd_ref, qn_ref, qx1_ref, qx2_ref, kn_ref, kp_ref, vm_ref,
                    *, n_heads):
    h = _modulated_norm(x_ref[0], g1_ref[...], sc_ref[0], sh_ref[0]).astype(BF16)
    tm = h.shape[0]
    wd = n_heads * LANES
    lane = lax.broadcasted_iota(jnp.int32, (tm, LANES), 1)
    lo_mask = lane < DH_DIFF

    qscale = DH_DIFF ** -0.5 * LOG2E
    qd = jnp.dot(h, wqkv_ref[:, 0:wd], preferred_element_type=F32)
    for hh in range(n_heads):
        sl = slice(hh * LANES, (hh + 1) * LANES)
        qd_ref[0, :, sl] = _half_lane_norm(qd[:, sl], gqd_ref[...], lo_mask, qscale).astype(BF16)
    kd = jnp.dot(h, wqkv_ref[:, wd:2 * wd], preferred_element_type=F32)
    for hh in range(n_heads):
        sl = slice(hh * LANES, (hh + 1) * LANES)
        kd_ref[0, :, sl] = _half_lane_norm(kd[:, sl], gkd_ref[...], lo_mask, 1.0).astype(BF16)
    vd_ref[0] = jnp.dot(h, wqkv_ref[:, 2 * wd:3 * wd], preferred_element_type=F32).astype(BF16)

    lat = jnp.dot(h, wlat_ref[...], preferred_element_type=F32)
    cq = lat[:, 0:Q_LORA]
    ckv = lat[:, Q_LORA:Q_LORA + KV_LORA]
    o = Q_LORA + KV_LORA
    kpe2 = lat[:, o:o + LANES]
    kpes2 = lat[:, o + LANES:o + 2 * LANES]

    def rms(v, g):
        return v * lax.rsqrt(jnp.mean(v * v, axis=-1, keepdims=True) + EPS) * g

    kv = jnp.dot(rms(ckv, gkva_ref[...]).astype(BF16), wkvb_ref[...], preferred_element_type=F32)
    vm_ref[0] = kv[:, wd:2 * wd].astype(BF16)
    ss_pe = jnp.sum(jnp.where(lo_mask, kpe2 * kpe2, 0.0), axis=-1, keepdims=True)
    kr2 = kpe2 * gkp_ref[...] * cosk_ref[...] + kpes2 * gkps_ref[...] * sink_ref[...]
    r_heads = []
    for hh in range(n_heads):
        sl = slice(hh * LANES, (hh + 1) * LANES)
        kn = kv[:, sl]
        ss = jnp.sum(kn * kn, axis=-1, keepdims=True) + ss_pe
        r = lax.rsqrt(ss * (1.0 / QK_HEAD) + EPS)
        r_heads.append(r)
        kn_ref[0, :, sl] = (kn * r * gkn_ref[...]).astype(BF16)
    for p in range(n_heads // 2):
        rr = jnp.where(lo_mask, r_heads[2 * p], r_heads[2 * p + 1])
        kp_ref[0, :, p * LANES:(p + 1) * LANES] = (kr2 * rr).astype(BF16)

    qm = jnp.dot(rms(cq, gqa_ref[...]).astype(BF16), wqb_ref[...], preferred_element_type=F32)
    half = QK_ROPE // 2
    wx = n_heads * half
    x1 = qm[:, wd:wd + wx]
    x2 = qm[:, wd + wx:wd + 2 * wx]
    lane_x = lax.broadcasted_iota(jnp.int32, (tm, wx), 1) // half
    sq_x = x1 * x1 + x2 * x2
    mscale = QK_HEAD ** -0.5 * LOG2E
    inv_x = jnp.zeros((tm, wx), F32)
    for hh in range(n_heads):
        sl = slice(hh * LANES, (hh + 1) * LANES)
        qn = qm[:, sl]
        ss = (jnp.sum(qn * qn, axis=-1, keepdims=True)
              + jnp.sum(jnp.where(lane_x == hh, sq_x, 0.0), axis=-1, keepdims=True))
        r = lax.rsqrt(ss * (1.0 / QK_HEAD) + EPS) * mscale
        qn_ref[0, :, sl] = (qn * r * gqn_ref[...]).astype(BF16)
        inv_x = jnp.where(lane_x == hh, r, inv_x)
    a1 = x1 * gq1_ref[...]
    a2 = x2 * gq2_ref[...]
    cq_t = cosq_ref[...]
    sq_t = sinq_ref[...]
    qx1_ref[0] = ((a1 * cq_t - a2 * sq_t) * inv_x).astype(BF16)
    qx2_ref[0] = ((a2 * cq_t + a1 * sq_t) * inv_x).astype(BF16)


def _in_proj(x, sc1, sh1, g1, wqkv, wlat, gqd, gkd, gqa, wqb, gkva, wkvb, gqn, gq1, gq2,
             gkn, gkp, gkps, cosq, sinq, cosk, sink, *, n_heads, tm):
    b, s, d = x.shape
    wd = n_heads * LANES
    wx = n_heads * (QK_ROPE // 2)
    row = lambda w: pl.BlockSpec((1, tm, w), lambda bi, i: (bi, i, 0))
    mod = pl.BlockSpec((1, 1, d), lambda bi, i: (bi, 0, 0))
    tab = lambda w: pl.BlockSpec((tm, w), lambda bi, i: (i, 0))
    consts = [g1, wqkv, wlat, gqd, gkd, gqa, wqb, gkva, wkvb, gqn, gq1, gq2, gkn, gkp, gkps]
    out_widths = [wd, wd, wd, wd, wx, wx, wd, wd // 2, wd]
    return pl.pallas_call(
        functools.partial(_in_proj_kernel, n_heads=n_heads),
        grid=(b, s // tm),
        in_specs=[row(d), mod, mod] + [_const_spec(a.shape) for a in consts]
                 + [tab(wx), tab(wx), tab(LANES), tab(LANES)],
        out_specs=[row(w) for w in out_widths],
        out_shape=[jax.ShapeDtypeStruct((b, s, w), BF16) for w in out_widths],
        compiler_params=_cparams(("parallel", "parallel")),
        name="in_proj",
    )(x, sc1, sh1, *consts, cosq, sinq, cosk, sink)


def _t5_bucket(rel):
    nb = REL_BUCKETS // 2
    max_exact = nb // 2
    base = jnp.where(rel > 0, nb, 0)
    n = jnp.abs(rel)
    nf = jnp.maximum(n, 1).astype(jnp.float32)
    large = max_exact + (jnp.log(nf / max_exact) / math.log(REL_MAX_DIST / max_exact)
                         * (nb - max_exact)).astype(jnp.int32)
    large = jnp.minimum(large, nb - 1)
    return base + jnp.where(n < max_exact, n, large)


def _bias_kernel(rb_ref, bucket_ref, o_ref):
    hh = pl.program_id(0)
    t = bucket_ref.shape[-1]
    far_left = rb_ref[REL_BUCKETS // 2 - 1, hh]
    far_right = rb_ref[REL_BUCKETS - 1, hh]
    o_ref[0, 0] = jnp.full((t, t), far_left, F32) * LOG2E
    o_ref[0, 4] = jnp.full((t, t), far_right, F32) * LOG2E
    for d in range(3):
        bt = bucket_ref[d]
        acc = jnp.zeros((t, t), F32)
        for bk in range(REL_BUCKETS):
            acc = jnp.where(bt == bk, rb_ref[bk, hh], acc)
        o_ref[0, d + 1] = acc * LOG2E


def _bias_tiles(rel_bias, t):
    n_heads = rel_bias.shape[1]
    j = jnp.arange(t, dtype=jnp.int32)[:, None]
    i = jnp.arange(t, dtype=jnp.int32)[None, :]
    buckets = jnp.stack([_t5_bucket(dd * t + j - i) for dd in (-1, 0, 1)])
    return pl.pallas_call(
        _bias_kernel,
        grid=(n_heads,),
        in_specs=[pl.BlockSpec(memory_space=pltpu.SMEM),
                  pl.BlockSpec((3, t, t), lambda hh: (0, 0, 0))],
        out_specs=pl.BlockSpec((1, 5, t, t), lambda hh: (hh, 0, 0, 0)),
        out_shape=jax.ShapeDtypeStruct((n_heads, 5, t, t), F32),
        compiler_params=_cparams(("arbitrary",)),
        name="rel_bias_tiles",
    )(rel_bias, buckets)


def _softmax_state_step(s, offset, vt, state):
    m_old, l_old, acc_old = state
    m_new = jnp.maximum(m_old, jnp.max(s, axis=0, keepdims=True) + offset)
    alpha = jnp.exp2(m_old - m_new)
    p = jnp.exp2(s - (m_new - offset))
    l_new = alpha * l_old + jnp.sum(p, axis=0, keepdims=True)
    acc_new = alpha * acc_old + jnp.dot(vt, p.astype(BF16), preferred_element_type=F32)
    return m_new, l_new, acc_new


def _init_state(dv, t):
    return (jnp.full((1, t), -jnp.inf, F32), jnp.zeros((1, t), F32), jnp.zeros((dv, t), F32))


def _diff_attn_kernel(rb_ref, lamv_ref, gsub_ref, qt_ref, k_ref, vt_ref, bias_ref, o_ref, *, lambda_init):
    hh = pl.program_id(1)
    qi = pl.program_id(2)
    t = qt_ref.shape[-1]
    nkt = vt_ref.shape[2]
    tk = vt_ref.shape[-1]
    nc = tk // t

    qt = qt_ref[0, 0]
    row = lax.broadcasted_iota(jnp.int32, qt.shape, 0)
    zero = jnp.zeros_like(qt)
    qts = (jnp.where(row < DH_DIFF, qt, zero), jnp.where(row < DH_DIFF, zero, qt))
    states = [_init_state(DV_DIFF, t), _init_state(DV_DIFF, t)]

    off_left = rb_ref[REL_BUCKETS // 2 - 1, hh] * LOG2E
    off_right = rb_ref[REL_BUCKETS - 1, hh] * LOG2E

    n_near = min(nkt, 3 if nc == 1 else 2)
    k_start = (qi - 1 + nc) // nc - 1

    def tile_index(r):
        return lax.rem(k_start + r + nkt, nkt)

    def scores(r):
        kj = tile_index(r)
        k_t = k_ref[0, pl.ds(pl.multiple_of(kj * tk, tk), tk), :]
        out = []
        for mp in range(2):
            s = jnp.dot(k_t, qts[mp], preferred_element_type=F32)
            if r < n_near:
                tiles = [bias_ref[0, jnp.clip(kj * nc + c - qi, -2, 2) + 2] for c in range(nc)]
                s = s + (tiles[0] if nc == 1 else jnp.concatenate(tiles, axis=0))
            out.append(s)
        return out

    s_next = scores(0)
    for r in range(nkt):
        s_cur = s_next
        if r + 1 < nkt:
            s_next = scores(r + 1)
        kj = tile_index(r)
        vt = vt_ref[0, 0, kj]
        offset = 0.0 if r < n_near else jnp.where(kj * nc < qi, off_left, off_right)
        for mp in range(2):
            states[mp] = _softmax_state_step(s_cur[mp], offset, vt, states[mp])

    lv = lamv_ref[...]
    lam = (jnp.exp(jnp.sum(lv[0:1] * lv[1:2], axis=-1, keepdims=True))
           - jnp.exp(jnp.sum(lv[2:3] * lv[3:4], axis=-1, keepdims=True)) + lambda_init)
    (_, l1, acc1), (_, l2, acc2) = states
    o = acc1 / l1 - lam * (acc2 / l2)
    ms = jnp.mean(o * o, axis=0, keepdims=True)
    y = o * lax.rsqrt(ms + EPS) * (gsub_ref[...] * (1.0 - lambda_init))
    o_ref[0] = y.T.astype(BF16)


def _diff_attn(rel_bias, lamv, gsub_col, qt, k, vt, bias, *, lambda_init):
    b, n_heads, _, s = qt.shape
    t = bias.shape[-1]
    nkt, tk = vt.shape[2], vt.shape[4]
    return pl.pallas_call(
        functools.partial(_diff_attn_kernel, lambda_init=lambda_init),
        grid=(b, n_heads, s // t),
        in_specs=[pl.BlockSpec(memory_space=pltpu.SMEM),
                  pl.BlockSpec(lamv.shape, lambda bi, hh, qi: (0, 0)),
                  pl.BlockSpec(gsub_col.shape, lambda bi, hh, qi: (0, 0)),
                  pl.BlockSpec((1, 1, LANES, t), lambda bi, hh, qi: (bi, hh, 0, qi)),
                  pl.BlockSpec((1, s, LANES), lambda bi, hh, qi: (bi, 0, hh)),
                  pl.BlockSpec((1, 1, nkt, DV_DIFF, tk), lambda bi, hh, qi: (bi, hh, 0, 0, 0)),
                  pl.BlockSpec((1, 5, t, t), lambda bi, hh, qi: (hh, 0, 0, 0))],
        out_specs=pl.BlockSpec((1, t, DV_DIFF), lambda bi, hh, qi: (bi, qi, hh)),
        out_shape=jax.ShapeDtypeStruct((b, s, n_heads * DV_DIFF), BF16),
        compiler_params=_cparams(("parallel", "parallel", "parallel")),
        name="diff_attn",
    )(rel_bias, lamv, gsub_col, qt, k, vt, bias)


def _mla_attn_kernel(qt_ref, k_ref, vt_ref, o_ref):
    nkt = vt_ref.shape[2]
    tk = vt_ref.shape[-1]
    qt = qt_ref[0, 0]
    state = _init_state(V_MLA, qt.shape[-1])
    scores = lambda kj: jnp.dot(k_ref[0, 0, kj * tk:(kj + 1) * tk, :], qt, preferred_element_type=F32)
    s_next = scores(0)
    for kj in range(nkt):
        s = s_next
        if kj + 1 < nkt:
            s_next = scores(kj + 1)
        state = _softmax_state_step(s, 0.0, vt_ref[0, 0, kj], state)
    _, l, acc = state
    o_ref[0] = (acc / l).T.astype(BF16)


def _mla_attn(qt, k, vt):
    b, n_heads, _, s = qt.shape
    t = _pick_tile(s, MLA_TQ)
    nkt, tk = vt.shape[2], vt.shape[4]
    return pl.pallas_call(
        _mla_attn_kernel,
        grid=(b, n_heads, s // t),
        in_specs=[pl.BlockSpec((1, 1, QK_HEAD, t), lambda bi, hh, qi: (bi, hh, 0, qi)),
                  pl.BlockSpec((1, 1, s, QK_HEAD), lambda bi, hh, qi: (bi, hh, 0, 0)),
                  pl.BlockSpec((1, 1, nkt, V_MLA, tk), lambda bi, hh, qi: (bi, hh, 0, 0, 0))],
        out_specs=pl.BlockSpec((1, t, V_MLA), lambda bi, hh, qi: (bi, qi, hh)),
        out_shape=jax.ShapeDtypeStruct((b, s, n_heads * V_MLA), BF16),
        compiler_params=_cparams(("parallel", "parallel", "parallel")),
        name="mla_attn",
    )(qt, k, vt)


def _out_proj_kernel(x_ref, gt_ref, a_ref, b_ref, wa_ref, wb_ref, o_ref):
    mix = (jnp.dot(a_ref[0], wa_ref[...], preferred_element_type=F32)
           + jnp.dot(b_ref[0], wb_ref[...], preferred_element_type=F32))
    o_ref[0] = x_ref[0] + gt_ref[0] * mix


def _out_proj(x, gt1, a, bb, wa, wb, *, tm):
    b, s, d = x.shape
    row = lambda w: pl.BlockSpec((1, tm, w), lambda bi, i: (bi, i, 0))
    return pl.pallas_call(
        _out_proj_kernel,
        grid=(b, s // tm),
        in_specs=[row(d), pl.BlockSpec((1, 1, d), lambda bi, i: (bi, 0, 0)),
                  row(a.shape[-1]), row(bb.shape[-1]), _const_spec(wa.shape), _const_spec(wb.shape)],
        out_specs=row(d),
        out_shape=jax.ShapeDtypeStruct((b, s, d), F32),
        compiler_params=_cparams(("parallel", "parallel")),
        name="out_proj",
    )(x, gt1, a, bb, wa, wb)


def _ffn_kernel(x_ref, sc_ref, sh_ref, gt_ref, g2_ref, wg_ref, wu_ref, wd_ref, o_ref, h_ref, acc_ref):
    j = pl.program_id(2)

    @pl.when(j == 0)
    def _():
        h_ref[...] = _modulated_norm(x_ref[0], g2_ref[...], sc_ref[0], sh_ref[0]).astype(BF16)
        acc_ref[...] = jnp.zeros(acc_ref.shape, F32)

    h = h_ref[...]
    g = jnp.dot(h, wg_ref[...], preferred_element_type=F32)
    u = jnp.dot(h, wu_ref[...], preferred_element_type=F32)
    act = (g * jax.nn.sigmoid(g) * u).astype(BF16)
    acc_ref[...] += jnp.dot(act, wd_ref[...], preferred_element_type=F32)

    @pl.when(j == pl.num_programs(2) - 1)
    def _():
        o_ref[0] = x_ref[0] + gt_ref[0] * acc_ref[...]


def _ffn(x, sc2, sh2, gt2, g2, wg, wu, wd, *, tm, tf):
    b, s, d = x.shape
    dff = wg.shape[1]
    row = pl.BlockSpec((1, tm, d), lambda bi, i, j: (bi, i, 0))
    mod = pl.BlockSpec((1, 1, d), lambda bi, i, j: (bi, 0, 0))
    return pl.pallas_call(
        _ffn_kernel,
        grid=(b, s // tm, dff // tf),
        in_specs=[row, mod, mod, mod, pl.BlockSpec((1, d), lambda bi, i, j: (0, 0)),
                  pl.BlockSpec((d, tf), lambda bi, i, j: (0, j)),
                  pl.BlockSpec((d, tf), lambda bi, i, j: (0, j)),
                  pl.BlockSpec((tf, d), lambda bi, i, j: (j, 0))],
        out_specs=row,
        out_shape=jax.ShapeDtypeStruct((b, s, d), F32),
        scratch_shapes=[pltpu.VMEM((tm, d), BF16), pltpu.VMEM((tm, d), F32)],
        compiler_params=_cparams(("parallel", "parallel", "arbitrary")),
        name="ffn",
    )(x, sc2, sh2, gt2, g2, wg, wu, wd)


def _rope_tables(s):
    pos = jnp.arange(s, dtype=jnp.float32)
    inv = 1.0 / (ROPE_THETA ** (jnp.arange(0, QK_ROPE, 2, dtype=jnp.float32) / QK_ROPE))
    ang = pos[:, None] * inv[None, :]
    return jnp.cos(ang), jnp.sin(ang)


def _pick_tile(n, pref):
    t = min(pref, n)
    while n % t:
        t //= 2
    return t


def kernel(x, c, rel_bias, w_ada, b_ada, g_norm1, w_in, g_q_diff, g_k_diff, lambda_vecs, g_subln, g_q_a, w_q_b, g_kv_a, w_kv_b, g_q_mla, g_k_mla, w_out, g_norm2, w_gate, w_up, w_down):
    b, s, d = x.shape
    depth = w_ada.shape[0]
    diff_width = d // 2
    n_hd = diff_width // DV_DIFF
    n_hm = (d - diff_width) // V_MLA
    assert n_hd == n_hm and n_hd % 2 == 0
    n_heads = n_hd
    wd = n_heads * LANES
    half = QK_ROPE // 2
    tk = _pick_tile(s, ATT_TK)
    nkt = s // tk
    t = _pick_tile(tk, DIFF_TQ)
    assert t + 1 >= REL_MAX_DIST
    tm = _pick_tile(s, 512)

    cos, sin = _rope_tables(s)
    cosq, sinq = jnp.tile(cos, (1, n_heads)), jnp.tile(sin, (1, n_heads))
    cosk = jnp.tile(cos, (1, 4))
    sink = jnp.tile(jnp.concatenate([-sin, sin], axis=1), (1, 2))

    c_pad = jnp.pad(c, ((0, (-b) % 8), (0, 0)))

    for l in range(depth):
        lambda_init = 0.8 - 0.6 * math.exp(-0.3 * l)
        mod = _ada(c_pad, w_ada[l], b_ada[l][None, :])[:b]
        sh1, sc1, gt1, sh2, sc2, gt2 = [m[:, None, :] for m in jnp.split(mod, 6, axis=-1)]

        wi = w_in[l]
        o = 3 * wd + Q_LORA + KV_LORA
        w_kpe = wi[:, o:o + QK_ROPE]
        w_kpe_sw = jnp.concatenate([w_kpe[:, half:], w_kpe[:, :half]], axis=1)
        wqkv = wi[:, :3 * wd].astype(BF16)
        wlat = jnp.concatenate([wi[:, 3 * wd:o], w_kpe, w_kpe, w_kpe_sw, w_kpe_sw], axis=1).astype(BF16)
        wq = w_q_b[l].reshape(Q_LORA, n_heads, QK_HEAD)
        wqb = jnp.concatenate([wq[:, :, :QK_NOPE].reshape(Q_LORA, -1),
                               wq[:, :, QK_NOPE:QK_NOPE + half].reshape(Q_LORA, -1),
                               wq[:, :, QK_NOPE + half:].reshape(Q_LORA, -1)], axis=1).astype(BF16)
        wkv = w_kv_b[l].reshape(KV_LORA, n_heads, QK_NOPE + V_MLA)
        wkvb = jnp.concatenate([wkv[:, :, :QK_NOPE].reshape(KV_LORA, -1),
                                wkv[:, :, QK_NOPE:].reshape(KV_LORA, -1)], axis=1).astype(BF16)
        gq, gk = g_q_mla[l], g_k_mla[l]
        gk_pe = gk[QK_NOPE:]
        gk_pe_sw = jnp.concatenate([gk_pe[half:], gk_pe[:half]])

        qd, kd, vd, qn, qx1, qx2, kn, kp, vm = _in_proj(
            x, sc1, sh1, g_norm1[l][None, :], wqkv, wlat,
            jnp.tile(g_q_diff[l], 2)[None, :], jnp.tile(g_k_diff[l], 2)[None, :],
            g_q_a[l][None, :], wqb, g_kv_a[l][None, :], wkvb,
            gq[None, :QK_NOPE], jnp.tile(gq[QK_NOPE:QK_NOPE + half], n_heads)[None, :],
            jnp.tile(gq[QK_NOPE + half:], n_heads)[None, :],
            gk[None, :QK_NOPE], jnp.tile(gk_pe, 2)[None, :], jnp.tile(gk_pe_sw, 2)[None, :],
            cosq, sinq, cosk, sink, n_heads=n_heads, tm=tm)

        def heads_t(a, w):
            return a.reshape(b, s, n_heads, w).transpose(0, 2, 3, 1)

        def value_t(a, w):
            return a.reshape(b, nkt, tk, n_heads, w).transpose(0, 3, 1, 4, 2)

        qt_d = heads_t(qd, LANES)
        vt_d = value_t(vd, DV_DIFF)
        qt_m = jnp.concatenate([heads_t(qn, QK_NOPE), heads_t(qx1, half), heads_t(qx2, half)], axis=2)
        k_m = jnp.concatenate([kn.reshape(b, s, n_heads, QK_NOPE), kp.reshape(b, s, n_heads, QK_ROPE)],
                              axis=-1).transpose(0, 2, 1, 3)
        vt_m = value_t(vm, V_MLA)

        bias = _bias_tiles(rel_bias, t)
        a_out = _diff_attn(rel_bias, lambda_vecs[l], g_subln[l][:, None], qt_d, kd, vt_d, bias,
                           lambda_init=lambda_init)
        b_out = _mla_attn(qt_m, k_m, vt_m)

        wo = w_out[l].astype(BF16)
        x = _out_proj(x, gt1, a_out, b_out, wo[:diff_width], wo[diff_width:], tm=tm)
        x = _ffn(x, sc2, sh2, gt2, g_norm2[l][None, :], w_gate[l].astype(BF16), w_up[l].astype(BF16),
                 w_down[l].astype(BF16), tm=tm, tf=_pick_tile(w_gate.shape[-1], 512))
    return x
```

```python
import functools
import math

import jax
import jax.numpy as jnp
from jax import lax
from jax.experimental import pallas as pl
from jax.experimental.pallas import tpu as pltpu

F32 = jnp.float32
BF16 = jnp.bfloat16

DH_DIFF = 64
DV_DIFF = 2 * DH_DIFF
QK_NOPE = 128
QK_ROPE = 64
QK_HEAD = QK_NOPE + QK_ROPE
V_MLA = 128
Q_LORA = 512
KV_LORA = 256
ROPE_THETA = 10000.0
REL_BUCKETS = 32
REL_MAX_DIST = 128
EPS = 1e-6
LOG2E = 1.4426950408889634

LANES = 128
VMEM_LIMIT_BYTES = 56 * 1024 * 1024

ATT_TK = 512
DIFF_TQ = 512
MLA_TQ = 512
KEY_CHUNK = 256
BOUNDED_DEPTH = 2
SCORE_BOUND = 40.0


def _cparams(sem):
    return pltpu.CompilerParams(dimension_semantics=sem, vmem_limit_bytes=VMEM_LIMIT_BYTES)


def _const_spec(shape):
    nd = len(shape)
    return pl.BlockSpec(shape, lambda *_: (0,) * nd, pipeline_mode=pl.Buffered(1))


def _ada_kernel(c_ref, w_ref, b_ref, o_ref):
    c = c_ref[...]
    ca = c * jax.nn.sigmoid(c)
    o_ref[...] = jnp.dot(ca, w_ref[...], preferred_element_type=F32,
                         precision=lax.Precision.HIGHEST) + b_ref[...]


def _ada(c_pad, w, b, tn=1024):
    m, d = c_pad.shape
    n = w.shape[1]
    return pl.pallas_call(
        _ada_kernel,
        grid=(n // tn,),
        in_specs=[pl.BlockSpec((m, d), lambda j: (0, 0)),
                  pl.BlockSpec((d, tn), lambda j: (0, j)),
                  pl.BlockSpec((1, tn), lambda j: (0, j))],
        out_specs=pl.BlockSpec((m, tn), lambda j: (0, j)),
        out_shape=jax.ShapeDtypeStruct((m, n), F32),
        compiler_params=_cparams(("arbitrary",)),
        name="ada",
    )(c_pad, w, b)


def _modulated_norm(x, g, sc, sh):
    ms = jnp.mean(x * x, axis=-1, keepdims=True)
    return (x * lax.rsqrt(ms + EPS) * g) * (1.0 + sc) + sh


def _half_lane_norm(blk, g2, lo_mask, out_scale):
    sq = blk * blk
    s_lo = jnp.sum(jnp.where(lo_mask, sq, 0.0), axis=-1, keepdims=True)
    s_hi = jnp.sum(jnp.where(lo_mask, 0.0, sq), axis=-1, keepdims=True)
    inv = jnp.where(lo_mask, lax.rsqrt(s_lo * (1.0 / DH_DIFF) + EPS),
                    lax.rsqrt(s_hi * (1.0 / DH_DIFF) + EPS))
    return blk * inv * (g2 * out_scale)


def _in_proj_kernel(x_ref, sc_ref, sh_ref, g1_ref, wqkv_ref, wlat_ref, gqd_ref, gkd_ref,
                    gqa_ref, wqb_ref, gkva_ref, wkvb_ref, gqn_ref, gq1_ref, gq2_ref,
                    gkn_ref, gkp_ref, gkps_ref, cosq_ref, sinq_ref, cosk_ref, sink_ref,
                    qd_ref, kd_ref, vd_ref, qn_ref, qx1_ref, qx2_ref, kn_ref, kp_ref, vm_ref,
                    *, n_heads):
    h = _modulated_norm(x_ref[0], g1_ref[...], sc_ref[0], sh_ref[0]).astype(BF16)
    tm = h.shape[0]
    wd = n_heads * LANES
    lane = lax.broadcasted_iota(jnp.int32, (tm, LANES), 1)
    lo_mask = lane < DH_DIFF

    qscale = DH_DIFF ** -0.5 * LOG2E
    qd = jnp.dot(h, wqkv_ref[:, 0:wd], preferred_element_type=F32)
    for hh in range(n_heads):
        sl = slice(hh * LANES, (hh + 1) * LANES)
        qd_ref[0, :, sl] = _half_lane_norm(qd[:, sl], gqd_ref[...], lo_mask, qscale).astype(BF16)
    kd = jnp.dot(h, wqkv_ref[:, wd:2 * wd], preferred_element_type=F32)
    for hh in range(n_heads):
        sl = slice(hh * LANES, (hh + 1) * LANES)
        kd_ref[0, :, sl] = _half_lane_norm(kd[:, sl], gkd_ref[...], lo_mask, 1.0).astype(BF16)
    vd_ref[0] = jnp.dot(h, wqkv_ref[:, 2 * wd:3 * wd], preferred_element_type=F32).astype(BF16)

    lat = jnp.dot(h, wlat_ref[...], preferred_element_type=F32)
    cq = lat[:, 0:Q_LORA]
    ckv = lat[:, Q_LORA:Q_LORA + KV_LORA]
    o = Q_LORA + KV_LORA
    kpe2 = lat[:, o:o + LANES]
    kpes2 = lat[:, o + LANES:o + 2 * LANES]

    def rms(v, g):
        return v * lax.rsqrt(jnp.mean(v * v, axis=-1, keepdims=True) + EPS) * g

    kv = jnp.dot(rms(ckv, gkva_ref[...]).astype(BF16), wkvb_ref[...], preferred_element_type=F32)
    vm_ref[0] = kv[:, wd:2 * wd].astype(BF16)
    ss_pe = jnp.sum(jnp.where(lo_mask, kpe2 * kpe2, 0.0), axis=-1, keepdims=True)
    kr2 = kpe2 * gkp_ref[...] * cosk_ref[...] + kpes2 * gkps_ref[...] * sink_ref[...]
    r_heads = []
    for hh in range(n_heads):
        sl = slice(hh * LANES, (hh + 1) * LANES)
        kn = kv[:, sl]
        ss = jnp.sum(kn * kn, axis=-1, keepdims=True) + ss_pe
        r = lax.rsqrt(ss * (1.0 / QK_HEAD) + EPS)
        r_heads.append(r)
        kn_ref[0, :, sl] = (kn * r * gkn_ref[...]).astype(BF16)
    for p in range(n_heads // 2):
        rr = jnp.where(lo_mask, r_heads[2 * p], r_heads[2 * p + 1])
        kp_ref[0, :, p * LANES:(p + 1) * LANES] = (kr2 * rr).astype(BF16)

    qm = jnp.dot(rms(cq, gqa_ref[...]).astype(BF16), wqb_ref[...], preferred_element_type=F32)
    half = QK_ROPE // 2
    wx = n_heads * half
    x1 = qm[:, wd:wd + wx]
    x2 = qm[:, wd + wx:wd + 2 * wx]
    lane_x = lax.broadcasted_iota(jnp.int32, (tm, wx), 1) // half
    sq_x = x1 * x1 + x2 * x2
    mscale = QK_HEAD ** -0.5 * LOG2E
    inv_x = jnp.zeros((tm, wx), F32)
    for hh in range(n_heads):
        sl = slice(hh * LANES, (hh + 1) * LANES)
        qn = qm[:, sl]
        ss = (jnp.sum(qn * qn, axis=-1, keepdims=True)
              + jnp.sum(jnp.where(lane_x == hh, sq_x, 0.0), axis=-1, keepdims=True))
        r = lax.rsqrt(ss * (1.0 / QK_HEAD) + EPS) * mscale
        qn_ref[0, :, sl] = (qn * r * gqn_ref[...]).astype(BF16)
        inv_x = jnp.where(lane_x == hh, r, inv_x)
    a1 = x1 * gq1_ref[...]
    a2 = x2 * gq2_ref[...]
    cq_t = cosq_ref[...]
    sq_t = sinq_ref[...]
    qx1_ref[0] = ((a1 * cq_t - a2 * sq_t) * inv_x).astype(BF16)
    qx2_ref[0] = ((a2 * cq_t + a1 * sq_t) * inv_x).astype(BF16)


def _in_proj(x, sc1, sh1, g1, wqkv, wlat, gqd, gkd, gqa, wqb, gkva, wkvb, gqn, gq1, gq2,
             gkn, gkp, gkps, cosq, sinq, cosk, sink, *, n_heads, tm):
    b, s, d = x.shape
    wd = n_heads * LANES
    wx = n_heads * (QK_ROPE // 2)
    row = lambda w: pl.BlockSpec((1, tm, w), lambda bi, i: (bi, i, 0))
    mod = pl.BlockSpec((1, 1, d), lambda bi, i: (bi, 0, 0))
    tab = lambda w: pl.BlockSpec((tm, w), lambda bi, i: (i, 0))
    consts = [g1, wqkv, wlat, gqd, gkd, gqa, wqb, gkva, wkvb, gqn, gq1, gq2, gkn, gkp, gkps]
    out_widths = [wd, wd, wd, wd, wx, wx, wd, wd // 2, wd]
    return pl.pallas_call(
        functools.partial(_in_proj_kernel, n_heads=n_heads),
        grid=(b, s // tm),
        in_specs=[row(d), mod, mod] + [_const_spec(a.shape) for a in consts]
                 + [tab(wx), tab(wx), tab(LANES), tab(LANES)],
        out_specs=[row(w) for w in out_widths],
        out_shape=[jax.ShapeDtypeStruct((b, s, w), BF16) for w in out_widths],
        compiler_params=_cparams(("parallel", "parallel")),
        name="in_proj",
    )(x, sc1, sh1, *consts, cosq, sinq, cosk, sink)


def _t5_bucket(rel):
    nb = REL_BUCKETS // 2
    max_exact = nb // 2
    base = jnp.where(rel > 0, nb, 0)
    n = jnp.abs(rel)
    nf = jnp.maximum(n, 1).astype(jnp.float32)
    large = max_exact + (jnp.log(nf / max_exact) / math.log(REL_MAX_DIST / max_exact)
                         * (nb - max_exact)).astype(jnp.int32)
    large = jnp.minimum(large, nb - 1)
    return base + jnp.where(n < max_exact, n, large)


def _bias_kernel(rb_ref, bucket_ref, o_ref):
    hh = pl.program_id(0)
    t = bucket_ref.shape[-1]
    far_left = rb_ref[REL_BUCKETS // 2 - 1, hh]
    far_right = rb_ref[REL_BUCKETS - 1, hh]
    o_ref[0, 0] = jnp.full((t, t), far_left, F32) * LOG2E
    o_ref[0, 4] = jnp.full((t, t), far_right, F32) * LOG2E
    for d in range(3):
        bt = bucket_ref[d]
        acc = jnp.zeros((t, t), F32)
        for bk in range(REL_BUCKETS):
            acc = jnp.where(bt == bk, rb_ref[bk, hh], acc)
        o_ref[0, d + 1] = acc * LOG2E


def _bias_tiles(rel_bias, t):
    n_heads = rel_bias.shape[1]
    j = jnp.arange(t, dtype=jnp.int32)[:, None]
    i = jnp.arange(t, dtype=jnp.int32)[None, :]
    buckets = jnp.stack([_t5_bucket(dd * t + j - i) for dd in (-1, 0, 1)])
    return pl.pallas_call(
        _bias_kernel,
        grid=(n_heads,),
        in_specs=[pl.BlockSpec(memory_space=pltpu.SMEM),
                  pl.BlockSpec((3, t, t), lambda hh: (0, 0, 0))],
        out_specs=pl.BlockSpec((1, 5, t, t), lambda hh: (hh, 0, 0, 0)),
        out_shape=jax.ShapeDtypeStruct((n_heads, 5, t, t), F32),
        compiler_params=_cparams(("arbitrary",)),
        name="rel_bias_tiles",
    )(rel_bias, buckets)


ONES_ROWS = 16


def _softmax_state_step(s, offset, vt, state):
    m_new, alpha, p = _softmax_weights(s, offset, state[0])
    return m_new, _accumulate(alpha, state[1], vt, p)


def _softmax_weights(s, offset, m_old):
    m_new = jnp.maximum(m_old, jnp.max(s, axis=0, keepdims=True) + offset)
    alpha = jnp.exp2(m_old - m_new)
    p = jnp.exp2(s - (m_new - offset)).astype(BF16)
    return m_new, alpha, p


def _accumulate(alpha, acc_old, vt, p):
    return alpha * acc_old + jnp.dot(vt, p, preferred_element_type=F32)


def _init_state(dv, t):
    return (jnp.full((1, t), -jnp.inf, F32), jnp.zeros((dv + ONES_ROWS, t), F32))


def _with_ones(vt):
    return jnp.concatenate([vt, jnp.ones((ONES_ROWS, vt.shape[1]), vt.dtype)], axis=0)


def _normalised(state, dv):
    _, acc = state
    return acc[:dv] / acc[dv:dv + 1]


def _diff_attn_kernel(rb_ref, lamv_ref, gsub_ref, qt_ref, k_ref, vt_ref, bias_ref, o_ref, *,
                      lambda_init, bounded):
    hh = pl.program_id(1)
    qi = pl.program_id(2)
    t = qt_ref.shape[-1]
    nkt = vt_ref.shape[2]

    qt = qt_ref[0, 0]
    row = lax.broadcasted_iota(jnp.int32, qt.shape, 0)
    zero = jnp.zeros_like(qt)
    qts = (jnp.where(row < DH_DIFF, qt, zero), jnp.where(row < DH_DIFF, zero, qt))

    off_left = rb_ref[REL_BUCKETS // 2 - 1, hh] * LOG2E
    off_right = rb_ref[REL_BUCKETS - 1, hh] * LOG2E

    n_near = min(nkt, 3)

    def tile_index(r):
        return lax.rem(qi - 1 + r + nkt, nkt)

    def far_offset(r):
        return jnp.where(tile_index(r) < qi, off_left, off_right)

    def biased_scores(r, mp, k0, ck):
        kj = tile_index(r)
        k_t = k_ref[0, pl.ds(pl.multiple_of(kj * t + k0, ck), ck), :]
        s = jnp.dot(k_t, qts[mp], preferred_element_type=F32)
        if r < n_near:
            s = s + bias_ref[0, jnp.clip(kj - qi, -2, 2) + 2, k0:k0 + ck, :]
        elif bounded:
            s = s + far_offset(r)
        return s

    if bounded:
        ck = min(t, KEY_CHUNK)
        npt = t // ck
        items = [(r, c, mp) for r in range(nkt) for c in range(npt) for mp in range(2)]
        score = lambda it: biased_scores(it[0], it[2], it[1] * ck, ck)
        accs = [jnp.zeros((DV_DIFF + ONES_ROWS, t), F32) for _ in range(2)]
        depth = min(BOUNDED_DEPTH, len(items))
        pending = [score(it) for it in items[:depth]]
        for n, (r, c, mp) in enumerate(items):
            s = pending.pop(0)
            if n + depth < len(items):
                pending.append(score(items[n + depth]))
            vt = _with_ones(vt_ref[0, 0, tile_index(r), :, c * ck:(c + 1) * ck])
            accs[mp] = accs[mp] + jnp.dot(vt, jnp.exp2(s).astype(BF16), preferred_element_type=F32)
        states = [(None, acc) for acc in accs]
    else:
        states = [_init_state(DV_DIFF, t), _init_state(DV_DIFF, t)]
        score = lambda r: [biased_scores(r, mp, 0, t) for mp in range(2)]
        s_next = score(0)
        for r in range(nkt):
            s_cur = s_next
            if r + 1 < nkt:
                s_next = score(r + 1)
            vt = _with_ones(vt_ref[0, 0, tile_index(r)])
            offset = 0.0 if r < n_near else far_offset(r)
            for mp in range(2):
                states[mp] = _softmax_state_step(s_cur[mp], offset, vt, states[mp])

    lv = lamv_ref[...]
    lam = (jnp.exp(jnp.sum(lv[0:1] * lv[1:2], axis=-1, keepdims=True))
           - jnp.exp(jnp.sum(lv[2:3] * lv[3:4], axis=-1, keepdims=True)) + lambda_init)
    o = _normalised(states[0], DV_DIFF) - lam * _normalised(states[1], DV_DIFF)
    ms = jnp.mean(o * o, axis=0, keepdims=True)
    y = o * lax.rsqrt(ms + EPS) * (gsub_ref[...] * (1.0 - lambda_init))
    o_ref[0] = y.T.astype(BF16)


def _diff_attn(rel_bias, lamv, gsub_col, qt, k, vt, bias, *, lambda_init, bounded):
    b, n_heads, _, s = qt.shape
    t = bias.shape[-1]
    nkt, tk = vt.shape[2], vt.shape[4]
    assert tk == t
    return pl.pallas_call(
        functools.partial(_diff_attn_kernel, lambda_init=lambda_init, bounded=bounded),
        grid=(b, n_heads, s // t),
        in_specs=[pl.BlockSpec(memory_space=pltpu.SMEM),
                  pl.BlockSpec(lamv.shape, lambda bi, hh, qi: (0, 0)),
                  pl.BlockSpec(gsub_col.shape, lambda bi, hh, qi: (0, 0)),
                  pl.BlockSpec((1, 1, LANES, t), lambda bi, hh, qi: (bi, hh, 0, qi)),
                  pl.BlockSpec((1, s, LANES), lambda bi, hh, qi: (bi, 0, hh)),
                  pl.BlockSpec((1, 1, nkt, DV_DIFF, tk), lambda bi, hh, qi: (bi, hh, 0, 0, 0)),
                  pl.BlockSpec((1, 5, t, t), lambda bi, hh, qi: (hh, 0, 0, 0))],
        out_specs=pl.BlockSpec((1, t, DV_DIFF), lambda bi, hh, qi: (bi, qi, hh)),
        out_shape=jax.ShapeDtypeStruct((b, s, n_heads * DV_DIFF), BF16),
        compiler_params=_cparams(("parallel", "parallel", "parallel")),
        name="diff_attn",
    )(rel_bias, lamv, gsub_col, qt, k, vt, bias)


def _mla_attn_kernel(qt_ref, k_ref, vt_ref, o_ref):
    nkt = vt_ref.shape[2]
    tk = vt_ref.shape[-1]
    qt = qt_ref[0, 0]
    m, acc = _init_state(V_MLA, qt.shape[-1])
    scores = lambda kj: jnp.dot(k_ref[0, 0, kj * tk:(kj + 1) * tk, :], qt, preferred_element_type=F32)
    depth = 2
    pending = [scores(kj) for kj in range(min(depth, nkt))]
    for kj in range(nkt):
        s = pending.pop(0)
        if kj + depth < nkt:
            pending.append(scores(kj + depth))
        m, alpha, p = _softmax_weights(s, 0.0, m)
        acc = _accumulate(alpha, acc, _with_ones(vt_ref[0, 0, kj]), p)
    o_ref[0] = _normalised((m, acc), V_MLA).T.astype(BF16)


def _mla_attn_bounded_kernel(qt_ref, k_ref, vt_ref, o_ref):
    nkt = vt_ref.shape[2]
    tk = vt_ref.shape[-1]
    qt = qt_ref[0, 0]
    acc = jnp.zeros((V_MLA + ONES_ROWS, qt.shape[-1]), F32)
    ck = min(tk, KEY_CHUNK)
    npt = tk // ck
    n_chunks = nkt * npt
    scores = lambda i: jnp.dot(k_ref[0, 0, i * ck:(i + 1) * ck, :], qt, preferred_element_type=F32)
    depth = min(BOUNDED_DEPTH, n_chunks)
    pending = [scores(i) for i in range(depth)]
    for i in range(n_chunks):
        s = pending.pop(0)
        if i + depth < n_chunks:
            pending.append(scores(i + depth))
        c = i % npt
        acc = acc + jnp.dot(_with_ones(vt_ref[0, 0, i // npt, :, c * ck:(c + 1) * ck]),
                            jnp.exp2(s).astype(BF16), preferred_element_type=F32)
    o_ref[0] = _normalised((None, acc), V_MLA).T.astype(BF16)


def _mla_attn(qt, k, vt, *, bounded):
    b, n_heads, _, s = qt.shape
    t = _pick_tile(s, MLA_TQ)
    nkt, tk = vt.shape[2], vt.shape[4]
    return pl.pallas_call(
        _mla_attn_bounded_kernel if bounded else _mla_attn_kernel,
        grid=(b, n_heads, s // t),
        in_specs=[pl.BlockSpec((1, 1, QK_HEAD, t), lambda bi, hh, qi: (bi, hh, 0, qi)),
                  pl.BlockSpec((1, 1, s, QK_HEAD), lambda bi, hh, qi: (bi, hh, 0, 0)),
                  pl.BlockSpec((1, 1, nkt, V_MLA, tk), lambda bi, hh, qi: (bi, hh, 0, 0, 0))],
        out_specs=pl.BlockSpec((1, t, V_MLA), lambda bi, hh, qi: (bi, qi, hh)),
        out_shape=jax.ShapeDtypeStruct((b, s, n_heads * V_MLA), BF16),
        compiler_params=_cparams(("parallel", "parallel", "parallel")),
        name="mla_attn",
    )(qt, k, vt)


def _out_proj_kernel(x_ref, gt_ref, a_ref, b_ref, wa_ref, wb_ref, o_ref):
    mix = (jnp.dot(a_ref[0], wa_ref[...], preferred_element_type=F32)
           + jnp.dot(b_ref[0], wb_ref[...], preferred_element_type=F32))
    o_ref[0] = x_ref[0] + gt_ref[0] * mix


def _out_proj(x, gt1, a, bb, wa, wb, *, tm):
    b, s, d = x.shape
    row = lambda w: pl.BlockSpec((1, tm, w), lambda bi, i: (bi, i, 0))
    return pl.pallas_call(
        _out_proj_kernel,
        grid=(b, s // tm),
        in_specs=[row(d), pl.BlockSpec((1, 1, d), lambda bi, i: (bi, 0, 0)),
                  row(a.shape[-1]), row(bb.shape[-1]), _const_spec(wa.shape), _const_spec(wb.shape)],
        out_specs=row(d),
        out_shape=jax.ShapeDtypeStruct((b, s, d), F32),
        compiler_params=_cparams(("parallel", "parallel")),
        name="out_proj",
    )(x, gt1, a, bb, wa, wb)


def _ffn_kernel(x_ref, sc_ref, sh_ref, gt_ref, g2_ref, wg_ref, wu_ref, wd_ref, o_ref, h_ref, acc_ref):
    j = pl.program_id(2)

    @pl.when(j == 0)
    def _():
        h_ref[...] = _modulated_norm(x_ref[0], g2_ref[...], sc_ref[0], sh_ref[0]).astype(BF16)
        acc_ref[...] = jnp.zeros(acc_ref.shape, F32)

    h = h_ref[...]
    g = jnp.dot(h, wg_ref[...], preferred_element_type=F32)
    u = jnp.dot(h, wu_ref[...], preferred_element_type=F32)
    act = (g * jax.nn.sigmoid(g) * u).astype(BF16)
    acc_ref[...] += jnp.dot(act, wd_ref[...], preferred_element_type=F32)

    @pl.when(j == pl.num_programs(2) - 1)
    def _():
        o_ref[0] = x_ref[0] + gt_ref[0] * acc_ref[...]


def _ffn(x, sc2, sh2, gt2, g2, wg, wu, wd, *, tm, tf):
    b, s, d = x.shape
    dff = wg.shape[1]
    row = pl.BlockSpec((1, tm, d), lambda bi, i, j: (bi, i, 0))
    mod = pl.BlockSpec((1, 1, d), lambda bi, i, j: (bi, 0, 0))
    return pl.pallas_call(
        _ffn_kernel,
        grid=(b, s // tm, dff // tf),
        in_specs=[row, mod, mod, mod, pl.BlockSpec((1, d), lambda bi, i, j: (0, 0)),
                  pl.BlockSpec((d, tf), lambda bi, i, j: (0, j)),
                  pl.BlockSpec((d, tf), lambda bi, i, j: (0, j)),
                  pl.BlockSpec((tf, d), lambda bi, i, j: (j, 0))],
        out_specs=row,
        out_shape=jax.ShapeDtypeStruct((b, s, d), F32),
        scratch_shapes=[pltpu.VMEM((tm, d), BF16), pltpu.VMEM((tm, d), F32)],
        compiler_params=_cparams(("parallel", "parallel", "arbitrary")),
        name="ffn",
    )(x, sc2, sh2, gt2, g2, wg, wu, wd)


def _rope_tables(s):
    pos = jnp.arange(s, dtype=jnp.float32)
    inv = 1.0 / (ROPE_THETA ** (jnp.arange(0, QK_ROPE, 2, dtype=jnp.float32) / QK_ROPE))
    ang = pos[:, None] * inv[None, :]
    return jnp.cos(ang), jnp.sin(ang)


def _pick_tile(n, pref):
    t = min(pref, n)
    while n % t:
        t //= 2
    return t


def kernel(x, c, rel_bias, w_ada, b_ada, g_norm1, w_in, g_q_diff, g_k_diff, lambda_vecs, g_subln, g_q_a, w_q_b, g_kv_a, w_kv_b, g_q_mla, g_k_mla, w_out, g_norm2, w_gate, w_up, w_down):
    b, s, d = x.shape
    depth = w_ada.shape[0]
    diff_width = d // 2
    n_hd = diff_width // DV_DIFF
    n_hm = (d - diff_width) // V_MLA
    assert n_hd == n_hm and n_hd % 2 == 0
    n_heads = n_hd
    wd = n_heads * LANES
    half = QK_ROPE // 2
    tk = _pick_tile(s, ATT_TK)
    nkt = s // tk
    t = tk
    assert t + 1 >= REL_MAX_DIST
    tm = _pick_tile(s, 512)

    cos, sin = _rope_tables(s)
    cosq, sinq = jnp.tile(cos, (1, n_heads)), jnp.tile(sin, (1, n_heads))
    cosk = jnp.tile(cos, (1, 4))
    sink = jnp.tile(jnp.concatenate([-sin, sin], axis=1), (1, 2))

    c_pad = jnp.pad(c, ((0, (-b) % 8), (0, 0)))

    for l in range(depth):
        lambda_init = 0.8 - 0.6 * math.exp(-0.3 * l)
        mod = _ada(c_pad, w_ada[l], b_ada[l][None, :])[:b]
        sh1, sc1, gt1, sh2, sc2, gt2 = [m[:, None, :] for m in jnp.split(mod, 6, axis=-1)]

        wi = w_in[l]
        o = 3 * wd + Q_LORA + KV_LORA
        w_kpe = wi[:, o:o + QK_ROPE]
        w_kpe_sw = jnp.concatenate([w_kpe[:, half:], w_kpe[:, :half]], axis=1)
        wqkv = wi[:, :3 * wd].astype(BF16)
        wlat = jnp.concatenate([wi[:, 3 * wd:o], w_kpe, w_kpe, w_kpe_sw, w_kpe_sw], axis=1).astype(BF16)
        wq = w_q_b[l].reshape(Q_LORA, n_heads, QK_HEAD)
        wqb = jnp.concatenate([wq[:, :, :QK_NOPE].reshape(Q_LORA, -1),
                               wq[:, :, QK_NOPE:QK_NOPE + half].reshape(Q_LORA, -1),
                               wq[:, :, QK_NOPE + half:].reshape(Q_LORA, -1)], axis=1).astype(BF16)
        wkv = w_kv_b[l].reshape(KV_LORA, n_heads, QK_NOPE + V_MLA)
        wkvb = jnp.concatenate([wkv[:, :, :QK_NOPE].reshape(KV_LORA, -1),
                                wkv[:, :, QK_NOPE:].reshape(KV_LORA, -1)], axis=1).astype(BF16)
        gq, gk = g_q_mla[l], g_k_mla[l]
        gk_pe = gk[QK_NOPE:]
        gk_pe_sw = jnp.concatenate([gk_pe[half:], gk_pe[:half]])

        qd, kd, vd, qn, qx1, qx2, kn, kp, vm = _in_proj(
            x, sc1, sh1, g_norm1[l][None, :], wqkv, wlat,
            jnp.tile(g_q_diff[l], 2)[None, :], jnp.tile(g_k_diff[l], 2)[None, :],
            g_q_a[l][None, :], wqb, g_kv_a[l][None, :], wkvb,
            gq[None, :QK_NOPE], jnp.tile(gq[QK_NOPE:QK_NOPE + half], n_heads)[None, :],
            jnp.tile(gq[QK_NOPE + half:], n_heads)[None, :],
            gk[None, :QK_NOPE], jnp.tile(gk_pe, 2)[None, :], jnp.tile(gk_pe_sw, 2)[None, :],
            cosq, sinq, cosk, sink, n_heads=n_heads, tm=tm)

        def heads_t(a, w):
            return a.reshape(b, s, n_heads, w).transpose(0, 2, 3, 1)

        def value_t(a, w):
            return a.reshape(b, nkt, tk, n_heads, w).transpose(0, 3, 1, 4, 2)

        qt_d = heads_t(qd, LANES)
        vt_d = value_t(vd, DV_DIFF)
        qt_m = jnp.concatenate([heads_t(qn, QK_NOPE), heads_t(qx1, half), heads_t(qx2, half)], axis=2)
        k_m = jnp.concatenate([kn.reshape(b, s, n_heads, QK_NOPE), kp.reshape(b, s, n_heads, QK_ROPE)],
                              axis=-1).transpose(0, 2, 1, 3)
        vt_m = value_t(vm, V_MLA)

        bias = _bias_tiles(rel_bias, t)
        gmax = lambda g: jnp.max(jnp.abs(g))
        bound_d = (1.02 * LOG2E * DH_DIFF ** 0.5 * gmax(g_q_diff[l]) * gmax(g_k_diff[l])
                   + LOG2E * gmax(rel_bias))
        bound_m = 1.02 * LOG2E * QK_HEAD ** 0.5 * gmax(g_q_mla[l]) * gmax(g_k_mla[l])
        diff_args = (rel_bias, lambda_vecs[l], g_subln[l][:, None], qt_d, kd, vt_d, bias)
        a_out = lax.cond(
            bound_d <= SCORE_BOUND,
            lambda *a: _diff_attn(*a, lambda_init=lambda_init, bounded=True),
            lambda *a: _diff_attn(*a, lambda_init=lambda_init, bounded=False), *diff_args)
        b_out = lax.cond(
            bound_m <= SCORE_BOUND,
            lambda *a: _mla_attn(*a, bounded=True),
            lambda *a: _mla_attn(*a, bounded=False), qt_m, k_m, vt_m)

        wo = w_out[l].astype(BF16)
        x = _out_proj(x, gt1, a_out, b_out, wo[:diff_width], wo[diff_width:], tm=tm)
        x = _ffn(x, sc2, sh2, gt2, g_norm2[l][None, :], w_gate[l].astype(BF16), w_up[l].astype(BF16),
                 w_down[l].astype(BF16), tm=tm, tf=_pick_tile(w_gate.shape[-1], 512))
    return x
```

```python
import functools
import math

import jax
import jax.numpy as jnp
from jax import lax
from jax.experimental import pallas as pl
from jax.experimental.pallas import tpu as pltpu

F32 = jnp.float32
BF16 = jnp.bfloat16

DH_DIFF = 64
DV_DIFF = 2 * DH_DIFF
QK_NOPE = 128
QK_ROPE = 64
QK_HEAD = QK_NOPE + QK_ROPE
V_MLA = 128
Q_LORA = 512
KV_LORA = 256
ROPE_THETA = 10000.0
REL_BUCKETS = 32
REL_MAX_DIST = 128
EPS = 1e-6
LOG2E = 1.4426950408889634

LANES = 128
SUBLANES = 8
VMEM_LIMIT_BYTES = 56 * 1024 * 1024

ATT_TK = 512
DIFF_TQ = 512
MLA_TQ = 512
KEY_CHUNK = 256
BOUNDED_DEPTH = 2
BOUNDED_Q_TILES = 4
SCORE_BOUND = 40.0


def _cparams(sem):
    return pltpu.CompilerParams(dimension_semantics=sem, vmem_limit_bytes=VMEM_LIMIT_BYTES)


def _const_spec(shape):
    nd = len(shape)
    return pl.BlockSpec(shape, lambda *_: (0,) * nd, pipeline_mode=pl.Buffered(1))


def _ada_kernel(c_ref, w_ref, b_ref, o_ref):
    c = c_ref[...]
    ca = c * jax.nn.sigmoid(c)
    o_ref[...] = jnp.dot(ca, w_ref[...], preferred_element_type=F32,
                         precision=lax.Precision.HIGHEST) + b_ref[...]


def _ada(c_pad, w, b, tn=1024):
    m, d = c_pad.shape
    n = w.shape[1]
    return pl.pallas_call(
        _ada_kernel,
        grid=(n // tn,),
        in_specs=[pl.BlockSpec((m, d), lambda j: (0, 0)),
                  pl.BlockSpec((d, tn), lambda j: (0, j)),
                  pl.BlockSpec((1, tn), lambda j: (0, j))],
        out_specs=pl.BlockSpec((m, tn), lambda j: (0, j)),
        out_shape=jax.ShapeDtypeStruct((m, n), F32),
        compiler_params=_cparams(("arbitrary",)),
        name="ada",
    )(c_pad, w, b)


def _modulated_norm(x, g, sc, sh):
    ms = jnp.mean(x * x, axis=-1, keepdims=True)
    return (x * lax.rsqrt(ms + EPS) * g) * (1.0 + sc) + sh


def _half_lane_norm(blk, g2, lo_mask, out_scale):
    sq = blk * blk
    s_lo = jnp.sum(jnp.where(lo_mask, sq, 0.0), axis=-1, keepdims=True)
    s_hi = jnp.sum(jnp.where(lo_mask, 0.0, sq), axis=-1, keepdims=True)
    inv = jnp.where(lo_mask, lax.rsqrt(s_lo * (1.0 / DH_DIFF) + EPS),
                    lax.rsqrt(s_hi * (1.0 / DH_DIFF) + EPS))
    return blk * inv * (g2 * out_scale)


def _in_proj_kernel(x_ref, sc_ref, sh_ref, g1_ref, wqkv_ref, wlat_ref, gqd_ref, gkd_ref,
                    gqa_ref, wqb_ref, gkva_ref, wkvb_ref, gqn_ref, gq1_ref, gq2_ref,
                    gkn_ref, gkp_ref, gkps_ref, cosq_ref, sinq_ref, cosk_ref, sink_ref,
                    qd_ref, kd_ref, vd_ref, qn_ref, qx1_ref, qx2_ref, kn_ref, kp_ref, vm_ref,
                    *, n_heads):
    h = _modulated_norm(x_ref[0], g1_ref[...], sc_ref[0], sh_ref[0]).astype(BF16)
    tm = h.shape[0]
    wd = n_heads * LANES
    lane = lax.broadcasted_iota(jnp.int32, (tm, LANES), 1)
    lo_mask = lane < DH_DIFF

    qscale = DH_DIFF ** -0.5 * LOG2E
    qd = jnp.dot(h, wqkv_ref[:, 0:wd], preferred_element_type=F32)
    for hh in range(n_heads):
        sl = slice(hh * LANES, (hh + 1) * LANES)
        qd_ref[0, :, sl] = _half_lane_norm(qd[:, sl], gqd_ref[...], lo_mask, qscale).astype(BF16)
    kd = jnp.dot(h, wqkv_ref[:, wd:2 * wd], preferred_element_type=F32)
    for hh in range(n_heads):
        sl = slice(hh * LANES, (hh + 1) * LANES)
        kd_ref[0, :, sl] = _half_lane_norm(kd[:, sl], gkd_ref[...], lo_mask, 1.0).astype(BF16)
    vd_ref[0] = jnp.dot(h, wqkv_ref[:, 2 * wd:3 * wd], preferred_element_type=F32).astype(BF16)

    lat = jnp.dot(h, wlat_ref[...], preferred_element_type=F32)
    cq = lat[:, 0:Q_LORA]
    ckv = lat[:, Q_LORA:Q_LORA + KV_LORA]
    o = Q_LORA + KV_LORA
    kpe2 = lat[:, o:o + LANES]
    kpes2 = lat[:, o + LANES:o + 2 * LANES]

    def rms(v, g):
        return v * lax.rsqrt(jnp.mean(v * v, axis=-1, keepdims=True) + EPS) * g

    kv = jnp.dot(rms(ckv, gkva_ref[...]).astype(BF16), wkvb_ref[...], preferred_element_type=F32)
    vm_ref[0] = kv[:, wd:2 * wd].astype(BF16)
    ss_pe = jnp.sum(jnp.where(lo_mask, kpe2 * kpe2, 0.0), axis=-1, keepdims=True)
    kr2 = kpe2 * gkp_ref[...] * cosk_ref[...] + kpes2 * gkps_ref[...] * sink_ref[...]
    r_heads = []
    for hh in range(n_heads):
        sl = slice(hh * LANES, (hh + 1) * LANES)
        kn = kv[:, sl]
        ss = jnp.sum(kn * kn, axis=-1, keepdims=True) + ss_pe
        r = lax.rsqrt(ss * (1.0 / QK_HEAD) + EPS)
        r_heads.append(r)
        kn_ref[0, :, sl] = (kn * r * gkn_ref[...]).astype(BF16)
    for p in range(n_heads // 2):
        rr = jnp.where(lo_mask, r_heads[2 * p], r_heads[2 * p + 1])
        kp_ref[0, :, p * LANES:(p + 1) * LANES] = (kr2 * rr).astype(BF16)

    qm = jnp.dot(rms(cq, gqa_ref[...]).astype(BF16), wqb_ref[...], preferred_element_type=F32)
    half = QK_ROPE // 2
    wx = n_heads * half
    x1 = qm[:, wd:wd + wx]
    x2 = qm[:, wd + wx:wd + 2 * wx]
    lane_x = lax.broadcasted_iota(jnp.int32, (tm, wx), 1) // half
    sq_x = x1 * x1 + x2 * x2
    mscale = QK_HEAD ** -0.5 * LOG2E
    inv_x = jnp.zeros((tm, wx), F32)
    for hh in range(n_heads):
        sl = slice(hh * LANES, (hh + 1) * LANES)
        qn = qm[:, sl]
        ss = (jnp.sum(qn * qn, axis=-1, keepdims=True)
              + jnp.sum(jnp.where(lane_x == hh, sq_x, 0.0), axis=-1, keepdims=True))
        r = lax.rsqrt(ss * (1.0 / QK_HEAD) + EPS) * mscale
        qn_ref[0, :, sl] = (qn * r * gqn_ref[...]).astype(BF16)
        inv_x = jnp.where(lane_x == hh, r, inv_x)
    a1 = x1 * gq1_ref[...]
    a2 = x2 * gq2_ref[...]
    cq_t = cosq_ref[...]
    sq_t = sinq_ref[...]
    qx1_ref[0] = ((a1 * cq_t - a2 * sq_t) * inv_x).astype(BF16)
    qx2_ref[0] = ((a2 * cq_t + a1 * sq_t) * inv_x).astype(BF16)


def _in_proj(x, sc1, sh1, g1, wqkv, wlat, gqd, gkd, gqa, wqb, gkva, wkvb, gqn, gq1, gq2,
             gkn, gkp, gkps, cosq, sinq, cosk, sink, *, n_heads, tm):
    b, s, d = x.shape
    wd = n_heads * LANES
    wx = n_heads * (QK_ROPE // 2)
    row = lambda w: pl.BlockSpec((1, tm, w), lambda bi, i: (bi, i, 0))
    mod = pl.BlockSpec((1, 1, d), lambda bi, i: (bi, 0, 0))
    tab = lambda w: pl.BlockSpec((tm, w), lambda bi, i: (i, 0))
    consts = [g1, wqkv, wlat, gqd, gkd, gqa, wqb, gkva, wkvb, gqn, gq1, gq2, gkn, gkp, gkps]
    out_widths = [wd, wd, wd, wd, wx, wx, wd, wd // 2, wd]
    return pl.pallas_call(
        functools.partial(_in_proj_kernel, n_heads=n_heads),
        grid=(b, s // tm),
        in_specs=[row(d), mod, mod] + [_const_spec(a.shape) for a in consts]
                 + [tab(wx), tab(wx), tab(LANES), tab(LANES)],
        out_specs=[row(w) for w in out_widths],
        out_shape=[jax.ShapeDtypeStruct((b, s, w), BF16) for w in out_widths],
        compiler_params=_cparams(("parallel", "parallel")),
        name="in_proj",
    )(x, sc1, sh1, *consts, cosq, sinq, cosk, sink)


def _t5_bucket(rel):
    nb = REL_BUCKETS // 2
    max_exact = nb // 2
    base = jnp.where(rel > 0, nb, 0)
    n = jnp.abs(rel)
    nf = jnp.maximum(n, 1).astype(jnp.float32)
    large = max_exact + (jnp.log(nf / max_exact) / math.log(REL_MAX_DIST / max_exact)
                         * (nb - max_exact)).astype(jnp.int32)
    large = jnp.minimum(large, nb - 1)
    return base + jnp.where(n < max_exact, n, large)


def _bias_kernel(rb_ref, bucket_ref, o_ref):
    hh = pl.program_id(0)
    t = o_ref.shape[-1]
    far_left = rb_ref[REL_BUCKETS // 2 - 1, hh]
    far_right = rb_ref[REL_BUCKETS - 1, hh]
    o_ref[0, 0] = jnp.full((t, t), far_left, F32) * LOG2E
    o_ref[0, 4] = jnp.full((t, t), far_right, F32) * LOG2E
    bk_strip = bucket_ref[...]
    strip = jnp.zeros(bk_strip.shape, F32)
    for bk in range(REL_BUCKETS):
        strip = jnp.where(bk_strip == bk, rb_ref[bk, hh], strip)
    strip = strip * LOG2E
    for d in (-1, 0, 1):
        diag = jnp.concatenate([strip[:, (2 - d) * t:(3 - d) * t], strip[:, (1 - d) * t:(2 - d) * t]], axis=1)
        rows = jnp.broadcast_to(diag, (t, 2 * t))
        o_ref[0, d + 2] = pltpu.roll(rows, 0, 1, stride=1, stride_axis=0)[:, :t]


def _bias_tiles(rel_bias, t):
    n_heads = rel_bias.shape[1]
    buckets = _t5_bucket(2 * t - jnp.arange(4 * t, dtype=jnp.int32))[None, :]
    return pl.pallas_call(
        _bias_kernel,
        grid=(n_heads,),
        in_specs=[pl.BlockSpec(memory_space=pltpu.SMEM),
                  pl.BlockSpec((1, 4 * t), lambda hh: (0, 0))],
        out_specs=pl.BlockSpec((1, 5, t, t), lambda hh: (hh, 0, 0, 0)),
        out_shape=jax.ShapeDtypeStruct((n_heads, 5, t, t), F32),
        compiler_params=_cparams(("arbitrary",)),
        name="rel_bias_tiles",
    )(rel_bias, buckets)


ONES_ROWS = 16


def _softmax_state_step(s, offset, vt, state):
    m_new, alpha, p = _softmax_weights(s, offset, state[0])
    return m_new, _accumulate(alpha, state[1], vt, p)


def _softmax_weights(s, offset, m_old):
    m_new = jnp.maximum(m_old, jnp.max(s, axis=0, keepdims=True) + offset)
    alpha = jnp.exp2(m_old - m_new)
    p = jnp.exp2(s - (m_new - offset)).astype(BF16)
    return m_new, alpha, p


def _accumulate(alpha, acc_old, vt, p):
    return alpha * acc_old + jnp.dot(vt, p, preferred_element_type=F32)


def _init_state(dv, t):
    return (jnp.full((1, t), -jnp.inf, F32), jnp.zeros((dv + ONES_ROWS, t), F32))


def _with_ones(vt):
    return jnp.concatenate([vt, jnp.ones((ONES_ROWS, vt.shape[1]), vt.dtype)], axis=0)


def _normalised(state, dv):
    _, acc = state
    return acc[:dv] / acc[dv:dv + 1]


def _diff_attn_kernel(rb_ref, lamv_ref, gsub_ref, qt_ref, k_ref, vt_ref, bias_ref, o_ref, *,
                      lambda_init, bounded):
    hh = pl.program_id(1)
    t = bias_ref.shape[-1]
    nq = qt_ref.shape[-1] // t
    nkt = vt_ref.shape[2]

    off_left = rb_ref[REL_BUCKETS // 2 - 1, hh] * LOG2E
    off_right = rb_ref[REL_BUCKETS - 1, hh] * LOG2E
    lv = lamv_ref[...]
    lam = (jnp.exp(jnp.sum(lv[0:1] * lv[1:2], axis=-1, keepdims=True))
           - jnp.exp(jnp.sum(lv[2:3] * lv[3:4], axis=-1, keepdims=True)) + lambda_init)

    n_near = min(nkt, 3)
    row = lax.broadcasted_iota(jnp.int32, (LANES, t), 0)
    zero = jnp.zeros((LANES, t), BF16)

    def query_maps(j):
        qt = qt_ref[0, 0, :, j * t:(j + 1) * t]
        return jnp.where(row < DH_DIFF, qt, zero), jnp.where(row < DH_DIFF, zero, qt)

    def tile_index(j, r):
        return lax.rem(pl.program_id(2) * nq + j - 1 + r + nkt, nkt)

    def far_offset(j, r):
        return jnp.where(tile_index(j, r) < pl.program_id(2) * nq + j, off_left, off_right)

    def biased_scores(j, qmap, r, k0, ck):
        kj = tile_index(j, r)
        k_t = k_ref[0, pl.ds(pl.multiple_of(kj * t + k0, ck), ck), :]
        s = jnp.dot(k_t, qmap, preferred_element_type=F32)
        if r < n_near:
            s = s + bias_ref[0, jnp.clip(kj - (pl.program_id(2) * nq + j), -2, 2) + 2, k0:k0 + ck, :]
        elif bounded:
            s = s + far_offset(j, r)
        return s

    def finish(j, outs):
        o = outs[0] - lam * outs[1]
        ms = jnp.mean(o * o, axis=0, keepdims=True)
        y = o * lax.rsqrt(ms + EPS) * (gsub_ref[...] * (1.0 - lambda_init))
        o_ref[0, j * t:(j + 1) * t, :] = y.T.astype(BF16)

    if bounded:
        ck = min(t, KEY_CHUNK)
        npt = t // ck
        qmaps = [query_maps(j) for j in range(nq)]
        items = [(j, r, c, mp) for j in range(nq) for r in range(nkt) for c in range(npt) for mp in range(2)]
        score = lambda it: biased_scores(it[0], qmaps[it[0]][it[3]], it[1], it[2] * ck, ck)
        depth = min(BOUNDED_DEPTH, len(items))
        pending = [score(it) for it in items[:depth]]
        for n, (j, r, c, mp) in enumerate(items):
            if (r, c, mp) == (0, 0, 0):
                accs = [jnp.zeros((DV_DIFF, t), F32) for _ in range(2)]
                dens = [jnp.zeros((SUBLANES, t), F32) for _ in range(2)]
            s = pending.pop(0)
            if n + depth < len(items):
                pending.append(score(items[n + depth]))
            p = jnp.exp2(s)
            dens[mp] = dens[mp] + jnp.sum(p.reshape(ck // SUBLANES, SUBLANES, t), axis=0)
            accs[mp] = accs[mp] + jnp.dot(vt_ref[0, 0, tile_index(j, r), :, c * ck:(c + 1) * ck],
                                          p.astype(BF16), preferred_element_type=F32)
            if (r, c, mp) == (nkt - 1, npt - 1, 1):
                finish(j, [acc / jnp.sum(den, axis=0, keepdims=True) for acc, den in zip(accs, dens)])
    else:
        for j in range(nq):
            qmap = query_maps(j)
            states = [_init_state(DV_DIFF, t), _init_state(DV_DIFF, t)]
            score = lambda r: [biased_scores(j, qmap[mp], r, 0, t) for mp in range(2)]
            s_next = score(0)
            for r in range(nkt):
                s_cur = s_next
                if r + 1 < nkt:
                    s_next = score(r + 1)
                vt = _with_ones(vt_ref[0, 0, tile_index(j, r)])
                offset = 0.0 if r < n_near else far_offset(j, r)
                for mp in range(2):
                    states[mp] = _softmax_state_step(s_cur[mp], offset, vt, states[mp])
            finish(j, [_normalised(st, DV_DIFF) for st in states])


def _diff_attn(rel_bias, lamv, gsub_col, qt, k, vt, bias, *, lambda_init, bounded):
    b, n_heads, _, s = qt.shape
    t = bias.shape[-1]
    nkt, tk = vt.shape[2], vt.shape[4]
    assert tk == t
    tq = _pick_tile(s, t * (BOUNDED_Q_TILES if bounded else 1))
    return pl.pallas_call(
        functools.partial(_diff_attn_kernel, lambda_init=lambda_init, bounded=bounded),
        grid=(b, n_heads, s // tq),
        in_specs=[pl.BlockSpec(memory_space=pltpu.SMEM),
                  pl.BlockSpec(lamv.shape, lambda bi, hh, qi: (0, 0)),
                  pl.BlockSpec(gsub_col.shape, lambda bi, hh, qi: (0, 0)),
                  pl.BlockSpec((1, 1, LANES, tq), lambda bi, hh, qi: (bi, hh, 0, qi)),
                  pl.BlockSpec((1, s, LANES), lambda bi, hh, qi: (bi, 0, hh)),
                  pl.BlockSpec((1, 1, nkt, DV_DIFF, tk), lambda bi, hh, qi: (bi, hh, 0, 0, 0)),
                  pl.BlockSpec((1, 5, t, t), lambda bi, hh, qi: (hh, 0, 0, 0))],
        out_specs=pl.BlockSpec((1, tq, DV_DIFF), lambda bi, hh, qi: (bi, qi, hh)),
        out_shape=jax.ShapeDtypeStruct((b, s, n_heads * DV_DIFF), BF16),
        compiler_params=_cparams(("parallel", "parallel", "parallel")),
        name="diff_attn",
    )(rel_bias, lamv, gsub_col, qt, k, vt, bias)


def _mla_attn_kernel(qt_ref, k_ref, vt_ref, o_ref):
    nkt = vt_ref.shape[2]
    tk = vt_ref.shape[-1]
    qt = qt_ref[0, 0]
    m, acc = _init_state(V_MLA, qt.shape[-1])
    scores = lambda kj: jnp.dot(k_ref[0, 0, kj * tk:(kj + 1) * tk, :], qt, preferred_element_type=F32)
    depth = 2
    pending = [scores(kj) for kj in range(min(depth, nkt))]
    for kj in range(nkt):
        s = pending.pop(0)
        if kj + depth < nkt:
            pending.append(scores(kj + depth))
        m, alpha, p = _softmax_weights(s, 0.0, m)
        acc = _accumulate(alpha, acc, _with_ones(vt_ref[0, 0, kj]), p)
    o_ref[0] = _normalised((m, acc), V_MLA).T.astype(BF16)


def _mla_attn_bounded_kernel(qt_ref, k_ref, vt_ref, o_ref):
    nkt = vt_ref.shape[2]
    tk = vt_ref.shape[-1]
    t = min(qt_ref.shape[-1], MLA_TQ)
    nq = qt_ref.shape[-1] // t
    ck = min(tk, KEY_CHUNK)
    npt = tk // ck
    n_chunks = nkt * npt
    items = [(j, i) for j in range(nq) for i in range(n_chunks)]

    def scores(item):
        j, i = item
        return jnp.dot(k_ref[0, 0, i * ck:(i + 1) * ck, :], qt_ref[0, 0, :, j * t:(j + 1) * t],
                       preferred_element_type=F32)

    depth = min(BOUNDED_DEPTH, len(items))
    pending = [scores(it) for it in items[:depth]]
    for n, (j, i) in enumerate(items):
        if i == 0:
            acc = jnp.zeros((V_MLA, t), F32)
            den = jnp.zeros((SUBLANES, t), F32)
        s = pending.pop(0)
        if n + depth < len(items):
            pending.append(scores(items[n + depth]))
        c = i % npt
        p = jnp.exp2(s)
        den = den + jnp.sum(p.reshape(ck // SUBLANES, SUBLANES, t), axis=0)
        acc = acc + jnp.dot(vt_ref[0, 0, i // npt, :, c * ck:(c + 1) * ck], p.astype(BF16),
                            preferred_element_type=F32)
        if i == n_chunks - 1:
            o_ref[0, j * t:(j + 1) * t, :] = (acc / jnp.sum(den, axis=0, keepdims=True)).T.astype(BF16)


def _mla_attn(qt, k, vt, *, bounded):
    b, n_heads, _, s = qt.shape
    t = _pick_tile(s, MLA_TQ * (BOUNDED_Q_TILES if bounded else 1))
    nkt, tk = vt.shape[2], vt.shape[4]
    return pl.pallas_call(
        _mla_attn_bounded_kernel if bounded else _mla_attn_kernel,
        grid=(b, n_heads, s // t),
        in_specs=[pl.BlockSpec((1, 1, QK_HEAD, t), lambda bi, hh, qi: (bi, hh, 0, qi)),
                  pl.BlockSpec((1, 1, s, QK_HEAD), lambda bi, hh, qi: (bi, hh, 0, 0)),
                  pl.BlockSpec((1, 1, nkt, V_MLA, tk), lambda bi, hh, qi: (bi, hh, 0, 0, 0))],
        out_specs=pl.BlockSpec((1, t, V_MLA), lambda bi, hh, qi: (bi, qi, hh)),
        out_shape=jax.ShapeDtypeStruct((b, s, n_heads * V_MLA), BF16),
        compiler_params=_cparams(("parallel", "parallel", "parallel")),
        name="mla_attn",
    )(qt, k, vt)


def _out_proj_kernel(x_ref, gt_ref, a_ref, b_ref, wa_ref, wb_ref, o_ref):
    mix = (jnp.dot(a_ref[0], wa_ref[...], preferred_element_type=F32)
           + jnp.dot(b_ref[0], wb_ref[...], preferred_element_type=F32))
    o_ref[0] = x_ref[0] + gt_ref[0] * mix


def _out_proj(x, gt1, a, bb, wa, wb, *, tm):
    b, s, d = x.shape
    row = lambda w: pl.BlockSpec((1, tm, w), lambda bi, i: (bi, i, 0))
    return pl.pallas_call(
        _out_proj_kernel,
        grid=(b, s // tm),
        in_specs=[row(d), pl.BlockSpec((1, 1, d), lambda bi, i: (bi, 0, 0)),
                  row(a.shape[-1]), row(bb.shape[-1]), _const_spec(wa.shape), _const_spec(wb.shape)],
        out_specs=row(d),
        out_shape=jax.ShapeDtypeStruct((b, s, d), F32),
        compiler_params=_cparams(("parallel", "parallel")),
        name="out_proj",
    )(x, gt1, a, bb, wa, wb)


def _ffn_kernel(x_ref, sc_ref, sh_ref, gt_ref, g2_ref, wg_ref, wu_ref, wd_ref, o_ref, h_ref, acc_ref):
    j = pl.program_id(2)

    @pl.when(j == 0)
    def _():
        h_ref[...] = _modulated_norm(x_ref[0], g2_ref[...], sc_ref[0], sh_ref[0]).astype(BF16)
        acc_ref[...] = jnp.zeros(acc_ref.shape, F32)

    h = h_ref[...]
    g = jnp.dot(h, wg_ref[...], preferred_element_type=F32)
    u = jnp.dot(h, wu_ref[...], preferred_element_type=F32)
    act = (g * jax.nn.sigmoid(g) * u).astype(BF16)
    acc_ref[...] += jnp.dot(act, wd_ref[...], preferred_element_type=F32)

    @pl.when(j == pl.num_programs(2) - 1)
    def _():
        o_ref[0] = x_ref[0] + gt_ref[0] * acc_ref[...]


def _ffn(x, sc2, sh2, gt2, g2, wg, wu, wd, *, tm, tf):
    b, s, d = x.shape
    dff = wg.shape[1]
    row = pl.BlockSpec((1, tm, d), lambda bi, i, j: (bi, i, 0))
    mod = pl.BlockSpec((1, 1, d), lambda bi, i, j: (bi, 0, 0))
    return pl.pallas_call(
        _ffn_kernel,
        grid=(b, s // tm, dff // tf),
        in_specs=[row, mod, mod, mod, pl.BlockSpec((1, d), lambda bi, i, j: (0, 0)),
                  pl.BlockSpec((d, tf), lambda bi, i, j: (0, j)),
                  pl.BlockSpec((d, tf), lambda bi, i, j: (0, j)),
                  pl.BlockSpec((tf, d), lambda bi, i, j: (j, 0))],
        out_specs=row,
        out_shape=jax.ShapeDtypeStruct((b, s, d), F32),
        scratch_shapes=[pltpu.VMEM((tm, d), BF16), pltpu.VMEM((tm, d), F32)],
        compiler_params=_cparams(("parallel", "parallel", "arbitrary")),
        name="ffn",
    )(x, sc2, sh2, gt2, g2, wg, wu, wd)


def _rope_tables(s):
    pos = jnp.arange(s, dtype=jnp.float32)
    inv = 1.0 / (ROPE_THETA ** (jnp.arange(0, QK_ROPE, 2, dtype=jnp.float32) / QK_ROPE))
    ang = pos[:, None] * inv[None, :]
    return jnp.cos(ang), jnp.sin(ang)


def _pick_tile(n, pref):
    t = min(pref, n)
    while n % t:
        t //= 2
    return t


def kernel(x, c, rel_bias, w_ada, b_ada, g_norm1, w_in, g_q_diff, g_k_diff, lambda_vecs, g_subln, g_q_a, w_q_b, g_kv_a, w_kv_b, g_q_mla, g_k_mla, w_out, g_norm2, w_gate, w_up, w_down):
    b, s, d = x.shape
    depth = w_ada.shape[0]
    diff_width = d // 2
    n_hd = diff_width // DV_DIFF
    n_hm = (d - diff_width) // V_MLA
    assert n_hd == n_hm and n_hd % 2 == 0
    n_heads = n_hd
    wd = n_heads * LANES
    half = QK_ROPE // 2
    tk = _pick_tile(s, ATT_TK)
    nkt = s // tk
    t = tk
    assert t + 1 >= REL_MAX_DIST
    tm = _pick_tile(s, 512)

    cos, sin = _rope_tables(s)
    cosq, sinq = jnp.tile(cos, (1, n_heads)), jnp.tile(sin, (1, n_heads))
    cosk = jnp.tile(cos, (1, 4))
    sink = jnp.tile(jnp.concatenate([-sin, sin], axis=1), (1, 2))

    c_pad = jnp.pad(c, ((0, (-b) % 8), (0, 0)))

    for l in range(depth):
        lambda_init = 0.8 - 0.6 * math.exp(-0.3 * l)
        mod = _ada(c_pad, w_ada[l], b_ada[l][None, :])[:b]
        sh1, sc1, gt1, sh2, sc2, gt2 = [m[:, None, :] for m in jnp.split(mod, 6, axis=-1)]

        wi = w_in[l]
        o = 3 * wd + Q_LORA + KV_LORA
        w_kpe = wi[:, o:o + QK_ROPE]
        w_kpe_sw = jnp.concatenate([w_kpe[:, half:], w_kpe[:, :half]], axis=1)
        wqkv = wi[:, :3 * wd].astype(BF16)
        wlat = jnp.concatenate([wi[:, 3 * wd:o], w_kpe, w_kpe, w_kpe_sw, w_kpe_sw], axis=1).astype(BF16)
        wq = w_q_b[l].reshape(Q_LORA, n_heads, QK_HEAD)
        wqb = jnp.concatenate([wq[:, :, :QK_NOPE].reshape(Q_LORA, -1),
                               wq[:, :, QK_NOPE:QK_NOPE + half].reshape(Q_LORA, -1),
                               wq[:, :, QK_NOPE + half:].reshape(Q_LORA, -1)], axis=1).astype(BF16)
        wkv = w_kv_b[l].reshape(KV_LORA, n_heads, QK_NOPE + V_MLA)
        wkvb = jnp.concatenate([wkv[:, :, :QK_NOPE].reshape(KV_LORA, -1),
                                wkv[:, :, QK_NOPE:].reshape(KV_LORA, -1)], axis=1).astype(BF16)
        gq, gk = g_q_mla[l], g_k_mla[l]
        gk_pe = gk[QK_NOPE:]
        gk_pe_sw = jnp.concatenate([gk_pe[half:], gk_pe[:half]])

        qd, kd, vd, qn, qx1, qx2, kn, kp, vm = _in_proj(
            x, sc1, sh1, g_norm1[l][None, :], wqkv, wlat,
            jnp.tile(g_q_diff[l], 2)[None, :], jnp.tile(g_k_diff[l], 2)[None, :],
            g_q_a[l][None, :], wqb, g_kv_a[l][None, :], wkvb,
            gq[None, :QK_NOPE], jnp.tile(gq[QK_NOPE:QK_NOPE + half], n_heads)[None, :],
            jnp.tile(gq[QK_NOPE + half:], n_heads)[None, :],
            gk[None, :QK_NOPE], jnp.tile(gk_pe, 2)[None, :], jnp.tile(gk_pe_sw, 2)[None, :],
            cosq, sinq, cosk, sink, n_heads=n_heads, tm=tm)

        def heads_t(a, w):
            return a.reshape(b, s, n_heads, w).transpose(0, 2, 3, 1)

        def value_t(a, w):
            return a.reshape(b, nkt, tk, n_heads, w).transpose(0, 3, 1, 4, 2)

        qt_d = heads_t(qd, LANES)
        vt_d = value_t(vd, DV_DIFF)
        qt_m = jnp.concatenate([heads_t(qn, QK_NOPE), heads_t(qx1, half), heads_t(qx2, half)], axis=2)
        k_m = jnp.concatenate([kn.reshape(b, s, n_heads, QK_NOPE), kp.reshape(b, s, n_heads, QK_ROPE)],
                              axis=-1).transpose(0, 2, 1, 3)
        vt_m = value_t(vm, V_MLA)

        bias = _bias_tiles(rel_bias, t)
        gmax = lambda g: jnp.max(jnp.abs(g))
        bound_d = (1.02 * LOG2E * DH_DIFF ** 0.5 * gmax(g_q_diff[l]) * gmax(g_k_diff[l])
                   + LOG2E * gmax(rel_bias))
        bound_m = 1.02 * LOG2E * QK_HEAD ** 0.5 * gmax(g_q_mla[l]) * gmax(g_k_mla[l])
        diff_args = (rel_bias, lambda_vecs[l], g_subln[l][:, None], qt_d, kd, vt_d, bias)
        a_out = lax.cond(
            bound_d <= SCORE_BOUND,
            lambda *a: _diff_attn(*a, lambda_init=lambda_init, bounded=True),
            lambda *a: _diff_attn(*a, lambda_init=lambda_init, bounded=False), *diff_args)
        b_out = lax.cond(
            bound_m <= SCORE_BOUND,
            lambda *a: _mla_attn(*a, bounded=True),
            lambda *a: _mla_attn(*a, bounded=False), qt_m, k_m, vt_m)

        wo = w_out[l].astype(BF16)
        x = _out_proj(x, gt1, a_out, b_out, wo[:diff_width], wo[diff_width:], tm=tm)
        x = _ffn(x, sc2, sh2, gt2, g_norm2[l][None, :], w_gate[l].astype(BF16), w_up[l].astype(BF16),
                 w_down[l].astype(BF16), tm=tm, tf=_pick_tile(w_gate.shape[-1], 512))
    return x
```

```python
import functools
import math

import jax
import jax.numpy as jnp
from jax import lax
from jax.experimental import pallas as pl
from jax.experimental.pallas import tpu as pltpu

F32 = jnp.float32
BF16 = jnp.bfloat16

DH_DIFF = 64
DV_DIFF = 2 * DH_DIFF
QK_NOPE = 128
QK_ROPE = 64
QK_HEAD = QK_NOPE + QK_ROPE
V_MLA = 128
Q_LORA = 512
KV_LORA = 256
ROPE_THETA = 10000.0
REL_BUCKETS = 32
REL_MAX_DIST = 128
EPS = 1e-6
LOG2E = 1.4426950408889634

LANES = 128
SUBLANES = 8
VMEM_LIMIT_BYTES = 56 * 1024 * 1024

FFN_TM = 1024
FFN_TF = 512
ATT_TK = 512
DIFF_TQ = 512
MLA_TQ = 512
KEY_CHUNK = 256
BOUNDED_DEPTH = 2
BOUNDED_Q_TILES = 4
SCORE_BOUND = 40.0


def _cparams(sem):
    return pltpu.CompilerParams(dimension_semantics=sem, vmem_limit_bytes=VMEM_LIMIT_BYTES)


def _const_spec(shape):
    nd = len(shape)
    return pl.BlockSpec(shape, lambda *_: (0,) * nd, pipeline_mode=pl.Buffered(1))


def _ada_kernel(c_ref, w_ref, b_ref, o_ref):
    c = c_ref[...]
    ca = c * jax.nn.sigmoid(c)
    o_ref[...] = jnp.dot(ca, w_ref[...], preferred_element_type=F32,
                         precision=lax.Precision.HIGHEST) + b_ref[...]


def _ada(c_pad, w, b, tn=1024):
    m, d = c_pad.shape
    n = w.shape[1]
    return pl.pallas_call(
        _ada_kernel,
        grid=(n // tn,),
        in_specs=[pl.BlockSpec((m, d), lambda j: (0, 0)),
                  pl.BlockSpec((d, tn), lambda j: (0, j)),
                  pl.BlockSpec((1, tn), lambda j: (0, j))],
        out_specs=pl.BlockSpec((m, tn), lambda j: (0, j)),
        out_shape=jax.ShapeDtypeStruct((m, n), F32),
        compiler_params=_cparams(("arbitrary",)),
        name="ada",
    )(c_pad, w, b)


def _modulated_norm(x, g, sc, sh):
    ms = jnp.mean(x * x, axis=-1, keepdims=True)
    return (x * lax.rsqrt(ms + EPS) * g) * (1.0 + sc) + sh


def _half_lane_norm(blk, g2, lo_mask, out_scale):
    sq = blk * blk
    s_lo = jnp.sum(jnp.where(lo_mask, sq, 0.0), axis=-1, keepdims=True)
    s_hi = jnp.sum(jnp.where(lo_mask, 0.0, sq), axis=-1, keepdims=True)
    inv = jnp.where(lo_mask, lax.rsqrt(s_lo * (1.0 / DH_DIFF) + EPS),
                    lax.rsqrt(s_hi * (1.0 / DH_DIFF) + EPS))
    return blk * inv * (g2 * out_scale)


def _in_proj_kernel(x_ref, sc_ref, sh_ref, g1_ref, wqkv_ref, wlat_ref, gqd_ref, gkd_ref,
                    gqa_ref, wqb_ref, gkva_ref, wkvb_ref, gqn_ref, gq1_ref, gq2_ref,
                    gkn_ref, gkp_ref, gkps_ref, cosq_ref, sinq_ref, cosk_ref, sink_ref,
                    qtd_ref, kd_ref, vtd_ref, qtm_ref, km_ref, vtm_ref, *, n_heads):
    h = _modulated_norm(x_ref[0], g1_ref[...], sc_ref[0], sh_ref[0]).astype(BF16)
    tm = h.shape[0]
    wd = n_heads * LANES
    lane = lax.broadcasted_iota(jnp.int32, (tm, LANES), 1)
    lo_mask = lane < DH_DIFF

    qscale = DH_DIFF ** -0.5 * LOG2E
    qd = jnp.dot(h, wqkv_ref[:, 0:wd], preferred_element_type=F32)
    for hh in range(n_heads):
        sl = slice(hh * LANES, (hh + 1) * LANES)
        qtd_ref[0, hh] = _half_lane_norm(qd[:, sl], gqd_ref[...], lo_mask, qscale).T.astype(BF16)
    kd = jnp.dot(h, wqkv_ref[:, wd:2 * wd], preferred_element_type=F32)
    for hh in range(n_heads):
        sl = slice(hh * LANES, (hh + 1) * LANES)
        kd_ref[0, :, sl] = _half_lane_norm(kd[:, sl], gkd_ref[...], lo_mask, 1.0).astype(BF16)
    vd = jnp.dot(h, wqkv_ref[:, 2 * wd:3 * wd], preferred_element_type=F32)
    for hh in range(n_heads):
        vtd_ref[0, hh, 0] = vd[:, hh * LANES:(hh + 1) * LANES].T.astype(BF16)

    lat = jnp.dot(h, wlat_ref[...], preferred_element_type=F32)
    cq = lat[:, 0:Q_LORA]
    ckv = lat[:, Q_LORA:Q_LORA + KV_LORA]
    o = Q_LORA + KV_LORA
    kpe2 = lat[:, o:o + LANES]
    kpes2 = lat[:, o + LANES:o + 2 * LANES]

    def rms(v, g):
        return v * lax.rsqrt(jnp.mean(v * v, axis=-1, keepdims=True) + EPS) * g

    kv = jnp.dot(rms(ckv, gkva_ref[...]).astype(BF16), wkvb_ref[...], preferred_element_type=F32)
    for hh in range(n_heads):
        vtm_ref[0, hh, 0] = kv[:, wd + hh * LANES:wd + (hh + 1) * LANES].T.astype(BF16)
    ss_pe = jnp.sum(jnp.where(lo_mask, kpe2 * kpe2, 0.0), axis=-1, keepdims=True)
    kr2 = kpe2 * gkp_ref[...] * cosk_ref[...] + kpes2 * gkps_ref[...] * sink_ref[...]
    for hh in range(n_heads):
        sl = slice(hh * LANES, (hh + 1) * LANES)
        kn = kv[:, sl]
        ss = jnp.sum(kn * kn, axis=-1, keepdims=True) + ss_pe
        r = lax.rsqrt(ss * (1.0 / QK_HEAD) + EPS)
        km_ref[0, hh, :, 0:QK_NOPE] = (kn * r * gkn_ref[...]).astype(BF16)
        km_ref[0, hh, :, QK_NOPE:QK_HEAD] = (kr2 * r)[:, 0:QK_ROPE].astype(BF16)

    qm = jnp.dot(rms(cq, gqa_ref[...]).astype(BF16), wqb_ref[...], preferred_element_type=F32)
    half = QK_ROPE // 2
    wx = n_heads * half
    x1 = qm[:, wd:wd + wx]
    x2 = qm[:, wd + wx:wd + 2 * wx]
    lane_x = lax.broadcasted_iota(jnp.int32, (tm, wx), 1) // half
    sq_x = x1 * x1 + x2 * x2
    mscale = QK_HEAD ** -0.5 * LOG2E
    inv_x = jnp.zeros((tm, wx), F32)
    for hh in range(n_heads):
        sl = slice(hh * LANES, (hh + 1) * LANES)
        qn = qm[:, sl]
        ss = (jnp.sum(qn * qn, axis=-1, keepdims=True)
              + jnp.sum(jnp.where(lane_x == hh, sq_x, 0.0), axis=-1, keepdims=True))
        r = lax.rsqrt(ss * (1.0 / QK_HEAD) + EPS) * mscale
        qtm_ref[0, hh, 0:QK_NOPE, :] = (qn * r * gqn_ref[...]).T.astype(BF16)
        inv_x = jnp.where(lane_x == hh, r, inv_x)
    a1 = x1 * gq1_ref[...]
    a2 = x2 * gq2_ref[...]
    cq_t = cosq_ref[...]
    sq_t = sinq_ref[...]
    x1t = ((a1 * cq_t - a2 * sq_t) * inv_x).T.astype(BF16)
    x2t = ((a2 * cq_t + a1 * sq_t) * inv_x).T.astype(BF16)
    for hh in range(n_heads):
        qtm_ref[0, hh, QK_NOPE:QK_NOPE + half, :] = x1t[hh * half:(hh + 1) * half]
        qtm_ref[0, hh, QK_NOPE + half:QK_HEAD, :] = x2t[hh * half:(hh + 1) * half]


def _in_proj(x, sc1, sh1, g1, wqkv, wlat, gqd, gkd, gqa, wqb, gkva, wkvb, gqn, gq1, gq2,
             gkn, gkp, gkps, cosq, sinq, cosk, sink, *, n_heads, tm):
    b, s, d = x.shape
    wd = n_heads * LANES
    wx = n_heads * (QK_ROPE // 2)
    row = lambda w: pl.BlockSpec((1, tm, w), lambda bi, i: (bi, i, 0))
    mod = pl.BlockSpec((1, 1, d), lambda bi, i: (bi, 0, 0))
    tab = lambda w: pl.BlockSpec((tm, w), lambda bi, i: (i, 0))
    consts = [g1, wqkv, wlat, gqd, gkd, gqa, wqb, gkva, wkvb, gqn, gq1, gq2, gkn, gkp, gkps]
    return pl.pallas_call(
        functools.partial(_in_proj_kernel, n_heads=n_heads),
        grid=(b, s // tm),
        in_specs=[row(d), mod, mod] + [_const_spec(a.shape) for a in consts]
                 + [tab(wx), tab(wx), tab(LANES), tab(LANES)],
        out_specs=[pl.BlockSpec((1, n_heads, LANES, tm), lambda bi, i: (bi, 0, 0, i)),
                   row(wd),
                   pl.BlockSpec((1, n_heads, 1, DV_DIFF, tm), lambda bi, i: (bi, 0, i, 0, 0)),
                   pl.BlockSpec((1, n_heads, QK_HEAD, tm), lambda bi, i: (bi, 0, 0, i)),
                   pl.BlockSpec((1, n_heads, tm, QK_HEAD), lambda bi, i: (bi, 0, i, 0)),
                   pl.BlockSpec((1, n_heads, 1, V_MLA, tm), lambda bi, i: (bi, 0, i, 0, 0))],
        out_shape=[jax.ShapeDtypeStruct((b, n_heads, LANES, s), BF16),
                   jax.ShapeDtypeStruct((b, s, wd), BF16),
                   jax.ShapeDtypeStruct((b, n_heads, s // tm, DV_DIFF, tm), BF16),
                   jax.ShapeDtypeStruct((b, n_heads, QK_HEAD, s), BF16),
                   jax.ShapeDtypeStruct((b, n_heads, s, QK_HEAD), BF16),
                   jax.ShapeDtypeStruct((b, n_heads, s // tm, V_MLA, tm), BF16)],
        compiler_params=_cparams(("parallel", "parallel")),
        name="in_proj",
    )(x, sc1, sh1, *consts, cosq, sinq, cosk, sink)


def _t5_bucket(rel):
    nb = REL_BUCKETS // 2
    max_exact = nb // 2
    base = jnp.where(rel > 0, nb, 0)
    n = jnp.abs(rel)
    nf = jnp.maximum(n, 1).astype(jnp.float32)
    large = max_exact + (jnp.log(nf / max_exact) / math.log(REL_MAX_DIST / max_exact)
                         * (nb - max_exact)).astype(jnp.int32)
    large = jnp.minimum(large, nb - 1)
    return base + jnp.where(n < max_exact, n, large)


def _bias_kernel(rb_ref, bucket_ref, o_ref):
    hh = pl.program_id(0)
    t = o_ref.shape[-1]
    far_left = rb_ref[REL_BUCKETS // 2 - 1, hh]
    far_right = rb_ref[REL_BUCKETS - 1, hh]
    o_ref[0, 0] = jnp.full((t, t), far_left, F32) * LOG2E
    o_ref[0, 4] = jnp.full((t, t), far_right, F32) * LOG2E
    bk_strip = bucket_ref[...]
    strip = jnp.zeros(bk_strip.shape, F32)
    for bk in range(REL_BUCKETS):
        strip = jnp.where(bk_strip == bk, rb_ref[bk, hh], strip)
    strip = strip * LOG2E
    for d in (-1, 0, 1):
        diag = jnp.concatenate([strip[:, (2 - d) * t:(3 - d) * t], strip[:, (1 - d) * t:(2 - d) * t]], axis=1)
        rows = jnp.broadcast_to(diag, (t, 2 * t))
        o_ref[0, d + 2] = pltpu.roll(rows, 0, 1, stride=1, stride_axis=0)[:, :t]


def _bias_tiles(rel_bias, t):
    n_heads = rel_bias.shape[1]
    buckets = _t5_bucket(2 * t - jnp.arange(4 * t, dtype=jnp.int32))[None, :]
    return pl.pallas_call(
        _bias_kernel,
        grid=(n_heads,),
        in_specs=[pl.BlockSpec(memory_space=pltpu.SMEM),
                  pl.BlockSpec((1, 4 * t), lambda hh: (0, 0))],
        out_specs=pl.BlockSpec((1, 5, t, t), lambda hh: (hh, 0, 0, 0)),
        out_shape=jax.ShapeDtypeStruct((n_heads, 5, t, t), F32),
        compiler_params=_cparams(("arbitrary",)),
        name="rel_bias_tiles",
    )(rel_bias, buckets)


ONES_ROWS = 16


def _softmax_state_step(s, offset, vt, state):
    m_new, alpha, p = _softmax_weights(s, offset, state[0])
    return m_new, _accumulate(alpha, state[1], vt, p)


def _softmax_weights(s, offset, m_old):
    m_new = jnp.maximum(m_old, jnp.max(s, axis=0, keepdims=True) + offset)
    alpha = jnp.exp2(m_old - m_new)
    p = jnp.exp2(s - (m_new - offset)).astype(BF16)
    return m_new, alpha, p


def _accumulate(alpha, acc_old, vt, p):
    return alpha * acc_old + jnp.dot(vt, p, preferred_element_type=F32)


def _init_state(dv, t):
    return (jnp.full((1, t), -jnp.inf, F32), jnp.zeros((dv + ONES_ROWS, t), F32))


def _with_ones(vt):
    return jnp.concatenate([vt, jnp.ones((ONES_ROWS, vt.shape[1]), vt.dtype)], axis=0)


def _normalised(state, dv):
    _, acc = state
    return acc[:dv] / acc[dv:dv + 1]


def _diff_attn_kernel(rb_ref, lamv_ref, gsub_ref, qt_ref, k_ref, vt_ref, bias_ref, o_ref, *,
                      lambda_init, bounded):
    hh = pl.program_id(1)
    t = bias_ref.shape[-1]
    nq = qt_ref.shape[-1] // t
    nkt = vt_ref.shape[2]

    off_left = rb_ref[REL_BUCKETS // 2 - 1, hh] * LOG2E
    off_right = rb_ref[REL_BUCKETS - 1, hh] * LOG2E
    lv = lamv_ref[...]
    lam = (jnp.exp(jnp.sum(lv[0:1] * lv[1:2], axis=-1, keepdims=True))
           - jnp.exp(jnp.sum(lv[2:3] * lv[3:4], axis=-1, keepdims=True)) + lambda_init)

    n_near = min(nkt, 3)
    row = lax.broadcasted_iota(jnp.int32, (LANES, t), 0)
    zero = jnp.zeros((LANES, t), BF16)

    def query_maps(j):
        qt = qt_ref[0, 0, :, j * t:(j + 1) * t]
        return jnp.where(row < DH_DIFF, qt, zero), jnp.where(row < DH_DIFF, zero, qt)

    def tile_index(j, r):
        return lax.rem(pl.program_id(2) * nq + j - 1 + r + nkt, nkt)

    def far_offset(j, r):
        return jnp.where(tile_index(j, r) < pl.program_id(2) * nq + j, off_left, off_right)

    def biased_scores(j, qmap, r, k0, ck):
        kj = tile_index(j, r)
        k_t = k_ref[0, pl.ds(pl.multiple_of(kj * t + k0, ck), ck), :]
        s = jnp.dot(k_t, qmap, preferred_element_type=F32)
        if r < n_near:
            s = s + bias_ref[0, jnp.clip(kj - (pl.program_id(2) * nq + j), -2, 2) + 2, k0:k0 + ck, :]
        elif bounded:
            s = s + far_offset(j, r)
        return s

    def finish(j, outs):
        o = outs[0] - lam * outs[1]
        ms = jnp.mean(o * o, axis=0, keepdims=True)
        y = o * lax.rsqrt(ms + EPS) * (gsub_ref[...] * (1.0 - lambda_init))
        o_ref[0, j * t:(j + 1) * t, :] = y.T.astype(BF16)

    if bounded:
        ck = min(t, KEY_CHUNK)
        npt = t // ck
        qmaps = [query_maps(j) for j in range(nq)]
        items = [(j, r, c, mp) for j in range(nq) for r in range(nkt) for c in range(npt) for mp in range(2)]
        score = lambda it: biased_scores(it[0], qmaps[it[0]][it[3]], it[1], it[2] * ck, ck)
        depth = min(BOUNDED_DEPTH, len(items))
        pending = [score(it) for it in items[:depth]]
        for n, (j, r, c, mp) in enumerate(items):
            if (r, c, mp) == (0, 0, 0):
                accs = [jnp.zeros((DV_DIFF, t), F32) for _ in range(2)]
                dens = [jnp.zeros((SUBLANES, t), F32) for _ in range(2)]
            s = pending.pop(0)
            if n + depth < len(items):
                pending.append(score(items[n + depth]))
            p = jnp.exp2(s)
            dens[mp] = dens[mp] + jnp.sum(p.reshape(ck // SUBLANES, SUBLANES, t), axis=0)
            accs[mp] = accs[mp] + jnp.dot(vt_ref[0, 0, tile_index(j, r), :, c * ck:(c + 1) * ck],
                                          p.astype(BF16), preferred_element_type=F32)
            if (r, c, mp) == (nkt - 1, npt - 1, 1):
                finish(j, [acc / jnp.sum(den, axis=0, keepdims=True) for acc, den in zip(accs, dens)])
    else:
        for j in range(nq):
            qmap = query_maps(j)
            states = [_init_state(DV_DIFF, t), _init_state(DV_DIFF, t)]
            score = lambda r: [biased_scores(j, qmap[mp], r, 0, t) for mp in range(2)]
            s_next = score(0)
            for r in range(nkt):
                s_cur = s_next
                if r + 1 < nkt:
                    s_next = score(r + 1)
                vt = _with_ones(vt_ref[0, 0, tile_index(j, r)])
                offset = 0.0 if r < n_near else far_offset(j, r)
                for mp in range(2):
                    states[mp] = _softmax_state_step(s_cur[mp], offset, vt, states[mp])
            finish(j, [_normalised(st, DV_DIFF) for st in states])


def _diff_attn(rel_bias, lamv, gsub_col, qt, k, vt, bias, *, lambda_init, bounded):
    b, n_heads, _, s = qt.shape
    t = bias.shape[-1]
    nkt, tk = vt.shape[2], vt.shape[4]
    assert tk == t
    tq = _pick_tile(s, t * (BOUNDED_Q_TILES if bounded else 1))
    return pl.pallas_call(
        functools.partial(_diff_attn_kernel, lambda_init=lambda_init, bounded=bounded),
        grid=(b, n_heads, s // tq),
        in_specs=[pl.BlockSpec(memory_space=pltpu.SMEM),
                  pl.BlockSpec(lamv.shape, lambda bi, hh, qi: (0, 0)),
                  pl.BlockSpec(gsub_col.shape, lambda bi, hh, qi: (0, 0)),
                  pl.BlockSpec((1, 1, LANES, tq), lambda bi, hh, qi: (bi, hh, 0, qi)),
                  pl.BlockSpec((1, s, LANES), lambda bi, hh, qi: (bi, 0, hh)),
                  pl.BlockSpec((1, 1, nkt, DV_DIFF, tk), lambda bi, hh, qi: (bi, hh, 0, 0, 0)),
                  pl.BlockSpec((1, 5, t, t), lambda bi, hh, qi: (hh, 0, 0, 0))],
        out_specs=pl.BlockSpec((1, tq, DV_DIFF), lambda bi, hh, qi: (bi, qi, hh)),
        out_shape=jax.ShapeDtypeStruct((b, s, n_heads * DV_DIFF), BF16),
        compiler_params=_cparams(("parallel", "parallel", "parallel")),
        name="diff_attn",
    )(rel_bias, lamv, gsub_col, qt, k, vt, bias)


def _mla_attn_kernel(qt_ref, k_ref, vt_ref, o_ref):
    nkt = vt_ref.shape[2]
    tk = vt_ref.shape[-1]
    qt = qt_ref[0, 0]
    m, acc = _init_state(V_MLA, qt.shape[-1])
    scores = lambda kj: jnp.dot(k_ref[0, 0, kj * tk:(kj + 1) * tk, :], qt, preferred_element_type=F32)
    depth = 2
    pending = [scores(kj) for kj in range(min(depth, nkt))]
    for kj in range(nkt):
        s = pending.pop(0)
        if kj + depth < nkt:
            pending.append(scores(kj + depth))
        m, alpha, p = _softmax_weights(s, 0.0, m)
        acc = _accumulate(alpha, acc, _with_ones(vt_ref[0, 0, kj]), p)
    o_ref[0] = _normalised((m, acc), V_MLA).T.astype(BF16)


def _mla_attn_bounded_kernel(qt_ref, k_ref, vt_ref, o_ref):
    nkt = vt_ref.shape[2]
    tk = vt_ref.shape[-1]
    t = min(qt_ref.shape[-1], MLA_TQ)
    nq = qt_ref.shape[-1] // t
    ck = min(tk, KEY_CHUNK)
    npt = tk // ck
    n_chunks = nkt * npt
    items = [(j, i) for j in range(nq) for i in range(n_chunks)]

    def scores(item):
        j, i = item
        return jnp.dot(k_ref[0, 0, i * ck:(i + 1) * ck, :], qt_ref[0, 0, :, j * t:(j + 1) * t],
                       preferred_element_type=F32)

    depth = min(BOUNDED_DEPTH, len(items))
    pending = [scores(it) for it in items[:depth]]
    for n, (j, i) in enumerate(items):
        if i == 0:
            acc = jnp.zeros((V_MLA, t), F32)
            den = jnp.zeros((SUBLANES, t), F32)
        s = pending.pop(0)
        if n + depth < len(items):
            pending.append(scores(items[n + depth]))
        c = i % npt
        p = jnp.exp2(s)
        den = den + jnp.sum(p.reshape(ck // SUBLANES, SUBLANES, t), axis=0)
        acc = acc + jnp.dot(vt_ref[0, 0, i // npt, :, c * ck:(c + 1) * ck], p.astype(BF16),
                            preferred_element_type=F32)
        if i == n_chunks - 1:
            o_ref[0, j * t:(j + 1) * t, :] = (acc / jnp.sum(den, axis=0, keepdims=True)).T.astype(BF16)


def _mla_attn(qt, k, vt, *, bounded):
    b, n_heads, _, s = qt.shape
    t = _pick_tile(s, MLA_TQ * (BOUNDED_Q_TILES if bounded else 1))
    nkt, tk = vt.shape[2], vt.shape[4]
    return pl.pallas_call(
        _mla_attn_bounded_kernel if bounded else _mla_attn_kernel,
        grid=(b, n_heads, s // t),
        in_specs=[pl.BlockSpec((1, 1, QK_HEAD, t), lambda bi, hh, qi: (bi, hh, 0, qi)),
                  pl.BlockSpec((1, 1, s, QK_HEAD), lambda bi, hh, qi: (bi, hh, 0, 0)),
                  pl.BlockSpec((1, 1, nkt, V_MLA, tk), lambda bi, hh, qi: (bi, hh, 0, 0, 0))],
        out_specs=pl.BlockSpec((1, t, V_MLA), lambda bi, hh, qi: (bi, qi, hh)),
        out_shape=jax.ShapeDtypeStruct((b, s, n_heads * V_MLA), BF16),
        compiler_params=_cparams(("parallel", "parallel", "parallel")),
        name="mla_attn",
    )(qt, k, vt)


def _out_proj_kernel(x_ref, gt_ref, a_ref, b_ref, wa_ref, wb_ref, sc_ref, sh_ref, g2_ref, o_ref, h_ref):
    mix = (jnp.dot(a_ref[0], wa_ref[...], preferred_element_type=F32)
           + jnp.dot(b_ref[0], wb_ref[...], preferred_element_type=F32))
    x1 = x_ref[0] + gt_ref[0] * mix
    o_ref[0] = x1
    h_ref[0] = _modulated_norm(x1, g2_ref[...], sc_ref[0], sh_ref[0]).astype(BF16)


def _out_proj(x, gt1, a, bb, wa, wb, sc2, sh2, g2, *, tm):
    b, s, d = x.shape
    row = lambda w: pl.BlockSpec((1, tm, w), lambda bi, i: (bi, i, 0))
    mod = pl.BlockSpec((1, 1, d), lambda bi, i: (bi, 0, 0))
    return pl.pallas_call(
        _out_proj_kernel,
        grid=(b, s // tm),
        in_specs=[row(d), mod, row(a.shape[-1]), row(bb.shape[-1]), _const_spec(wa.shape),
                  _const_spec(wb.shape), mod, mod, _const_spec(g2.shape)],
        out_specs=[row(d), row(d)],
        out_shape=[jax.ShapeDtypeStruct((b, s, d), F32), jax.ShapeDtypeStruct((b, s, d), BF16)],
        compiler_params=_cparams(("parallel", "parallel")),
        name="out_proj",
    )(x, gt1, a, bb, wa, wb, sc2, sh2, g2)


def _ffn_kernel(x_ref, h_ref, gt_ref, wg_ref, wu_ref, wd_ref, o_ref):
    j = pl.program_id(2)

    @pl.when(j == 0)
    def _():
        o_ref[...] = jnp.zeros(o_ref.shape, F32)

    h = h_ref[0]
    g = jnp.dot(h, wg_ref[...], preferred_element_type=F32)
    u = jnp.dot(h, wu_ref[...], preferred_element_type=F32)
    act = (g * jax.nn.sigmoid(g) * u).astype(BF16)
    o_ref[0] += jnp.dot(act, wd_ref[...], preferred_element_type=F32)

    @pl.when(j == pl.num_programs(2) - 1)
    def _():
        o_ref[0] = x_ref[0] + gt_ref[0] * o_ref[0]


def _ffn(x, h2, gt2, wg, wu, wd, *, tm, tf):
    b, s, d = x.shape
    dff = wg.shape[1]
    row = pl.BlockSpec((1, tm, d), lambda bi, i, j: (bi, i, 0))
    out_row = pl.BlockSpec((1, tm, d), lambda bi, i, j: (bi, i, 0), pipeline_mode=pl.Buffered(1))
    return pl.pallas_call(
        _ffn_kernel,
        grid=(b, s // tm, dff // tf),
        in_specs=[row, row, pl.BlockSpec((1, 1, d), lambda bi, i, j: (bi, 0, 0)),
                  pl.BlockSpec((d, tf), lambda bi, i, j: (0, j)),
                  pl.BlockSpec((d, tf), lambda bi, i, j: (0, j)),
                  pl.BlockSpec((tf, d), lambda bi, i, j: (j, 0))],
        out_specs=out_row,
        out_shape=jax.ShapeDtypeStruct((b, s, d), F32),
        compiler_params=_cparams(("parallel", "parallel", "arbitrary")),
        name="ffn",
    )(x, h2, gt2, wg, wu, wd)


def _rope_tables(s):
    pos = jnp.arange(s, dtype=jnp.float32)
    inv = 1.0 / (ROPE_THETA ** (jnp.arange(0, QK_ROPE, 2, dtype=jnp.float32) / QK_ROPE))
    ang = pos[:, None] * inv[None, :]
    return jnp.cos(ang), jnp.sin(ang)


def _pick_tile(n, pref):
    t = min(pref, n)
    while n % t:
        t //= 2
    return t


def kernel(x, c, rel_bias, w_ada, b_ada, g_norm1, w_in, g_q_diff, g_k_diff, lambda_vecs, g_subln, g_q_a, w_q_b, g_kv_a, w_kv_b, g_q_mla, g_k_mla, w_out, g_norm2, w_gate, w_up, w_down):
    b, s, d = x.shape
    depth = w_ada.shape[0]
    diff_width = d // 2
    n_hd = diff_width // DV_DIFF
    n_hm = (d - diff_width) // V_MLA
    assert n_hd == n_hm and n_hd % 2 == 0
    n_heads = n_hd
    wd = n_heads * LANES
    half = QK_ROPE // 2
    tk = _pick_tile(s, ATT_TK)
    nkt = s // tk
    t = tk
    assert t + 1 >= REL_MAX_DIST
    tm = _pick_tile(s, 512)

    cos, sin = _rope_tables(s)
    cosq, sinq = jnp.tile(cos, (1, n_heads)), jnp.tile(sin, (1, n_heads))
    cosk = jnp.tile(cos, (1, 4))
    sink = jnp.tile(jnp.concatenate([-sin, sin], axis=1), (1, 2))

    c_pad = jnp.pad(c, ((0, (-b) % 8), (0, 0)))

    for l in range(depth):
        lambda_init = 0.8 - 0.6 * math.exp(-0.3 * l)
        mod = _ada(c_pad, w_ada[l], b_ada[l][None, :])[:b]
        sh1, sc1, gt1, sh2, sc2, gt2 = [m[:, None, :] for m in jnp.split(mod, 6, axis=-1)]

        wi = w_in[l]
        o = 3 * wd + Q_LORA + KV_LORA
        w_kpe = wi[:, o:o + QK_ROPE]
        w_kpe_sw = jnp.concatenate([w_kpe[:, half:], w_kpe[:, :half]], axis=1)
        wqkv = wi[:, :3 * wd].astype(BF16)
        wlat = jnp.concatenate([wi[:, 3 * wd:o], w_kpe, w_kpe, w_kpe_sw, w_kpe_sw], axis=1).astype(BF16)
        wq = w_q_b[l].reshape(Q_LORA, n_heads, QK_HEAD)
        wqb = jnp.concatenate([wq[:, :, :QK_NOPE].reshape(Q_LORA, -1),
                               wq[:, :, QK_NOPE:QK_NOPE + half].reshape(Q_LORA, -1),
                               wq[:, :, QK_NOPE + half:].reshape(Q_LORA, -1)], axis=1).astype(BF16)
        wkv = w_kv_b[l].reshape(KV_LORA, n_heads, QK_NOPE + V_MLA)
        wkvb = jnp.concatenate([wkv[:, :, :QK_NOPE].reshape(KV_LORA, -1),
                                wkv[:, :, QK_NOPE:].reshape(KV_LORA, -1)], axis=1).astype(BF16)
        gq, gk = g_q_mla[l], g_k_mla[l]
        gk_pe = gk[QK_NOPE:]
        gk_pe_sw = jnp.concatenate([gk_pe[half:], gk_pe[:half]])

        qt_d, kd, vt_d, qt_m, k_m, vt_m = _in_proj(
            x, sc1, sh1, g_norm1[l][None, :], wqkv, wlat,
            jnp.tile(g_q_diff[l], 2)[None, :], jnp.tile(g_k_diff[l], 2)[None, :],
            g_q_a[l][None, :], wqb, g_kv_a[l][None, :], wkvb,
            gq[None, :QK_NOPE], jnp.tile(gq[QK_NOPE:QK_NOPE + half], n_heads)[None, :],
            jnp.tile(gq[QK_NOPE + half:], n_heads)[None, :],
            gk[None, :QK_NOPE], jnp.tile(gk_pe, 2)[None, :], jnp.tile(gk_pe_sw, 2)[None, :],
            cosq, sinq, cosk, sink, n_heads=n_heads, tm=tk)

        bias = _bias_tiles(rel_bias, t)
        gmax = lambda g: jnp.max(jnp.abs(g))
        bound_d = (1.02 * LOG2E * DH_DIFF ** 0.5 * gmax(g_q_diff[l]) * gmax(g_k_diff[l])
                   + LOG2E * gmax(rel_bias))
        bound_m = 1.02 * LOG2E * QK_HEAD ** 0.5 * gmax(g_q_mla[l]) * gmax(g_k_mla[l])
        diff_args = (rel_bias, lambda_vecs[l], g_subln[l][:, None], qt_d, kd, vt_d, bias)
        a_out = lax.cond(
            bound_d <= SCORE_BOUND,
            lambda *a: _diff_attn(*a, lambda_init=lambda_init, bounded=True),
            lambda *a: _diff_attn(*a, lambda_init=lambda_init, bounded=False), *diff_args)
        b_out = lax.cond(
            bound_m <= SCORE_BOUND,
            lambda *a: _mla_attn(*a, bounded=True),
            lambda *a: _mla_attn(*a, bounded=False), qt_m, k_m, vt_m)

        wo = w_out[l].astype(BF16)
        x, h2 = _out_proj(x, gt1, a_out, b_out, wo[:diff_width], wo[diff_width:], sc2, sh2,
                          g_norm2[l][None, :], tm=tm)
        x = _ffn(x, h2, gt2, w_gate[l].astype(BF16), w_up[l].astype(BF16), w_down[l].astype(BF16),
                 tm=_pick_tile(s, FFN_TM), tf=_pick_tile(w_gate.shape[-1], FFN_TF))
    return x
```

```python
import functools
import math

import jax
import jax.numpy as jnp
from jax import lax
from jax.experimental import pallas as pl
from jax.experimental.pallas import tpu as pltpu

F32 = jnp.float32
BF16 = jnp.bfloat16

DH_DIFF = 64
DV_DIFF = 2 * DH_DIFF
QK_NOPE = 128
QK_ROPE = 64
QK_HEAD = QK_NOPE + QK_ROPE
V_MLA = 128
Q_LORA = 512
KV_LORA = 256
ROPE_THETA = 10000.0
REL_BUCKETS = 32
REL_MAX_DIST = 128
EPS = 1e-6
LOG2E = 1.4426950408889634

LANES = 128
SUBLANES = 8
VMEM_LIMIT_BYTES = 56 * 1024 * 1024

OUT_PROJ_SPLIT = 4
FFN_TM = 1024
FFN_TF = 512
ATT_TK = 512
DIFF_TQ = 512
MLA_TQ = 512
KEY_CHUNK = 256
BOUNDED_DEPTH = 2
BOUNDED_Q_TILES = 4
SCORE_BOUND = 40.0


def _cparams(sem):
    return pltpu.CompilerParams(dimension_semantics=sem, vmem_limit_bytes=VMEM_LIMIT_BYTES)


def _const_spec(shape):
    nd = len(shape)
    return pl.BlockSpec(shape, lambda *_: (0,) * nd, pipeline_mode=pl.Buffered(1))


def _ada_kernel(c_ref, w_ref, b_ref, o_ref):
    c = c_ref[...]
    ca = c * jax.nn.sigmoid(c)
    o_ref[...] = jnp.dot(ca.astype(BF16), w_ref[...].astype(BF16), preferred_element_type=F32) + b_ref[...]


def _ada(c_pad, w, b, tn=1024):
    m, d = c_pad.shape
    n = w.shape[1]
    return pl.pallas_call(
        _ada_kernel,
        grid=(n // tn,),
        in_specs=[pl.BlockSpec((m, d), lambda j: (0, 0)),
                  pl.BlockSpec((d, tn), lambda j: (0, j)),
                  pl.BlockSpec((1, tn), lambda j: (0, j))],
        out_specs=pl.BlockSpec((m, tn), lambda j: (0, j)),
        out_shape=jax.ShapeDtypeStruct((m, n), F32),
        compiler_params=_cparams(("arbitrary",)),
        name="ada",
    )(c_pad, w, b)


def _modulated_norm(x, g, sc, sh):
    ms = jnp.mean(x * x, axis=-1, keepdims=True)
    return (x * lax.rsqrt(ms + EPS) * g) * (1.0 + sc) + sh


def _half_lane_norm(blk, g2, lo_mask, out_scale):
    sq = blk * blk
    s_lo = jnp.sum(jnp.where(lo_mask, sq, 0.0), axis=-1, keepdims=True)
    s_hi = jnp.sum(jnp.where(lo_mask, 0.0, sq), axis=-1, keepdims=True)
    inv = jnp.where(lo_mask, lax.rsqrt(s_lo * (1.0 / DH_DIFF) + EPS),
                    lax.rsqrt(s_hi * (1.0 / DH_DIFF) + EPS))
    return blk * inv * (g2 * out_scale)


def _in_proj_kernel(x_ref, sc_ref, sh_ref, g1_ref, wqkv_ref, wlat_ref, gqd_ref, gkd_ref,
                    gqa_ref, wqb_ref, gkva_ref, wkvb_ref, gqn_ref, gq1_ref, gq2_ref,
                    gkn_ref, gkp_ref, gkps_ref, cosq_ref, sinq_ref, cosk_ref, sink_ref,
                    qtd_ref, kd_ref, vtd_ref, qtm_ref, km_ref, vtm_ref, *, n_heads):
    h = _modulated_norm(x_ref[0], g1_ref[...], sc_ref[0], sh_ref[0]).astype(BF16)
    tm = h.shape[0]
    wd = n_heads * LANES
    lane = lax.broadcasted_iota(jnp.int32, (tm, LANES), 1)
    lo_mask = lane < DH_DIFF

    qscale = DH_DIFF ** -0.5 * LOG2E
    qd = jnp.dot(h, wqkv_ref[:, 0:wd], preferred_element_type=F32)
    for hh in range(n_heads):
        sl = slice(hh * LANES, (hh + 1) * LANES)
        qtd_ref[0, hh] = _half_lane_norm(qd[:, sl], gqd_ref[...], lo_mask, qscale).T.astype(BF16)
    kd = jnp.dot(h, wqkv_ref[:, wd:2 * wd], preferred_element_type=F32)
    for hh in range(n_heads):
        sl = slice(hh * LANES, (hh + 1) * LANES)
        kd_ref[0, :, sl] = _half_lane_norm(kd[:, sl], gkd_ref[...], lo_mask, 1.0).astype(BF16)
    vd = jnp.dot(h, wqkv_ref[:, 2 * wd:3 * wd], preferred_element_type=F32)
    for hh in range(n_heads):
        vtd_ref[0, hh, 0] = vd[:, hh * LANES:(hh + 1) * LANES].T.astype(BF16)

    lat = jnp.dot(h, wqkv_ref[:, 3 * wd:3 * wd + Q_LORA + KV_LORA], preferred_element_type=F32)
    cq = lat[:, 0:Q_LORA]
    ckv = lat[:, Q_LORA:Q_LORA + KV_LORA]
    kpe4 = jnp.dot(h, wlat_ref[...], preferred_element_type=F32)
    kpe2 = kpe4[:, 0:LANES]
    kpes2 = kpe4[:, LANES:2 * LANES]

    def rms(v, g):
        return v * lax.rsqrt(jnp.mean(v * v, axis=-1, keepdims=True) + EPS) * g

    kv = jnp.dot(rms(ckv, gkva_ref[...]).astype(BF16), wkvb_ref[...], preferred_element_type=F32)
    for hh in range(n_heads):
        vtm_ref[0, hh, 0] = kv[:, wd + hh * LANES:wd + (hh + 1) * LANES].T.astype(BF16)
    ss_pe = jnp.sum(jnp.where(lo_mask, kpe2 * kpe2, 0.0), axis=-1, keepdims=True)
    kr2 = kpe2 * gkp_ref[...] * cosk_ref[...] + kpes2 * gkps_ref[...] * sink_ref[...]
    for hh in range(n_heads):
        sl = slice(hh * LANES, (hh + 1) * LANES)
        kn = kv[:, sl]
        ss = jnp.sum(kn * kn, axis=-1, keepdims=True) + ss_pe
        r = lax.rsqrt(ss * (1.0 / QK_HEAD) + EPS)
        km_ref[0, hh, :, 0:QK_NOPE] = (kn * r * gkn_ref[...]).astype(BF16)
        km_ref[0, hh, :, QK_NOPE:QK_HEAD] = (kr2 * r)[:, 0:QK_ROPE].astype(BF16)

    qm = jnp.dot(rms(cq, gqa_ref[...]).astype(BF16), wqb_ref[...], preferred_element_type=F32)
    half = QK_ROPE // 2
    wx = n_heads * half
    x1 = qm[:, wd:wd + wx]
    x2 = qm[:, wd + wx:wd + 2 * wx]
    lane_x = lax.broadcasted_iota(jnp.int32, (tm, wx), 1) // half
    sq_x = x1 * x1 + x2 * x2
    mscale = QK_HEAD ** -0.5 * LOG2E
    inv_x = jnp.zeros((tm, wx), F32)
    for hh in range(n_heads):
        sl = slice(hh * LANES, (hh + 1) * LANES)
        qn = qm[:, sl]
        ss = (jnp.sum(qn * qn, axis=-1, keepdims=True)
              + jnp.sum(jnp.where(lane_x == hh, sq_x, 0.0), axis=-1, keepdims=True))
        r = lax.rsqrt(ss * (1.0 / QK_HEAD) + EPS) * mscale
        qtm_ref[0, hh, 0:QK_NOPE, :] = (qn * r * gqn_ref[...]).T.astype(BF16)
        inv_x = jnp.where(lane_x == hh, r, inv_x)
    a1 = x1 * gq1_ref[...]
    a2 = x2 * gq2_ref[...]
    cq_t = cosq_ref[...]
    sq_t = sinq_ref[...]
    x1t = ((a1 * cq_t - a2 * sq_t) * inv_x).T.astype(BF16)
    x2t = ((a2 * cq_t + a1 * sq_t) * inv_x).T.astype(BF16)
    for hh in range(n_heads):
        qtm_ref[0, hh, QK_NOPE:QK_NOPE + half, :] = x1t[hh * half:(hh + 1) * half]
        qtm_ref[0, hh, QK_NOPE + half:QK_HEAD, :] = x2t[hh * half:(hh + 1) * half]


def _in_proj(x, sc1, sh1, g1, wqkv, wlat, gqd, gkd, gqa, wqb, gkva, wkvb, gqn, gq1, gq2,
             gkn, gkp, gkps, cosq, sinq, cosk, sink, *, n_heads, tm):
    b, s, d = x.shape
    wd = n_heads * LANES
    wx = n_heads * (QK_ROPE // 2)
    row = lambda w: pl.BlockSpec((1, tm, w), lambda bi, i: (bi, i, 0))
    mod = pl.BlockSpec((1, 1, d), lambda bi, i: (bi, 0, 0))
    tab = lambda w: pl.BlockSpec((tm, w), lambda bi, i: (i, 0))
    consts = [g1, wqkv, wlat, gqd, gkd, gqa, wqb, gkva, wkvb, gqn, gq1, gq2, gkn, gkp, gkps]
    return pl.pallas_call(
        functools.partial(_in_proj_kernel, n_heads=n_heads),
        grid=(b, s // tm),
        in_specs=[row(d), mod, mod] + [_const_spec(a.shape) for a in consts]
                 + [tab(wx), tab(wx), tab(LANES), tab(LANES)],
        out_specs=[pl.BlockSpec((1, n_heads, LANES, tm), lambda bi, i: (bi, 0, 0, i)),
                   row(wd),
                   pl.BlockSpec((1, n_heads, 1, DV_DIFF, tm), lambda bi, i: (bi, 0, i, 0, 0)),
                   pl.BlockSpec((1, n_heads, QK_HEAD, tm), lambda bi, i: (bi, 0, 0, i)),
                   pl.BlockSpec((1, n_heads, tm, QK_HEAD), lambda bi, i: (bi, 0, i, 0)),
                   pl.BlockSpec((1, n_heads, 1, V_MLA, tm), lambda bi, i: (bi, 0, i, 0, 0))],
        out_shape=[jax.ShapeDtypeStruct((b, n_heads, LANES, s), BF16),
                   jax.ShapeDtypeStruct((b, s, wd), BF16),
                   jax.ShapeDtypeStruct((b, n_heads, s // tm, DV_DIFF, tm), BF16),
                   jax.ShapeDtypeStruct((b, n_heads, QK_HEAD, s), BF16),
                   jax.ShapeDtypeStruct((b, n_heads, s, QK_HEAD), BF16),
                   jax.ShapeDtypeStruct((b, n_heads, s // tm, V_MLA, tm), BF16)],
        compiler_params=_cparams(("parallel", "parallel")),
        name="in_proj",
    )(x, sc1, sh1, *consts, cosq, sinq, cosk, sink)


def _t5_bucket(rel):
    nb = REL_BUCKETS // 2
    max_exact = nb // 2
    base = jnp.where(rel > 0, nb, 0)
    n = jnp.abs(rel)
    nf = jnp.maximum(n, 1).astype(jnp.float32)
    large = max_exact + (jnp.log(nf / max_exact) / math.log(REL_MAX_DIST / max_exact)
                         * (nb - max_exact)).astype(jnp.int32)
    large = jnp.minimum(large, nb - 1)
    return base + jnp.where(n < max_exact, n, large)


def _bias_kernel(rb_ref, bucket_ref, o_ref):
    hh = pl.program_id(0)
    t = o_ref.shape[-1]
    far_left = rb_ref[REL_BUCKETS // 2 - 1, hh]
    far_right = rb_ref[REL_BUCKETS - 1, hh]
    o_ref[0, 0] = jnp.full((t, t), far_left, F32) * LOG2E
    o_ref[0, 4] = jnp.full((t, t), far_right, F32) * LOG2E
    bk_strip = bucket_ref[...]
    strip = jnp.zeros(bk_strip.shape, F32)
    for bk in range(REL_BUCKETS):
        strip = jnp.where(bk_strip == bk, rb_ref[bk, hh], strip)
    strip = strip * LOG2E
    for d in (-1, 0, 1):
        diag = jnp.concatenate([strip[:, (2 - d) * t:(3 - d) * t], strip[:, (1 - d) * t:(2 - d) * t]], axis=1)
        rows = jnp.broadcast_to(diag, (t, 2 * t))
        o_ref[0, d + 2] = pltpu.roll(rows, 0, 1, stride=1, stride_axis=0)[:, :t]


def _bias_tiles(rel_bias, t):
    n_heads = rel_bias.shape[1]
    buckets = _t5_bucket(2 * t - jnp.arange(4 * t, dtype=jnp.int32))[None, :]
    return pl.pallas_call(
        _bias_kernel,
        grid=(n_heads,),
        in_specs=[pl.BlockSpec(memory_space=pltpu.SMEM),
                  pl.BlockSpec((1, 4 * t), lambda hh: (0, 0))],
        out_specs=pl.BlockSpec((1, 5, t, t), lambda hh: (hh, 0, 0, 0)),
        out_shape=jax.ShapeDtypeStruct((n_heads, 5, t, t), F32),
        compiler_params=_cparams(("arbitrary",)),
        name="rel_bias_tiles",
    )(rel_bias, buckets)


ONES_ROWS = 16


def _softmax_state_step(s, offset, vt, state):
    m_new, alpha, p = _softmax_weights(s, offset, state[0])
    return m_new, _accumulate(alpha, state[1], vt, p)


def _softmax_weights(s, offset, m_old):
    m_new = jnp.maximum(m_old, jnp.max(s, axis=0, keepdims=True) + offset)
    alpha = jnp.exp2(m_old - m_new)
    p = jnp.exp2(s - (m_new - offset)).astype(BF16)
    return m_new, alpha, p


def _accumulate(alpha, acc_old, vt, p):
    return alpha * acc_old + jnp.dot(vt, p, preferred_element_type=F32)


def _init_state(dv, t):
    return (jnp.full((1, t), -jnp.inf, F32), jnp.zeros((dv + ONES_ROWS, t), F32))


def _with_ones(vt):
    return jnp.concatenate([vt, jnp.ones((ONES_ROWS, vt.shape[1]), vt.dtype)], axis=0)


def _normalised(state, dv):
    _, acc = state
    return acc[:dv] / acc[dv:dv + 1]


def _diff_attn_kernel(rb_ref, lamv_ref, gsub_ref, qt_ref, k_ref, vt_ref, bias_ref, o_ref, *,
                      lambda_init, bounded):
    hh = pl.program_id(0)
    t = bias_ref.shape[-1]
    nq = qt_ref.shape[-1] // t
    nkt = vt_ref.shape[2]

    off_left = rb_ref[REL_BUCKETS // 2 - 1, hh] * LOG2E
    off_right = rb_ref[REL_BUCKETS - 1, hh] * LOG2E
    lv = lamv_ref[...]
    lam = (jnp.exp(jnp.sum(lv[0:1] * lv[1:2], axis=-1, keepdims=True))
           - jnp.exp(jnp.sum(lv[2:3] * lv[3:4], axis=-1, keepdims=True)) + lambda_init)

    n_near = min(nkt, 3)
    row = lax.broadcasted_iota(jnp.int32, (LANES, t), 0)
    zero = jnp.zeros((LANES, t), BF16)

    def query_maps(j):
        qt = qt_ref[0, 0, :, j * t:(j + 1) * t]
        return jnp.where(row < DH_DIFF, qt, zero), jnp.where(row < DH_DIFF, zero, qt)

    def tile_index(j, r):
        return lax.rem(pl.program_id(2) * nq + j - 1 + r + nkt, nkt)

    def far_offset(j, r):
        return jnp.where(tile_index(j, r) < pl.program_id(2) * nq + j, off_left, off_right)

    def biased_scores(j, qmap, r, k0, ck):
        kj = tile_index(j, r)
        k_t = k_ref[0, pl.ds(pl.multiple_of(kj * t + k0, ck), ck), :]
        s = jnp.dot(k_t, qmap, preferred_element_type=F32)
        if r < n_near:
            s = s + bias_ref[0, jnp.clip(kj - (pl.program_id(2) * nq + j), -2, 2) + 2, k0:k0 + ck, :]
        elif bounded:
            s = s + far_offset(j, r)
        return s

    def finish(j, outs):
        o = outs[0] - lam * outs[1]
        ms = jnp.mean(o * o, axis=0, keepdims=True)
        y = o * lax.rsqrt(ms + EPS) * (gsub_ref[...] * (1.0 - lambda_init))
        o_ref[0, j * t:(j + 1) * t, :] = y.T.astype(BF16)

    if bounded:
        ck = min(t, KEY_CHUNK)
        npt = t // ck
        qmaps = [query_maps(j) for j in range(nq)]
        items = [(j, r, c, mp) for j in range(nq) for r in range(nkt) for c in range(npt) for mp in range(2)]
        score = lambda it: biased_scores(it[0], qmaps[it[0]][it[3]], it[1], it[2] * ck, ck)
        depth = min(BOUNDED_DEPTH, len(items))
        pending = [score(it) for it in items[:depth]]
        for n, (j, r, c, mp) in enumerate(items):
            if (r, c, mp) == (0, 0, 0):
                accs = [jnp.zeros((DV_DIFF, t), F32) for _ in range(2)]
                dens = [jnp.zeros((SUBLANES, t), F32) for _ in range(2)]
            s = pending.pop(0)
            if n + depth < len(items):
                pending.append(score(items[n + depth]))
            p = jnp.exp2(s)
            dens[mp] = dens[mp] + jnp.sum(p.reshape(ck // SUBLANES, SUBLANES, t), axis=0)
            accs[mp] = accs[mp] + jnp.dot(vt_ref[0, 0, tile_index(j, r), :, c * ck:(c + 1) * ck],
                                          p.astype(BF16), preferred_element_type=F32)
            if (r, c, mp) == (nkt - 1, npt - 1, 1):
                finish(j, [acc / jnp.sum(den, axis=0, keepdims=True) for acc, den in zip(accs, dens)])
    else:
        for j in range(nq):
            qmap = query_maps(j)
            states = [_init_state(DV_DIFF, t), _init_state(DV_DIFF, t)]
            score = lambda r: [biased_scores(j, qmap[mp], r, 0, t) for mp in range(2)]
            s_next = score(0)
            for r in range(nkt):
                s_cur = s_next
                if r + 1 < nkt:
                    s_next = score(r + 1)
                vt = _with_ones(vt_ref[0, 0, tile_index(j, r)])
                offset = 0.0 if r < n_near else far_offset(j, r)
                for mp in range(2):
                    states[mp] = _softmax_state_step(s_cur[mp], offset, vt, states[mp])
            finish(j, [_normalised(st, DV_DIFF) for st in states])


def _diff_attn(rel_bias, lamv, gsub_col, qt, k, vt, bias, *, lambda_init, bounded):
    b, n_heads, _, s = qt.shape
    t = bias.shape[-1]
    nkt, tk = vt.shape[2], vt.shape[4]
    assert tk == t
    tq = _pick_tile(s, t * (BOUNDED_Q_TILES if bounded else 1))
    return pl.pallas_call(
        functools.partial(_diff_attn_kernel, lambda_init=lambda_init, bounded=bounded),
        grid=(n_heads, b, s // tq),
        in_specs=[pl.BlockSpec(memory_space=pltpu.SMEM),
                  pl.BlockSpec(lamv.shape, lambda hh, bi, qi: (0, 0)),
                  pl.BlockSpec(gsub_col.shape, lambda hh, bi, qi: (0, 0)),
                  pl.BlockSpec((1, 1, LANES, tq), lambda hh, bi, qi: (bi, hh, 0, qi)),
                  pl.BlockSpec((1, s, LANES), lambda hh, bi, qi: (bi, 0, hh)),
                  pl.BlockSpec((1, 1, nkt, DV_DIFF, tk), lambda hh, bi, qi: (bi, hh, 0, 0, 0)),
                  pl.BlockSpec((1, 5, t, t), lambda hh, bi, qi: (hh, 0, 0, 0))],
        out_specs=pl.BlockSpec((1, tq, DV_DIFF), lambda hh, bi, qi: (bi, qi, hh)),
        out_shape=jax.ShapeDtypeStruct((b, s, n_heads * DV_DIFF), BF16),
        compiler_params=_cparams(("parallel", "parallel", "parallel")),
        name="diff_attn",
    )(rel_bias, lamv, gsub_col, qt, k, vt, bias)


def _mla_attn_kernel(qt_ref, k_ref, vt_ref, o_ref):
    nkt = vt_ref.shape[2]
    tk = vt_ref.shape[-1]
    qt = qt_ref[0, 0]
    m, acc = _init_state(V_MLA, qt.shape[-1])
    scores = lambda kj: jnp.dot(k_ref[0, 0, kj * tk:(kj + 1) * tk, :], qt, preferred_element_type=F32)
    depth = 2
    pending = [scores(kj) for kj in range(min(depth, nkt))]
    for kj in range(nkt):
        s = pending.pop(0)
        if kj + depth < nkt:
            pending.append(scores(kj + depth))
        m, alpha, p = _softmax_weights(s, 0.0, m)
        acc = _accumulate(alpha, acc, _with_ones(vt_ref[0, 0, kj]), p)
    o_ref[0] = _normalised((m, acc), V_MLA).T.astype(BF16)


def _mla_attn_bounded_kernel(qt_ref, k_ref, vt_ref, o_ref):
    nkt = vt_ref.shape[2]
    tk = vt_ref.shape[-1]
    t = min(qt_ref.shape[-1], MLA_TQ)
    nq = qt_ref.shape[-1] // t
    ck = min(tk, KEY_CHUNK)
    npt = tk // ck
    n_chunks = nkt * npt
    items = [(j, i) for j in range(nq) for i in range(n_chunks)]

    def scores(item):
        j, i = item
        return jnp.dot(k_ref[0, 0, i * ck:(i + 1) * ck, :], qt_ref[0, 0, :, j * t:(j + 1) * t],
                       preferred_element_type=F32)

    depth = min(BOUNDED_DEPTH, len(items))
    pending = [scores(it) for it in items[:depth]]
    for n, (j, i) in enumerate(items):
        if i == 0:
            acc = jnp.zeros((V_MLA, t), F32)
            den = jnp.zeros((SUBLANES, t), F32)
        s = pending.pop(0)
        if n + depth < len(items):
            pending.append(scores(items[n + depth]))
        c = i % npt
        p = jnp.exp2(s)
        den = den + jnp.sum(p.reshape(ck // SUBLANES, SUBLANES, t), axis=0)
        acc = acc + jnp.dot(vt_ref[0, 0, i // npt, :, c * ck:(c + 1) * ck], p.astype(BF16),
                            preferred_element_type=F32)
        if i == n_chunks - 1:
            o_ref[0, j * t:(j + 1) * t, :] = (acc / jnp.sum(den, axis=0, keepdims=True)).T.astype(BF16)


def _mla_attn(qt, k, vt, *, bounded):
    b, n_heads, _, s = qt.shape
    t = _pick_tile(s, MLA_TQ * (BOUNDED_Q_TILES if bounded else 1))
    nkt, tk = vt.shape[2], vt.shape[4]
    return pl.pallas_call(
        _mla_attn_bounded_kernel if bounded else _mla_attn_kernel,
        grid=(b, n_heads, s // t),
        in_specs=[pl.BlockSpec((1, 1, QK_HEAD, t), lambda bi, hh, qi: (bi, hh, 0, qi)),
                  pl.BlockSpec((1, 1, s, QK_HEAD), lambda bi, hh, qi: (bi, hh, 0, 0)),
                  pl.BlockSpec((1, 1, nkt, V_MLA, tk), lambda bi, hh, qi: (bi, hh, 0, 0, 0))],
        out_specs=pl.BlockSpec((1, t, V_MLA), lambda bi, hh, qi: (bi, qi, hh)),
        out_shape=jax.ShapeDtypeStruct((b, s, n_heads * V_MLA), BF16),
        compiler_params=_cparams(("parallel", "parallel", "parallel")),
        name="mla_attn",
    )(qt, k, vt)


def _out_proj_kernel(x_ref, gt_ref, a_ref, b_ref, wa_ref, wb_ref, sc_ref, sh_ref, g2_ref, o_ref, h_ref):
    tm = x_ref.shape[1]
    rows = tm // OUT_PROJ_SPLIT
    mixes = []
    for i in range(OUT_PROJ_SPLIT):
        sl = slice(i * rows, (i + 1) * rows)
        mixes.append(jnp.dot(a_ref[0, sl, :], wa_ref[...], preferred_element_type=F32)
                     + jnp.dot(b_ref[0, sl, :], wb_ref[...], preferred_element_type=F32))
    for i in range(OUT_PROJ_SPLIT):
        sl = slice(i * rows, (i + 1) * rows)
        x1 = x_ref[0, sl, :] + gt_ref[0] * mixes[i]
        o_ref[0, sl, :] = x1
        h_ref[0, sl, :] = _modulated_norm(x1, g2_ref[...], sc_ref[0], sh_ref[0]).astype(BF16)


def _out_proj(x, gt1, a, bb, wa, wb, sc2, sh2, g2, *, tm):
    b, s, d = x.shape
    row = lambda w: pl.BlockSpec((1, tm, w), lambda bi, i: (bi, i, 0))
    mod = pl.BlockSpec((1, 1, d), lambda bi, i: (bi, 0, 0))
    return pl.pallas_call(
        _out_proj_kernel,
        grid=(b, s // tm),
        in_specs=[row(d), mod, row(a.shape[-1]), row(bb.shape[-1]), _const_spec(wa.shape),
                  _const_spec(wb.shape), mod, mod, _const_spec(g2.shape)],
        out_specs=[row(d), row(d)],
        out_shape=[jax.ShapeDtypeStruct((b, s, d), F32), jax.ShapeDtypeStruct((b, s, d), BF16)],
        compiler_params=_cparams(("parallel", "parallel")),
        name="out_proj",
    )(x, gt1, a, bb, wa, wb, sc2, sh2, g2)


def _ffn_kernel(x_ref, h_ref, gt_ref, wg_ref, wu_ref, wd_ref, o_ref):
    j = pl.program_id(2)

    @pl.when(j == 0)
    def _():
        o_ref[...] = jnp.zeros(o_ref.shape, F32)

    h = h_ref[0]
    g = jnp.dot(h, wg_ref[...], preferred_element_type=F32)
    u = jnp.dot(h, wu_ref[...], preferred_element_type=F32)
    act = (g * jax.nn.sigmoid(g) * u).astype(BF16)
    o_ref[0] += jnp.dot(act, wd_ref[...], preferred_element_type=F32)

    @pl.when(j == pl.num_programs(2) - 1)
    def _():
        o_ref[0] = x_ref[0] + gt_ref[0] * o_ref[0]


def _ffn(x, h2, gt2, wg, wu, wd, *, tm, tf):
    b, s, d = x.shape
    dff = wg.shape[1]
    row = pl.BlockSpec((1, tm, d), lambda bi, i, j: (bi, i, 0))
    out_row = pl.BlockSpec((1, tm, d), lambda bi, i, j: (bi, i, 0), pipeline_mode=pl.Buffered(1))
    return pl.pallas_call(
        _ffn_kernel,
        grid=(b, s // tm, dff // tf),
        in_specs=[row, row, pl.BlockSpec((1, 1, d), lambda bi, i, j: (bi, 0, 0)),
                  pl.BlockSpec((d, tf), lambda bi, i, j: (0, j)),
                  pl.BlockSpec((d, tf), lambda bi, i, j: (0, j)),
                  pl.BlockSpec((tf, d), lambda bi, i, j: (j, 0))],
        out_specs=out_row,
        out_shape=jax.ShapeDtypeStruct((b, s, d), F32),
        compiler_params=_cparams(("parallel", "parallel", "arbitrary")),
        name="ffn",
    )(x, h2, gt2, wg, wu, wd)


def _rope_tables(s):
    pos = jnp.arange(s, dtype=jnp.float32)
    inv = 1.0 / (ROPE_THETA ** (jnp.arange(0, QK_ROPE, 2, dtype=jnp.float32) / QK_ROPE))
    ang = pos[:, None] * inv[None, :]
    return jnp.cos(ang), jnp.sin(ang)


def _pick_tile(n, pref):
    t = min(pref, n)
    while n % t:
        t //= 2
    return t


def kernel(x, c, rel_bias, w_ada, b_ada, g_norm1, w_in, g_q_diff, g_k_diff, lambda_vecs, g_subln, g_q_a, w_q_b, g_kv_a, w_kv_b, g_q_mla, g_k_mla, w_out, g_norm2, w_gate, w_up, w_down):
    b, s, d = x.shape
    depth = w_ada.shape[0]
    diff_width = d // 2
    n_hd = diff_width // DV_DIFF
    n_hm = (d - diff_width) // V_MLA
    assert n_hd == n_hm and n_hd % 2 == 0
    n_heads = n_hd
    wd = n_heads * LANES
    half = QK_ROPE // 2
    tk = _pick_tile(s, ATT_TK)
    nkt = s // tk
    t = tk
    assert t + 1 >= REL_MAX_DIST
    tm = _pick_tile(s, 512)

    cos, sin = _rope_tables(s)
    cosq, sinq = jnp.tile(cos, (1, n_heads)), jnp.tile(sin, (1, n_heads))
    cosk = jnp.tile(cos, (1, 4))
    sink = jnp.tile(jnp.concatenate([-sin, sin], axis=1), (1, 2))

    c_pad = jnp.pad(c, ((0, (-b) % 8), (0, 0)))

    for l in range(depth):
        lambda_init = 0.8 - 0.6 * math.exp(-0.3 * l)
        mod = _ada(c_pad, w_ada[l], b_ada[l][None, :])[:b]
        sh1, sc1, gt1, sh2, sc2, gt2 = [m[:, None, :] for m in jnp.split(mod, 6, axis=-1)]

        wi = w_in[l]
        o = 3 * wd + Q_LORA + KV_LORA
        w_kpe = wi[:, o:o + QK_ROPE]
        w_kpe_sw = jnp.concatenate([w_kpe[:, half:], w_kpe[:, :half]], axis=1)
        wqkv = wi.astype(BF16)
        wlat = jnp.concatenate([w_kpe, w_kpe, w_kpe_sw, w_kpe_sw], axis=1).astype(BF16)
        wq = w_q_b[l].reshape(Q_LORA, n_heads, QK_HEAD)
        wqb = jnp.concatenate([wq[:, :, :QK_NOPE].reshape(Q_LORA, -1),
                               wq[:, :, QK_NOPE:QK_NOPE + half].reshape(Q_LORA, -1),
                               wq[:, :, QK_NOPE + half:].reshape(Q_LORA, -1)], axis=1).astype(BF16)
        wkv = w_kv_b[l].reshape(KV_LORA, n_heads, QK_NOPE + V_MLA)
        wkvb = jnp.concatenate([wkv[:, :, :QK_NOPE].reshape(KV_LORA, -1),
                                wkv[:, :, QK_NOPE:].reshape(KV_LORA, -1)], axis=1).astype(BF16)
        gq, gk = g_q_mla[l], g_k_mla[l]
        gk_pe = gk[QK_NOPE:]
        gk_pe_sw = jnp.concatenate([gk_pe[half:], gk_pe[:half]])

        qt_d, kd, vt_d, qt_m, k_m, vt_m = _in_proj(
            x, sc1, sh1, g_norm1[l][None, :], wqkv, wlat,
            jnp.tile(g_q_diff[l], 2)[None, :], jnp.tile(g_k_diff[l], 2)[None, :],
            g_q_a[l][None, :], wqb, g_kv_a[l][None, :], wkvb,
            gq[None, :QK_NOPE], jnp.tile(gq[QK_NOPE:QK_NOPE + half], n_heads)[None, :],
            jnp.tile(gq[QK_NOPE + half:], n_heads)[None, :],
            gk[None, :QK_NOPE], jnp.tile(gk_pe, 2)[None, :], jnp.tile(gk_pe_sw, 2)[None, :],
            cosq, sinq, cosk, sink, n_heads=n_heads, tm=tk)

        bias = _bias_tiles(rel_bias, t)
        gmax = lambda g: jnp.max(jnp.abs(g))
        bound_d = (1.02 * LOG2E * DH_DIFF ** 0.5 * gmax(g_q_diff[l]) * gmax(g_k_diff[l])
                   + LOG2E * gmax(rel_bias))
        bound_m = 1.02 * LOG2E * QK_HEAD ** 0.5 * gmax(g_q_mla[l]) * gmax(g_k_mla[l])
        diff_args = (rel_bias, lambda_vecs[l], g_subln[l][:, None], qt_d, kd, vt_d, bias)
        a_out = lax.cond(
            bound_d <= SCORE_BOUND,
            lambda *a: _diff_attn(*a, lambda_init=lambda_init, bounded=True),
            lambda *a: _diff_attn(*a, lambda_init=lambda_init, bounded=False), *diff_args)
        b_out = lax.cond(
            bound_m <= SCORE_BOUND,
            lambda *a: _mla_attn(*a, bounded=True),
            lambda *a: _mla_attn(*a, bounded=False), qt_m, k_m, vt_m)

        wo = w_out[l].astype(BF16)
        x, h2 = _out_proj(x, gt1, a_out, b_out, wo[:diff_width], wo[diff_width:], sc2, sh2,
                          g_norm2[l][None, :], tm=tm)
        x = _ffn(x, h2, gt2, w_gate[l].astype(BF16), w_up[l].astype(BF16), w_down[l].astype(BF16),
                 tm=_pick_tile(s, FFN_TM), tf=_pick_tile(w_gate.shape[-1], FFN_TF))
    return x
```

```python
import functools
import math

import jax
import jax.numpy as jnp
from jax import lax
from jax.experimental import pallas as pl
from jax.experimental.pallas import tpu as pltpu

F32 = jnp.float32
BF16 = jnp.bfloat16

DH_DIFF = 64
DV_DIFF = 2 * DH_DIFF
QK_NOPE = 128
QK_ROPE = 64
QK_HEAD = QK_NOPE + QK_ROPE
V_MLA = 128
Q_LORA = 512
KV_LORA = 256
ROPE_THETA = 10000.0
REL_BUCKETS = 32
REL_MAX_DIST = 128
EPS = 1e-6
LOG2E = 1.4426950408889634

LANES = 128
SUBLANES = 8
VMEM_LIMIT_BYTES = 56 * 1024 * 1024

OUT_PROJ_SPLIT = 4
FFN_TM = 1024
FFN_TF = 512
ATT_TK = 512
DIFF_TQ = 512
MLA_TQ = 512
KEY_CHUNK = 256
BOUNDED_DEPTH = 2
BOUNDED_Q_TILES = 4
SCORE_BOUND = 40.0


def _cparams(sem):
    return pltpu.CompilerParams(dimension_semantics=sem, vmem_limit_bytes=VMEM_LIMIT_BYTES)


def _const_spec(shape):
    nd = len(shape)
    return pl.BlockSpec(shape, lambda *_: (0,) * nd, pipeline_mode=pl.Buffered(1))


def _ada_kernel(c_ref, w_ref, b_ref, o_ref):
    c = c_ref[...]
    ca = c * jax.nn.sigmoid(c)
    o_ref[...] = jnp.dot(ca.astype(BF16), w_ref[...].astype(BF16), preferred_element_type=F32) + b_ref[...]


def _ada(c_pad, w, b, tn=1024):
    m, d = c_pad.shape
    n = w.shape[1]
    return pl.pallas_call(
        _ada_kernel,
        grid=(n // tn,),
        in_specs=[pl.BlockSpec((m, d), lambda j: (0, 0)),
                  pl.BlockSpec((d, tn), lambda j: (0, j)),
                  pl.BlockSpec((1, tn), lambda j: (0, j))],
        out_specs=pl.BlockSpec((m, tn), lambda j: (0, j)),
        out_shape=jax.ShapeDtypeStruct((m, n), F32),
        compiler_params=_cparams(("arbitrary",)),
        name="ada",
    )(c_pad, w, b)


def _modulated_norm(x, g, sc, sh):
    ms = jnp.mean(x * x, axis=-1, keepdims=True)
    return (x * lax.rsqrt(ms + EPS) * g) * (1.0 + sc) + sh


def _half_lane_norm(blk, g2, lo_mask, out_scale):
    sq = blk * blk
    s_lo = jnp.sum(jnp.where(lo_mask, sq, 0.0), axis=-1, keepdims=True)
    s_hi = jnp.sum(jnp.where(lo_mask, 0.0, sq), axis=-1, keepdims=True)
    inv = jnp.where(lo_mask, lax.rsqrt(s_lo * (1.0 / DH_DIFF) + EPS),
                    lax.rsqrt(s_hi * (1.0 / DH_DIFF) + EPS))
    return blk * inv * (g2 * out_scale)


def _in_proj_kernel(x_ref, sc_ref, sh_ref, g1_ref, wqkv_ref, wlat_ref, gqd_ref, gkd_ref,
                    gqa_ref, wqb_ref, gkva_ref, wkvb_ref, gqn_ref, gq1_ref, gq2_ref,
                    gkn_ref, gkp_ref, gkps_ref, cosq_ref, sinq_ref, cosk_ref, sink_ref,
                    qtd_ref, kd_ref, vtd_ref, qtm_ref, km_ref, vtm_ref, *, n_heads):
    h = _modulated_norm(x_ref[0], g1_ref[...], sc_ref[0], sh_ref[0]).astype(BF16)
    tm = h.shape[0]
    wd = n_heads * LANES
    lane = lax.broadcasted_iota(jnp.int32, (tm, LANES), 1)
    lo_mask = lane < DH_DIFF

    qscale = DH_DIFF ** -0.5 * LOG2E
    qd = jnp.dot(h, wqkv_ref[:, 0:wd], preferred_element_type=F32)
    for hh in range(n_heads):
        sl = slice(hh * LANES, (hh + 1) * LANES)
        qtd_ref[0, hh] = _half_lane_norm(qd[:, sl], gqd_ref[...], lo_mask, qscale).T.astype(BF16)
    kd = jnp.dot(h, wqkv_ref[:, wd:2 * wd], preferred_element_type=F32)
    for hh in range(n_heads):
        sl = slice(hh * LANES, (hh + 1) * LANES)
        kd_ref[0, :, sl] = _half_lane_norm(kd[:, sl], gkd_ref[...], lo_mask, 1.0).astype(BF16)
    vd = jnp.dot(h, wqkv_ref[:, 2 * wd:3 * wd], preferred_element_type=F32)
    for hh in range(n_heads):
        vtd_ref[0, hh, 0] = vd[:, hh * LANES:(hh + 1) * LANES].T.astype(BF16)

    lat = jnp.dot(h, wqkv_ref[:, 3 * wd:3 * wd + Q_LORA + KV_LORA], preferred_element_type=F32)
    cq = lat[:, 0:Q_LORA]
    ckv = lat[:, Q_LORA:Q_LORA + KV_LORA]
    kpe4 = jnp.dot(h, wlat_ref[...], preferred_element_type=F32)
    kpe2 = kpe4[:, 0:LANES]
    kpes2 = kpe4[:, LANES:2 * LANES]

    def rms(v, g):
        return v * lax.rsqrt(jnp.mean(v * v, axis=-1, keepdims=True) + EPS) * g

    kv = jnp.dot(rms(ckv, gkva_ref[...]).astype(BF16), wkvb_ref[...], preferred_element_type=F32)
    for hh in range(n_heads):
        vtm_ref[0, hh, 0] = kv[:, wd + hh * LANES:wd + (hh + 1) * LANES].T.astype(BF16)
    ss_pe = jnp.sum(jnp.where(lo_mask, kpe2 * kpe2, 0.0), axis=-1, keepdims=True)
    kr2 = kpe2 * gkp_ref[...] * cosk_ref[...] + kpes2 * gkps_ref[...] * sink_ref[...]
    for hh in range(n_heads):
        sl = slice(hh * LANES, (hh + 1) * LANES)
        kn = kv[:, sl]
        ss = jnp.sum(kn * kn, axis=-1, keepdims=True) + ss_pe
        r = lax.rsqrt(ss * (1.0 / QK_HEAD) + EPS)
        km_ref[0, hh, :, 0:QK_NOPE] = (kn * r * gkn_ref[...]).astype(BF16)
        km_ref[0, hh, :, QK_NOPE:QK_HEAD] = (kr2 * r)[:, 0:QK_ROPE].astype(BF16)

    qm = jnp.dot(rms(cq, gqa_ref[...]).astype(BF16), wqb_ref[...], preferred_element_type=F32)
    half = QK_ROPE // 2
    wx = n_heads * half
    x1 = qm[:, wd:wd + wx]
    x2 = qm[:, wd + wx:wd + 2 * wx]
    lane_x = lax.broadcasted_iota(jnp.int32, (tm, wx), 1) // half
    sq_x = x1 * x1 + x2 * x2
    mscale = QK_HEAD ** -0.5 * LOG2E
    inv_x = jnp.zeros((tm, wx), F32)
    for hh in range(n_heads):
        sl = slice(hh * LANES, (hh + 1) * LANES)
        qn = qm[:, sl]
        ss = (jnp.sum(qn * qn, axis=-1, keepdims=True)
              + jnp.sum(jnp.where(lane_x == hh, sq_x, 0.0), axis=-1, keepdims=True))
        r = lax.rsqrt(ss * (1.0 / QK_HEAD) + EPS) * mscale
        qtm_ref[0, hh, 0:QK_NOPE, :] = (qn * r * gqn_ref[...]).T.astype(BF16)
        inv_x = jnp.where(lane_x == hh, r, inv_x)
    a1 = x1 * gq1_ref[...]
    a2 = x2 * gq2_ref[...]
    cq_t = cosq_ref[...]
    sq_t = sinq_ref[...]
    x1t = ((a1 * cq_t - a2 * sq_t) * inv_x).T.astype(BF16)
    x2t = ((a2 * cq_t + a1 * sq_t) * inv_x).T.astype(BF16)
    for hh in range(n_heads):
        qtm_ref[0, hh, QK_NOPE:QK_NOPE + half, :] = x1t[hh * half:(hh + 1) * half]
        qtm_ref[0, hh, QK_NOPE + half:QK_HEAD, :] = x2t[hh * half:(hh + 1) * half]


def _in_proj(x, sc1, sh1, g1, wqkv, wlat, gqd, gkd, gqa, wqb, gkva, wkvb, gqn, gq1, gq2,
             gkn, gkp, gkps, cosq, sinq, cosk, sink, *, n_heads, tm):
    b, s, d = x.shape
    wd = n_heads * LANES
    wx = n_heads * (QK_ROPE // 2)
    row = lambda w: pl.BlockSpec((1, tm, w), lambda bi, i: (bi, i, 0))
    mod = pl.BlockSpec((1, 1, d), lambda bi, i: (bi, 0, 0))
    tab = lambda w: pl.BlockSpec((tm, w), lambda bi, i: (i, 0))
    consts = [g1, wqkv, wlat, gqd, gkd, gqa, wqb, gkva, wkvb, gqn, gq1, gq2, gkn, gkp, gkps]
    return pl.pallas_call(
        functools.partial(_in_proj_kernel, n_heads=n_heads),
        grid=(b, s // tm),
        in_specs=[row(d), mod, mod] + [_const_spec(a.shape) for a in consts]
                 + [tab(wx), tab(wx), tab(LANES), tab(LANES)],
        out_specs=[pl.BlockSpec((1, n_heads, LANES, tm), lambda bi, i: (bi, 0, 0, i)),
                   row(wd),
                   pl.BlockSpec((1, n_heads, 1, DV_DIFF, tm), lambda bi, i: (bi, 0, i, 0, 0)),
                   pl.BlockSpec((1, n_heads, QK_HEAD, tm), lambda bi, i: (bi, 0, 0, i)),
                   pl.BlockSpec((1, n_heads, tm, QK_HEAD), lambda bi, i: (bi, 0, i, 0)),
                   pl.BlockSpec((1, n_heads, 1, V_MLA, tm), lambda bi, i: (bi, 0, i, 0, 0))],
        out_shape=[jax.ShapeDtypeStruct((b, n_heads, LANES, s), BF16),
                   jax.ShapeDtypeStruct((b, s, wd), BF16),
                   jax.ShapeDtypeStruct((b, n_heads, s // tm, DV_DIFF, tm), BF16),
                   jax.ShapeDtypeStruct((b, n_heads, QK_HEAD, s), BF16),
                   jax.ShapeDtypeStruct((b, n_heads, s, QK_HEAD), BF16),
                   jax.ShapeDtypeStruct((b, n_heads, s // tm, V_MLA, tm), BF16)],
        compiler_params=_cparams(("parallel", "parallel")),
        name="in_proj",
    )(x, sc1, sh1, *consts, cosq, sinq, cosk, sink)


def _t5_bucket(rel):
    nb = REL_BUCKETS // 2
    max_exact = nb // 2
    base = jnp.where(rel > 0, nb, 0)
    n = jnp.abs(rel)
    nf = jnp.maximum(n, 1).astype(jnp.float32)
    large = max_exact + (jnp.log(nf / max_exact) / math.log(REL_MAX_DIST / max_exact)
                         * (nb - max_exact)).astype(jnp.int32)
    large = jnp.minimum(large, nb - 1)
    return base + jnp.where(n < max_exact, n, large)


def _bias_kernel(rb_ref, bucket_ref, o_ref):
    hh = pl.program_id(0)
    t = o_ref.shape[-1]
    far_left = rb_ref[REL_BUCKETS // 2 - 1, hh]
    far_right = rb_ref[REL_BUCKETS - 1, hh]
    o_ref[0, 0] = jnp.full((t, t), far_left, F32) * LOG2E
    o_ref[0, 4] = jnp.full((t, t), far_right, F32) * LOG2E
    bk_strip = bucket_ref[...]
    strip = jnp.zeros(bk_strip.shape, F32)
    for bk in range(REL_BUCKETS):
        strip = jnp.where(bk_strip == bk, rb_ref[bk, hh], strip)
    strip = strip * LOG2E
    for d in (-1, 0, 1):
        diag = jnp.concatenate([strip[:, (2 - d) * t:(3 - d) * t], strip[:, (1 - d) * t:(2 - d) * t]], axis=1)
        rows = jnp.broadcast_to(diag, (t, 2 * t))
        o_ref[0, d + 2] = pltpu.roll(rows, 0, 1, stride=1, stride_axis=0)[:, :t]


def _bias_tiles(rel_bias, t):
    n_heads = rel_bias.shape[1]
    buckets = _t5_bucket(2 * t - jnp.arange(4 * t, dtype=jnp.int32))[None, :]
    return pl.pallas_call(
        _bias_kernel,
        grid=(n_heads,),
        in_specs=[pl.BlockSpec(memory_space=pltpu.SMEM),
                  pl.BlockSpec((1, 4 * t), lambda hh: (0, 0))],
        out_specs=pl.BlockSpec((1, 5, t, t), lambda hh: (hh, 0, 0, 0)),
        out_shape=jax.ShapeDtypeStruct((n_heads, 5, t, t), F32),
        compiler_params=_cparams(("arbitrary",)),
        name="rel_bias_tiles",
    )(rel_bias, buckets)


ONES_ROWS = 16


def _softmax_state_step(s, offset, vt, state):
    m_new, alpha, p = _softmax_weights(s, offset, state[0])
    return m_new, _accumulate(alpha, state[1], vt, p)


def _softmax_weights(s, offset, m_old):
    m_new = jnp.maximum(m_old, jnp.max(s, axis=0, keepdims=True) + offset)
    alpha = jnp.exp2(m_old - m_new)
    p = jnp.exp2(s - (m_new - offset)).astype(BF16)
    return m_new, alpha, p


def _accumulate(alpha, acc_old, vt, p):
    return alpha * acc_old + jnp.dot(vt, p, preferred_element_type=F32)


def _init_state(dv, t):
    return (jnp.full((1, t), -jnp.inf, F32), jnp.zeros((dv + ONES_ROWS, t), F32))


def _with_ones(vt):
    return jnp.concatenate([vt, jnp.ones((ONES_ROWS, vt.shape[1]), vt.dtype)], axis=0)


def _normalised(state, dv):
    _, acc = state
    return acc[:dv] / acc[dv:dv + 1]


def _diff_attn_kernel(rb_ref, lamv_ref, gsub_ref, qt_ref, k_ref, vt_ref, bias_ref, o_ref, *,
                      lambda_init, bounded):
    hh = pl.program_id(0)
    t = bias_ref.shape[-1]
    nq = qt_ref.shape[-1] // t
    nkt = vt_ref.shape[2]

    off_left = rb_ref[REL_BUCKETS // 2 - 1, hh] * LOG2E
    off_right = rb_ref[REL_BUCKETS - 1, hh] * LOG2E
    lv = lamv_ref[...]
    lam = (jnp.exp(jnp.sum(lv[0:1] * lv[1:2], axis=-1, keepdims=True))
           - jnp.exp(jnp.sum(lv[2:3] * lv[3:4], axis=-1, keepdims=True)) + lambda_init)

    n_near = min(nkt, 3)
    row = lax.broadcasted_iota(jnp.int32, (LANES, t), 0)
    zero = jnp.zeros((LANES, t), BF16)

    def query_maps(j):
        qt = qt_ref[0, 0, :, j * t:(j + 1) * t]
        return jnp.where(row < DH_DIFF, qt, zero), jnp.where(row < DH_DIFF, zero, qt)

    def tile_index(j, r):
        return lax.rem(pl.program_id(2) * nq + j - 1 + r + nkt, nkt)

    def far_offset(j, r):
        return jnp.where(tile_index(j, r) < pl.program_id(2) * nq + j, off_left, off_right)

    def biased_scores(j, qmap, r, k0, ck):
        kj = tile_index(j, r)
        k_t = k_ref[0, pl.ds(pl.multiple_of(kj * t + k0, ck), ck), :]
        s = jnp.dot(k_t, qmap, preferred_element_type=F32)
        if r < n_near:
            s = s + bias_ref[0, jnp.clip(kj - (pl.program_id(2) * nq + j), -2, 2) + 2, k0:k0 + ck, :]
        elif bounded:
            s = s + far_offset(j, r)
        return s

    def finish(j, outs):
        o = outs[0] - lam * outs[1]
        ms = jnp.mean(o * o, axis=0, keepdims=True)
        y = o * lax.rsqrt(ms + EPS) * (gsub_ref[...] * (1.0 - lambda_init))
        o_ref[0, j * t:(j + 1) * t, :] = y.T.astype(BF16)

    if bounded:
        ck = min(t, KEY_CHUNK)
        npt = t // ck
        qmaps = [query_maps(j) for j in range(nq)]
        items = [(j, r, c, mp) for j in range(nq) for r in range(nkt) for c in range(npt) for mp in range(2)]
        def weights(it):
            p = jnp.exp2(biased_scores(it[0], qmaps[it[0]][it[3]], it[1], it[2] * ck, ck))
            return p.astype(BF16), jnp.sum(p.reshape(ck // SUBLANES, SUBLANES, t), axis=0)

        depth = min(BOUNDED_DEPTH, len(items))
        pending = [weights(it) for it in items[:depth]]
        for n, (j, r, c, mp) in enumerate(items):
            if (r, c, mp) == (0, 0, 0):
                accs = [jnp.zeros((DV_DIFF, t), F32) for _ in range(2)]
                dens = [jnp.zeros((SUBLANES, t), F32) for _ in range(2)]
            p, psum = pending.pop(0)
            if n + depth < len(items):
                pending.append(weights(items[n + depth]))
            dens[mp] = dens[mp] + psum
            accs[mp] = accs[mp] + jnp.dot(vt_ref[0, 0, tile_index(j, r), :, c * ck:(c + 1) * ck], p,
                                          preferred_element_type=F32)
            if (r, c, mp) == (nkt - 1, npt - 1, 1):
                finish(j, [acc / jnp.sum(den, axis=0, keepdims=True) for acc, den in zip(accs, dens)])
    else:
        for j in range(nq):
            qmap = query_maps(j)
            states = [_init_state(DV_DIFF, t), _init_state(DV_DIFF, t)]
            score = lambda r: [biased_scores(j, qmap[mp], r, 0, t) for mp in range(2)]
            s_next = score(0)
            for r in range(nkt):
                s_cur = s_next
                if r + 1 < nkt:
                    s_next = score(r + 1)
                vt = _with_ones(vt_ref[0, 0, tile_index(j, r)])
                offset = 0.0 if r < n_near else far_offset(j, r)
                for mp in range(2):
                    states[mp] = _softmax_state_step(s_cur[mp], offset, vt, states[mp])
            finish(j, [_normalised(st, DV_DIFF) for st in states])


def _diff_attn(rel_bias, lamv, gsub_col, qt, k, vt, bias, *, lambda_init, bounded):
    b, n_heads, _, s = qt.shape
    t = bias.shape[-1]
    nkt, tk = vt.shape[2], vt.shape[4]
    assert tk == t
    tq = _pick_tile(s, t * (BOUNDED_Q_TILES if bounded else 1))
    return pl.pallas_call(
        functools.partial(_diff_attn_kernel, lambda_init=lambda_init, bounded=bounded),
        grid=(n_heads, b, s // tq),
        in_specs=[pl.BlockSpec(memory_space=pltpu.SMEM),
                  pl.BlockSpec(lamv.shape, lambda hh, bi, qi: (0, 0)),
                  pl.BlockSpec(gsub_col.shape, lambda hh, bi, qi: (0, 0)),
                  pl.BlockSpec((1, 1, LANES, tq), lambda hh, bi, qi: (bi, hh, 0, qi)),
                  pl.BlockSpec((1, s, LANES), lambda hh, bi, qi: (bi, 0, hh)),
                  pl.BlockSpec((1, 1, nkt, DV_DIFF, tk), lambda hh, bi, qi: (bi, hh, 0, 0, 0)),
                  pl.BlockSpec((1, 5, t, t), lambda hh, bi, qi: (hh, 0, 0, 0))],
        out_specs=pl.BlockSpec((1, tq, DV_DIFF), lambda hh, bi, qi: (bi, qi, hh)),
        out_shape=jax.ShapeDtypeStruct((b, s, n_heads * DV_DIFF), BF16),
        compiler_params=_cparams(("parallel", "parallel", "parallel")),
        name="diff_attn",
    )(rel_bias, lamv, gsub_col, qt, k, vt, bias)


def _mla_attn_kernel(qt_ref, k_ref, vt_ref, o_ref):
    nkt = vt_ref.shape[2]
    tk = vt_ref.shape[-1]
    qt = qt_ref[0, 0]
    m, acc = _init_state(V_MLA, qt.shape[-1])
    scores = lambda kj: jnp.dot(k_ref[0, 0, kj * tk:(kj + 1) * tk, :], qt, preferred_element_type=F32)
    depth = 2
    pending = [scores(kj) for kj in range(min(depth, nkt))]
    for kj in range(nkt):
        s = pending.pop(0)
        if kj + depth < nkt:
            pending.append(scores(kj + depth))
        m, alpha, p = _softmax_weights(s, 0.0, m)
        acc = _accumulate(alpha, acc, _with_ones(vt_ref[0, 0, kj]), p)
    o_ref[0] = _normalised((m, acc), V_MLA).T.astype(BF16)


def _mla_attn_bounded_kernel(qt_ref, k_ref, vt_ref, o_ref):
    nkt = vt_ref.shape[2]
    tk = vt_ref.shape[-1]
    t = min(qt_ref.shape[-1], MLA_TQ)
    nq = qt_ref.shape[-1] // t
    ck = min(tk, KEY_CHUNK)
    npt = tk // ck
    n_chunks = nkt * npt
    items = [(j, i) for j in range(nq) for i in range(n_chunks)]

    def scores(item):
        j, i = item
        return jnp.dot(k_ref[0, 0, i * ck:(i + 1) * ck, :], qt_ref[0, 0, :, j * t:(j + 1) * t],
                       preferred_element_type=F32)

    depth = min(BOUNDED_DEPTH, len(items))
    pending = [scores(it) for it in items[:depth]]
    for n, (j, i) in enumerate(items):
        if i == 0:
            acc = jnp.zeros((V_MLA, t), F32)
            den = jnp.zeros((SUBLANES, t), F32)
        s = pending.pop(0)
        if n + depth < len(items):
            pending.append(scores(items[n + depth]))
        c = i % npt
        p = jnp.exp2(s)
        den = den + jnp.sum(p.reshape(ck // SUBLANES, SUBLANES, t), axis=0)
        acc = acc + jnp.dot(vt_ref[0, 0, i // npt, :, c * ck:(c + 1) * ck], p.astype(BF16),
                            preferred_element_type=F32)
        if i == n_chunks - 1:
            o_ref[0, j * t:(j + 1) * t, :] = (acc / jnp.sum(den, axis=0, keepdims=True)).T.astype(BF16)


def _mla_attn(qt, k, vt, *, bounded):
    b, n_heads, _, s = qt.shape
    t = _pick_tile(s, MLA_TQ * (BOUNDED_Q_TILES if bounded else 1))
    nkt, tk = vt.shape[2], vt.shape[4]
    return pl.pallas_call(
        _mla_attn_bounded_kernel if bounded else _mla_attn_kernel,
        grid=(b, n_heads, s // t),
        in_specs=[pl.BlockSpec((1, 1, QK_HEAD, t), lambda bi, hh, qi: (bi, hh, 0, qi)),
                  pl.BlockSpec((1, 1, s, QK_HEAD), lambda bi, hh, qi: (bi, hh, 0, 0)),
                  pl.BlockSpec((1, 1, nkt, V_MLA, tk), lambda bi, hh, qi: (bi, hh, 0, 0, 0))],
        out_specs=pl.BlockSpec((1, t, V_MLA), lambda bi, hh, qi: (bi, qi, hh)),
        out_shape=jax.ShapeDtypeStruct((b, s, n_heads * V_MLA), BF16),
        compiler_params=_cparams(("parallel", "parallel", "parallel")),
        name="mla_attn",
    )(qt, k, vt)


def _out_proj_kernel(x_ref, gt_ref, a_ref, b_ref, wa_ref, wb_ref, sc_ref, sh_ref, g2_ref, o_ref, h_ref):
    tm = x_ref.shape[1]
    rows = tm // OUT_PROJ_SPLIT
    mixes = []
    for i in range(OUT_PROJ_SPLIT):
        sl = slice(i * rows, (i + 1) * rows)
        mixes.append(jnp.dot(a_ref[0, sl, :], wa_ref[...], preferred_element_type=F32)
                     + jnp.dot(b_ref[0, sl, :], wb_ref[...], preferred_element_type=F32))
    for i in range(OUT_PROJ_SPLIT):
        sl = slice(i * rows, (i + 1) * rows)
        x1 = x_ref[0, sl, :] + gt_ref[0] * mixes[i]
        o_ref[0, sl, :] = x1
        h_ref[0, sl, :] = _modulated_norm(x1, g2_ref[...], sc_ref[0], sh_ref[0]).astype(BF16)


def _out_proj(x, gt1, a, bb, wa, wb, sc2, sh2, g2, *, tm):
    b, s, d = x.shape
    row = lambda w: pl.BlockSpec((1, tm, w), lambda bi, i: (bi, i, 0))
    mod = pl.BlockSpec((1, 1, d), lambda bi, i: (bi, 0, 0))
    return pl.pallas_call(
        _out_proj_kernel,
        grid=(b, s // tm),
        in_specs=[row(d), mod, row(a.shape[-1]), row(bb.shape[-1]), _const_spec(wa.shape),
                  _const_spec(wb.shape), mod, mod, _const_spec(g2.shape)],
        out_specs=[row(d), row(d)],
        out_shape=[jax.ShapeDtypeStruct((b, s, d), F32), jax.ShapeDtypeStruct((b, s, d), BF16)],
        compiler_params=_cparams(("parallel", "parallel")),
        name="out_proj",
    )(x, gt1, a, bb, wa, wb, sc2, sh2, g2)


def _ffn_kernel(x_ref, h_ref, gt_ref, wg_ref, wu_ref, wd_ref, o_ref, xs_ref, *, n_x_chunks):
    j = pl.program_id(2)
    rows = x_ref.shape[1]

    @pl.when(j == 0)
    def _():
        o_ref[...] = jnp.zeros(o_ref.shape, F32)

    chunk = jnp.minimum(j, n_x_chunks - 1)
    xs_ref[pl.ds(pl.multiple_of(chunk * rows, rows), rows), :] = x_ref[0]

    h = h_ref[0]
    g = jnp.dot(h, wg_ref[...], preferred_element_type=F32)
    u = jnp.dot(h, wu_ref[...], preferred_element_type=F32)
    act = (g * jax.nn.sigmoid(g) * u).astype(BF16)
    o_ref[0] += jnp.dot(act, wd_ref[...], preferred_element_type=F32)

    @pl.when(j == pl.num_programs(2) - 1)
    def _():
        o_ref[0] = xs_ref[...] + gt_ref[0] * o_ref[0]


def _ffn(x, h2, gt2, wg, wu, wd, *, tm, tf):
    b, s, d = x.shape
    dff = wg.shape[1]
    nj = dff // tf
    nxc = 1
    while nxc * 2 <= nj and (tm // (nxc * 2)) % SUBLANES == 0:
        nxc *= 2
    row = pl.BlockSpec((1, tm, d), lambda bi, i, j: (bi, i, 0))
    x_chunk = pl.BlockSpec((1, tm // nxc, d), lambda bi, i, j: (bi, i * nxc + jnp.minimum(j, nxc - 1), 0))
    return pl.pallas_call(
        functools.partial(_ffn_kernel, n_x_chunks=nxc),
        grid=(b, s // tm, nj),
        in_specs=[x_chunk, row, pl.BlockSpec((1, 1, d), lambda bi, i, j: (bi, 0, 0)),
                  pl.BlockSpec((d, tf), lambda bi, i, j: (0, j)),
                  pl.BlockSpec((d, tf), lambda bi, i, j: (0, j)),
                  pl.BlockSpec((tf, d), lambda bi, i, j: (j, 0))],
        out_specs=row,
        out_shape=jax.ShapeDtypeStruct((b, s, d), F32),
        scratch_shapes=[pltpu.VMEM((tm, d), F32)],
        compiler_params=_cparams(("parallel", "parallel", "arbitrary")),
        name="ffn",
    )(x, h2, gt2, wg, wu, wd)


def _rope_tables(s):
    pos = jnp.arange(s, dtype=jnp.float32)
    inv = 1.0 / (ROPE_THETA ** (jnp.arange(0, QK_ROPE, 2, dtype=jnp.float32) / QK_ROPE))
    ang = pos[:, None] * inv[None, :]
    return jnp.cos(ang), jnp.sin(ang)


def _pick_tile(n, pref):
    t = min(pref, n)
    while n % t:
        t //= 2
    return t


def kernel(x, c, rel_bias, w_ada, b_ada, g_norm1, w_in, g_q_diff, g_k_diff, lambda_vecs, g_subln, g_q_a, w_q_b, g_kv_a, w_kv_b, g_q_mla, g_k_mla, w_out, g_norm2, w_gate, w_up, w_down):
    b, s, d = x.shape
    depth = w_ada.shape[0]
    diff_width = d // 2
    n_hd = diff_width // DV_DIFF
    n_hm = (d - diff_width) // V_MLA
    assert n_hd == n_hm and n_hd % 2 == 0
    n_heads = n_hd
    wd = n_heads * LANES
    half = QK_ROPE // 2
    tk = _pick_tile(s, ATT_TK)
    nkt = s // tk
    t = tk
    assert t + 1 >= REL_MAX_DIST
    tm = _pick_tile(s, 512)

    cos, sin = _rope_tables(s)
    cosq, sinq = jnp.tile(cos, (1, n_heads)), jnp.tile(sin, (1, n_heads))
    cosk = jnp.tile(cos, (1, 4))
    sink = jnp.tile(jnp.concatenate([-sin, sin], axis=1), (1, 2))

    c_pad = jnp.pad(c, ((0, (-b) % 8), (0, 0)))

    for l in range(depth):
        lambda_init = 0.8 - 0.6 * math.exp(-0.3 * l)
        mod = _ada(c_pad, w_ada[l], b_ada[l][None, :])[:b]
        sh1, sc1, gt1, sh2, sc2, gt2 = [m[:, None, :] for m in jnp.split(mod, 6, axis=-1)]

        wi = w_in[l]
        o = 3 * wd + Q_LORA + KV_LORA
        w_kpe = wi[:, o:o + QK_ROPE]
        w_kpe_sw = jnp.concatenate([w_kpe[:, half:], w_kpe[:, :half]], axis=1)
        wqkv = wi.astype(BF16)
        wlat = jnp.concatenate([w_kpe, w_kpe, w_kpe_sw, w_kpe_sw], axis=1).astype(BF16)
        wq = w_q_b[l].reshape(Q_LORA, n_heads, QK_HEAD)
        wqb = jnp.concatenate([wq[:, :, :QK_NOPE].reshape(Q_LORA, -1),
                               wq[:, :, QK_NOPE:QK_NOPE + half].reshape(Q_LORA, -1),
                               wq[:, :, QK_NOPE + half:].reshape(Q_LORA, -1)], axis=1).astype(BF16)
        wkv = w_kv_b[l].reshape(KV_LORA, n_heads, QK_NOPE + V_MLA)
        wkvb = jnp.concatenate([wkv[:, :, :QK_NOPE].reshape(KV_LORA, -1),
                                wkv[:, :, QK_NOPE:].reshape(KV_LORA, -1)], axis=1).astype(BF16)
        gq, gk = g_q_mla[l], g_k_mla[l]
        gk_pe = gk[QK_NOPE:]
        gk_pe_sw = jnp.concatenate([gk_pe[half:], gk_pe[:half]])

        qt_d, kd, vt_d, qt_m, k_m, vt_m = _in_proj(
            x, sc1, sh1, g_norm1[l][None, :], wqkv, wlat,
            jnp.tile(g_q_diff[l], 2)[None, :], jnp.tile(g_k_diff[l], 2)[None, :],
            g_q_a[l][None, :], wqb, g_kv_a[l][None, :], wkvb,
            gq[None, :QK_NOPE], jnp.tile(gq[QK_NOPE:QK_NOPE + half], n_heads)[None, :],
            jnp.tile(gq[QK_NOPE + half:], n_heads)[None, :],
            gk[None, :QK_NOPE], jnp.tile(gk_pe, 2)[None, :], jnp.tile(gk_pe_sw, 2)[None, :],
            cosq, sinq, cosk, sink, n_heads=n_heads, tm=tk)

        bias = _bias_tiles(rel_bias, t)
        gmax = lambda g: jnp.max(jnp.abs(g))
        bound_d = (1.02 * LOG2E * DH_DIFF ** 0.5 * gmax(g_q_diff[l]) * gmax(g_k_diff[l])
                   + LOG2E * gmax(rel_bias))
        bound_m = 1.02 * LOG2E * QK_HEAD ** 0.5 * gmax(g_q_mla[l]) * gmax(g_k_mla[l])
        diff_args = (rel_bias, lambda_vecs[l], g_subln[l][:, None], qt_d, kd, vt_d, bias)
        a_out = lax.cond(
            bound_d <= SCORE_BOUND,
            lambda *a: _diff_attn(*a, lambda_init=lambda_init, bounded=True),
            lambda *a: _diff_attn(*a, lambda_init=lambda_init, bounded=False), *diff_args)
        b_out = lax.cond(
            bound_m <= SCORE_BOUND,
            lambda *a: _mla_attn(*a, bounded=True),
            lambda *a: _mla_attn(*a, bounded=False), qt_m, k_m, vt_m)

        wo = w_out[l].astype(BF16)
        x, h2 = _out_proj(x, gt1, a_out, b_out, wo[:diff_width], wo[diff_width:], sc2, sh2,
                          g_norm2[l][None, :], tm=tm)
        x = _ffn(x, h2, gt2, w_gate[l].astype(BF16), w_up[l].astype(BF16), w_down[l].astype(BF16),
                 tm=_pick_tile(s, FFN_TM), tf=_pick_tile(w_gate.shape[-1], FFN_TF))
    return x
```

```python
import functools
import math

import jax
import jax.numpy as jnp
from jax import lax
from jax.experimental import pallas as pl
from jax.experimental.pallas import tpu as pltpu

F32 = jnp.float32
BF16 = jnp.bfloat16

DH_DIFF = 64
DV_DIFF = 2 * DH_DIFF
QK_NOPE = 128
QK_ROPE = 64
QK_HEAD = QK_NOPE + QK_ROPE
V_MLA = 128
Q_LORA = 512
KV_LORA = 256
ROPE_THETA = 10000.0
REL_BUCKETS = 32
REL_MAX_DIST = 128
EPS = 1e-6
LOG2E = 1.4426950408889634

LANES = 128
SUBLANES = 8
VMEM_LIMIT_BYTES = 56 * 1024 * 1024

OUT_PROJ_SPLIT = 4
IN_PROJ_SPLIT = 2
FFN_TM = 1024
FFN_TF = 512
ATT_TK = 512
DIFF_TQ = 512
MLA_TQ = 512
KEY_CHUNK = 256
BOUNDED_DEPTH = 2
DIFF_BOUNDED_Q_TILES = 4
MLA_BOUNDED_Q_TILES = 8
SCORE_BOUND = 40.0


def _cparams(sem):
    return pltpu.CompilerParams(dimension_semantics=sem, vmem_limit_bytes=VMEM_LIMIT_BYTES)


def _const_spec(shape):
    nd = len(shape)
    return pl.BlockSpec(shape, lambda *_: (0,) * nd, pipeline_mode=pl.Buffered(1))


def _ada_kernel(c_ref, w_ref, b_ref, o_ref):
    c = c_ref[...]
    ca = c * jax.nn.sigmoid(c)
    o_ref[...] = jnp.dot(ca.astype(BF16), w_ref[...].astype(BF16), preferred_element_type=F32) + b_ref[...]


def _ada(c_pad, w, b, tn=1024):
    m, d = c_pad.shape
    n = w.shape[1]
    return pl.pallas_call(
        _ada_kernel,
        grid=(n // tn,),
        in_specs=[pl.BlockSpec((m, d), lambda j: (0, 0)),
                  pl.BlockSpec((d, tn), lambda j: (0, j)),
                  pl.BlockSpec((1, tn), lambda j: (0, j))],
        out_specs=pl.BlockSpec((m, tn), lambda j: (0, j)),
        out_shape=jax.ShapeDtypeStruct((m, n), F32),
        compiler_params=_cparams(("arbitrary",)),
        name="ada",
    )(c_pad, w, b)


def _modulated_norm(x, g, sc, sh):
    ms = jnp.mean(x * x, axis=-1, keepdims=True)
    return (x * lax.rsqrt(ms + EPS) * g) * (1.0 + sc) + sh


def _half_lane_norm(blk, g2, lo_mask, out_scale):
    sq = blk * blk
    s_lo = jnp.sum(jnp.where(lo_mask, sq, 0.0), axis=-1, keepdims=True)
    s_hi = jnp.sum(jnp.where(lo_mask, 0.0, sq), axis=-1, keepdims=True)
    inv = jnp.where(lo_mask, lax.rsqrt(s_lo * (1.0 / DH_DIFF) + EPS),
                    lax.rsqrt(s_hi * (1.0 / DH_DIFF) + EPS))
    return blk * inv * (g2 * out_scale)


def _in_proj_kernel(x_ref, sc_ref, sh_ref, g1_ref, wqkv_ref, wlat_ref, gqd_ref, gkd_ref,
                    gqa_ref, wqb_ref, gkva_ref, wkvb_ref, gqn_ref, gq1_ref, gq2_ref,
                    gkn_ref, gkp_ref, gkps_ref, cosq_ref, sinq_ref, cosk_ref, sink_ref,
                    qtd_ref, kd_ref, vtd_ref, qtm_ref, km_ref, vtm_ref, *, n_heads):
    tm = x_ref.shape[1]
    rows = tm // IN_PROJ_SPLIT
    wd = n_heads * LANES
    half = QK_ROPE // 2
    wx = n_heads * half
    lo_mask = lax.broadcasted_iota(jnp.int32, (rows, LANES), 1) < DH_DIFF
    lane_x = lax.broadcasted_iota(jnp.int32, (rows, wx), 1) // half
    qscale = DH_DIFF ** -0.5 * LOG2E
    mscale = QK_HEAD ** -0.5 * LOG2E

    def rms(v, g):
        return v * lax.rsqrt(jnp.mean(v * v, axis=-1, keepdims=True) + EPS) * g

    for sb in range(IN_PROJ_SPLIT):
        rs = slice(sb * rows, (sb + 1) * rows)
        h = _modulated_norm(x_ref[0, rs, :], g1_ref[...], sc_ref[0], sh_ref[0]).astype(BF16)
        lat = jnp.dot(h, wqkv_ref[:, 3 * wd:3 * wd + Q_LORA + KV_LORA], preferred_element_type=F32)
        cq = lat[:, 0:Q_LORA]
        ckv = lat[:, Q_LORA:Q_LORA + KV_LORA]
        kpe4 = jnp.dot(h, wlat_ref[...], preferred_element_type=F32)
        kpe2 = kpe4[:, 0:LANES]
        kpes2 = kpe4[:, LANES:2 * LANES]

        kv = jnp.dot(rms(ckv, gkva_ref[...]).astype(BF16), wkvb_ref[...], preferred_element_type=F32)
        for hh in range(n_heads):
            vtm_ref[0, hh, 0, :, rs] = kv[:, wd + hh * LANES:wd + (hh + 1) * LANES].T.astype(BF16)
        ss_pe = jnp.sum(jnp.where(lo_mask, kpe2 * kpe2, 0.0), axis=-1, keepdims=True)
        kr2 = kpe2 * gkp_ref[...] * cosk_ref[rs, :] + kpes2 * gkps_ref[...] * sink_ref[rs, :]
        for hh in range(n_heads):
            sl = slice(hh * LANES, (hh + 1) * LANES)
            kn = kv[:, sl]
            ss = jnp.sum(kn * kn, axis=-1, keepdims=True) + ss_pe
            r = lax.rsqrt(ss * (1.0 / QK_HEAD) + EPS)
            km_ref[0, hh, rs, 0:QK_NOPE] = (kn * r * gkn_ref[...]).astype(BF16)
            km_ref[0, hh, rs, QK_NOPE:QK_HEAD] = (kr2 * r)[:, 0:QK_ROPE].astype(BF16)

        qm = jnp.dot(rms(cq, gqa_ref[...]).astype(BF16), wqb_ref[...], preferred_element_type=F32)
        x1 = qm[:, wd:wd + wx]
        x2 = qm[:, wd + wx:wd + 2 * wx]
        sq_x = x1 * x1 + x2 * x2
        inv_x = jnp.zeros((rows, wx), F32)
        for hh in range(n_heads):
            sl = slice(hh * LANES, (hh + 1) * LANES)
            qn = qm[:, sl]
            ss = (jnp.sum(qn * qn, axis=-1, keepdims=True)
                  + jnp.sum(jnp.where(lane_x == hh, sq_x, 0.0), axis=-1, keepdims=True))
            r = lax.rsqrt(ss * (1.0 / QK_HEAD) + EPS) * mscale
            qtm_ref[0, hh, 0:QK_NOPE, rs] = (qn * r * gqn_ref[...]).T.astype(BF16)
            inv_x = jnp.where(lane_x == hh, r, inv_x)
        a1 = x1 * gq1_ref[...]
        a2 = x2 * gq2_ref[...]
        cq_t = cosq_ref[rs, :]
        sq_t = sinq_ref[rs, :]
        x1t = ((a1 * cq_t - a2 * sq_t) * inv_x).T.astype(BF16)
        x2t = ((a2 * cq_t + a1 * sq_t) * inv_x).T.astype(BF16)
        for hh in range(n_heads):
            qtm_ref[0, hh, QK_NOPE:QK_NOPE + half, rs] = x1t[hh * half:(hh + 1) * half]
            qtm_ref[0, hh, QK_NOPE + half:QK_HEAD, rs] = x2t[hh * half:(hh + 1) * half]

        vd = jnp.dot(h, wqkv_ref[:, 2 * wd:3 * wd], preferred_element_type=F32)
        for hh in range(n_heads):
            vtd_ref[0, hh, 0, :, rs] = vd[:, hh * LANES:(hh + 1) * LANES].T.astype(BF16)
        qd = jnp.dot(h, wqkv_ref[:, 0:wd], preferred_element_type=F32)
        for hh in range(n_heads):
            sl = slice(hh * LANES, (hh + 1) * LANES)
            qtd_ref[0, hh, :, rs] = _half_lane_norm(qd[:, sl], gqd_ref[...], lo_mask, qscale).T.astype(BF16)
        kd = jnp.dot(h, wqkv_ref[:, wd:2 * wd], preferred_element_type=F32)
        for hh in range(n_heads):
            sl = slice(hh * LANES, (hh + 1) * LANES)
            kd_ref[0, rs, sl] = _half_lane_norm(kd[:, sl], gkd_ref[...], lo_mask, 1.0).astype(BF16)


def _in_proj(x, sc1, sh1, g1, wqkv, wlat, gqd, gkd, gqa, wqb, gkva, wkvb, gqn, gq1, gq2,
             gkn, gkp, gkps, cosq, sinq, cosk, sink, *, n_heads, tm):
    b, s, d = x.shape
    wd = n_heads * LANES
    wx = n_heads * (QK_ROPE // 2)
    row = lambda w: pl.BlockSpec((1, tm, w), lambda bi, i: (bi, i, 0))
    mod = pl.BlockSpec((1, 1, d), lambda bi, i: (bi, 0, 0))
    tab = lambda w: pl.BlockSpec((tm, w), lambda bi, i: (i, 0))
    consts = [g1, wqkv, wlat, gqd, gkd, gqa, wqb, gkva, wkvb, gqn, gq1, gq2, gkn, gkp, gkps]
    return pl.pallas_call(
        functools.partial(_in_proj_kernel, n_heads=n_heads),
        grid=(b, s // tm),
        in_specs=[row(d), mod, mod] + [_const_spec(a.shape) for a in consts]
                 + [tab(wx), tab(wx), tab(LANES), tab(LANES)],
        out_specs=[pl.BlockSpec((1, n_heads, LANES, tm), lambda bi, i: (bi, 0, 0, i)),
                   row(wd),
                   pl.BlockSpec((1, n_heads, 1, DV_DIFF, tm), lambda bi, i: (bi, 0, i, 0, 0)),
                   pl.BlockSpec((1, n_heads, QK_HEAD, tm), lambda bi, i: (bi, 0, 0, i)),
                   pl.BlockSpec((1, n_heads, tm, QK_HEAD), lambda bi, i: (bi, 0, i, 0)),
                   pl.BlockSpec((1, n_heads, 1, V_MLA, tm), lambda bi, i: (bi, 0, i, 0, 0))],
        out_shape=[jax.ShapeDtypeStruct((b, n_heads, LANES, s), BF16),
                   jax.ShapeDtypeStruct((b, s, wd), BF16),
                   jax.ShapeDtypeStruct((b, n_heads, s // tm, DV_DIFF, tm), BF16),
                   jax.ShapeDtypeStruct((b, n_heads, QK_HEAD, s), BF16),
                   jax.ShapeDtypeStruct((b, n_heads, s, QK_HEAD), BF16),
                   jax.ShapeDtypeStruct((b, n_heads, s // tm, V_MLA, tm), BF16)],
        compiler_params=_cparams(("parallel", "parallel")),
        name="in_proj",
    )(x, sc1, sh1, *consts, cosq, sinq, cosk, sink)


def _t5_bucket(rel):
    nb = REL_BUCKETS // 2
    max_exact = nb // 2
    base = jnp.where(rel > 0, nb, 0)
    n = jnp.abs(rel)
    nf = jnp.maximum(n, 1).astype(jnp.float32)
    large = max_exact + (jnp.log(nf / max_exact) / math.log(REL_MAX_DIST / max_exact)
                         * (nb - max_exact)).astype(jnp.int32)
    large = jnp.minimum(large, nb - 1)
    return base + jnp.where(n < max_exact, n, large)


def _bias_kernel(rb_ref, bucket_ref, o_ref):
    hh = pl.program_id(0)
    t = o_ref.shape[-1]
    far_left = rb_ref[REL_BUCKETS // 2 - 1, hh]
    far_right = rb_ref[REL_BUCKETS - 1, hh]
    o_ref[0, 0] = jnp.full((t, t), far_left, F32) * LOG2E
    o_ref[0, 4] = jnp.full((t, t), far_right, F32) * LOG2E
    bk_strip = bucket_ref[...]
    strip = jnp.zeros(bk_strip.shape, F32)
    for bk in range(REL_BUCKETS):
        strip = jnp.where(bk_strip == bk, rb_ref[bk, hh], strip)
    strip = strip * LOG2E
    for d in (-1, 0, 1):
        diag = jnp.concatenate([strip[:, (2 - d) * t:(3 - d) * t], strip[:, (1 - d) * t:(2 - d) * t]], axis=1)
        rows = jnp.broadcast_to(diag, (t, 2 * t))
        o_ref[0, d + 2] = pltpu.roll(rows, 0, 1, stride=1, stride_axis=0)[:, :t]


def _bias_tiles(rel_bias, t):
    n_heads = rel_bias.shape[1]
    buckets = _t5_bucket(2 * t - jnp.arange(4 * t, dtype=jnp.int32))[None, :]
    return pl.pallas_call(
        _bias_kernel,
        grid=(n_heads,),
        in_specs=[pl.BlockSpec(memory_space=pltpu.SMEM),
                  pl.BlockSpec((1, 4 * t), lambda hh: (0, 0))],
        out_specs=pl.BlockSpec((1, 5, t, t), lambda hh: (hh, 0, 0, 0)),
        out_shape=jax.ShapeDtypeStruct((n_heads, 5, t, t), F32),
        compiler_params=_cparams(("arbitrary",)),
        name="rel_bias_tiles",
    )(rel_bias, buckets)


ONES_ROWS = 16


def _softmax_state_step(s, offset, vt, state):
    m_new, alpha, p = _softmax_weights(s, offset, state[0])
    return m_new, _accumulate(alpha, state[1], vt, p)


def _softmax_weights(s, offset, m_old):
    m_new = jnp.maximum(m_old, jnp.max(s, axis=0, keepdims=True) + offset)
    alpha = jnp.exp2(m_old - m_new)
    p = jnp.exp2(s - (m_new - offset)).astype(BF16)
    return m_new, alpha, p


def _accumulate(alpha, acc_old, vt, p):
    return alpha * acc_old + jnp.dot(vt, p, preferred_element_type=F32)


def _init_state(dv, t):
    return (jnp.full((1, t), -jnp.inf, F32), jnp.zeros((dv + ONES_ROWS, t), F32))


def _with_ones(vt):
    return jnp.concatenate([vt, jnp.ones((ONES_ROWS, vt.shape[1]), vt.dtype)], axis=0)


def _normalised(state, dv):
    _, acc = state
    return acc[:dv] / acc[dv:dv + 1]


def _diff_attn_kernel(rb_ref, lamv_ref, gsub_ref, qt_ref, k_ref, vt_ref, bias_ref, o_ref, *,
                      lambda_init, bounded):
    hh = pl.program_id(0)
    t = bias_ref.shape[-1]
    nq = qt_ref.shape[-1] // t
    nkt = vt_ref.shape[2]

    off_left = rb_ref[REL_BUCKETS // 2 - 1, hh] * LOG2E
    off_right = rb_ref[REL_BUCKETS - 1, hh] * LOG2E
    lv = lamv_ref[...]
    lam = (jnp.exp(jnp.sum(lv[0:1] * lv[1:2], axis=-1, keepdims=True))
           - jnp.exp(jnp.sum(lv[2:3] * lv[3:4], axis=-1, keepdims=True)) + lambda_init)

    n_near = min(nkt, 3)
    row = lax.broadcasted_iota(jnp.int32, (LANES, t), 0)
    zero = jnp.zeros((LANES, t), BF16)

    def query_maps(j):
        qt = qt_ref[0, 0, :, j * t:(j + 1) * t]
        return jnp.where(row < DH_DIFF, qt, zero), jnp.where(row < DH_DIFF, zero, qt)

    def tile_index(j, r):
        return lax.rem(pl.program_id(2) * nq + j - 1 + r + nkt, nkt)

    def far_offset(j, r):
        return jnp.where(tile_index(j, r) < pl.program_id(2) * nq + j, off_left, off_right)

    def biased_scores(j, qmap, r, k0, ck):
        kj = tile_index(j, r)
        k_t = k_ref[0, pl.ds(pl.multiple_of(kj * t + k0, ck), ck), :]
        s = jnp.dot(k_t, qmap, preferred_element_type=F32)
        if r < n_near:
            s = s + bias_ref[0, jnp.clip(kj - (pl.program_id(2) * nq + j), -2, 2) + 2, k0:k0 + ck, :]
        elif bounded:
            s = s + far_offset(j, r)
        return s

    def finish(j, outs):
        o = outs[0] - lam * outs[1]
        ms = jnp.mean(o * o, axis=0, keepdims=True)
        y = o * lax.rsqrt(ms + EPS) * (gsub_ref[...] * (1.0 - lambda_init))
        o_ref[0, j * t:(j + 1) * t, :] = y.T.astype(BF16)

    if bounded:
        ck = min(t, KEY_CHUNK)
        npt = t // ck
        qmaps = [query_maps(j) for j in range(nq)]
        items = [(j, r, c, mp) for j in range(nq) for r in range(nkt) for c in range(npt) for mp in range(2)]
        score = lambda it: biased_scores(it[0], qmaps[it[0]][it[3]], it[1], it[2] * ck, ck)
        depth = min(BOUNDED_DEPTH, len(items))
        pending = [score(it) for it in items[:depth]]
        for n, (j, r, c, mp) in enumerate(items):
            if (r, c, mp) == (0, 0, 0):
                accs = [jnp.zeros((DV_DIFF, t), F32) for _ in range(2)]
                dens = [jnp.zeros((SUBLANES, t), F32) for _ in range(2)]
            s = pending.pop(0)
            if n + depth < len(items):
                pending.append(score(items[n + depth]))
            p = jnp.exp2(s)
            dens[mp] = dens[mp] + jnp.sum(p.reshape(ck // SUBLANES, SUBLANES, t), axis=0)
            accs[mp] = accs[mp] + jnp.dot(vt_ref[0, 0, tile_index(j, r), :, c * ck:(c + 1) * ck],
                                          p.astype(BF16), preferred_element_type=F32)
            if (r, c, mp) == (nkt - 1, npt - 1, 1):
                finish(j, [acc / jnp.sum(den, axis=0, keepdims=True) for acc, den in zip(accs, dens)])
    else:
        for j in range(nq):
            qmap = query_maps(j)
            states = [_init_state(DV_DIFF, t), _init_state(DV_DIFF, t)]
            score = lambda r: [biased_scores(j, qmap[mp], r, 0, t) for mp in range(2)]
            s_next = score(0)
            for r in range(nkt):
                s_cur = s_next
                if r + 1 < nkt:
                    s_next = score(r + 1)
                vt = _with_ones(vt_ref[0, 0, tile_index(j, r)])
                offset = 0.0 if r < n_near else far_offset(j, r)
                for mp in range(2):
                    states[mp] = _softmax_state_step(s_cur[mp], offset, vt, states[mp])
            finish(j, [_normalised(st, DV_DIFF) for st in states])


def _diff_attn(rel_bias, lamv, gsub_col, qt, k, vt, bias, *, lambda_init, bounded):
    b, n_heads, _, s = qt.shape
    t = bias.shape[-1]
    nkt, tk = vt.shape[2], vt.shape[4]
    assert tk == t
    tq = _pick_tile(s, t * (DIFF_BOUNDED_Q_TILES if bounded else 1))
    return pl.pallas_call(
        functools.partial(_diff_attn_kernel, lambda_init=lambda_init, bounded=bounded),
        grid=(n_heads, b, s // tq),
        in_specs=[pl.BlockSpec(memory_space=pltpu.SMEM),
                  pl.BlockSpec(lamv.shape, lambda hh, bi, qi: (0, 0)),
                  pl.BlockSpec(gsub_col.shape, lambda hh, bi, qi: (0, 0)),
                  pl.BlockSpec((1, 1, LANES, tq), lambda hh, bi, qi: (bi, hh, 0, qi)),
                  pl.BlockSpec((1, s, LANES), lambda hh, bi, qi: (bi, 0, hh)),
                  pl.BlockSpec((1, 1, nkt, DV_DIFF, tk), lambda hh, bi, qi: (bi, hh, 0, 0, 0)),
                  pl.BlockSpec((1, 5, t, t), lambda hh, bi, qi: (hh, 0, 0, 0))],
        out_specs=pl.BlockSpec((1, tq, DV_DIFF), lambda hh, bi, qi: (bi, qi, hh)),
        out_shape=jax.ShapeDtypeStruct((b, s, n_heads * DV_DIFF), BF16),
        compiler_params=_cparams(("parallel", "parallel", "parallel")),
        name="diff_attn",
    )(rel_bias, lamv, gsub_col, qt, k, vt, bias)


def _mla_attn_kernel(qt_ref, k_ref, vt_ref, o_ref):
    nkt = vt_ref.shape[2]
    tk = vt_ref.shape[-1]
    qt = qt_ref[0, 0]
    m, acc = _init_state(V_MLA, qt.shape[-1])
    scores = lambda kj: jnp.dot(k_ref[0, 0, kj * tk:(kj + 1) * tk, :], qt, preferred_element_type=F32)
    depth = 2
    pending = [scores(kj) for kj in range(min(depth, nkt))]
    for kj in range(nkt):
        s = pending.pop(0)
        if kj + depth < nkt:
            pending.append(scores(kj + depth))
        m, alpha, p = _softmax_weights(s, 0.0, m)
        acc = _accumulate(alpha, acc, _with_ones(vt_ref[0, 0, kj]), p)
    o_ref[0] = _normalised((m, acc), V_MLA).T.astype(BF16)


def _mla_attn_bounded_kernel(qt_ref, k_ref, vt_ref, o_ref):
    nkt = vt_ref.shape[2]
    tk = vt_ref.shape[-1]
    t = min(qt_ref.shape[-1], MLA_TQ)
    nq = qt_ref.shape[-1] // t
    ck = min(tk, KEY_CHUNK)
    npt = tk // ck
    n_chunks = nkt * npt
    items = [(j, i) for j in range(nq) for i in range(n_chunks)]

    def scores(item):
        j, i = item
        return jnp.dot(k_ref[0, 0, i * ck:(i + 1) * ck, :], qt_ref[0, 0, :, j * t:(j + 1) * t],
                       preferred_element_type=F32)

    depth = min(BOUNDED_DEPTH, len(items))
    pending = [scores(it) for it in items[:depth]]
    for n, (j, i) in enumerate(items):
        if i == 0:
            acc = jnp.zeros((V_MLA, t), F32)
            den = jnp.zeros((SUBLANES, t), F32)
        s = pending.pop(0)
        if n + depth < len(items):
            pending.append(scores(items[n + depth]))
        c = i % npt
        p = jnp.exp2(s)
        den = den + jnp.sum(p.reshape(ck // SUBLANES, SUBLANES, t), axis=0)
        acc = acc + jnp.dot(vt_ref[0, 0, i // npt, :, c * ck:(c + 1) * ck], p.astype(BF16),
                            preferred_element_type=F32)
        if i == n_chunks - 1:
            o_ref[0, j * t:(j + 1) * t, :] = (acc / jnp.sum(den, axis=0, keepdims=True)).T.astype(BF16)


def _mla_attn(qt, k, vt, *, bounded):
    b, n_heads, _, s = qt.shape
    t = _pick_tile(s, MLA_TQ * (MLA_BOUNDED_Q_TILES if bounded else 1))
    nkt, tk = vt.shape[2], vt.shape[4]
    return pl.pallas_call(
        _mla_attn_bounded_kernel if bounded else _mla_attn_kernel,
        grid=(b, n_heads, s // t),
        in_specs=[pl.BlockSpec((1, 1, QK_HEAD, t), lambda bi, hh, qi: (bi, hh, 0, qi)),
                  pl.BlockSpec((1, 1, s, QK_HEAD), lambda bi, hh, qi: (bi, hh, 0, 0)),
                  pl.BlockSpec((1, 1, nkt, V_MLA, tk), lambda bi, hh, qi: (bi, hh, 0, 0, 0))],
        out_specs=pl.BlockSpec((1, t, V_MLA), lambda bi, hh, qi: (bi, qi, hh)),
        out_shape=jax.ShapeDtypeStruct((b, s, n_heads * V_MLA), BF16),
        compiler_params=_cparams(("parallel", "parallel", "parallel")),
        name="mla_attn",
    )(qt, k, vt)


def _out_proj_kernel(x_ref, gt_ref, a_ref, b_ref, wa_ref, wb_ref, sc_ref, sh_ref, g2_ref, o_ref, h_ref):
    tm = x_ref.shape[1]
    rows = tm // OUT_PROJ_SPLIT
    mixes = []
    for i in range(OUT_PROJ_SPLIT):
        sl = slice(i * rows, (i + 1) * rows)
        mixes.append(jnp.dot(a_ref[0, sl, :], wa_ref[...], preferred_element_type=F32)
                     + jnp.dot(b_ref[0, sl, :], wb_ref[...], preferred_element_type=F32))
    for i in range(OUT_PROJ_SPLIT):
        sl = slice(i * rows, (i + 1) * rows)
        x1 = x_ref[0, sl, :] + gt_ref[0] * mixes[i]
        o_ref[0, sl, :] = x1
        h_ref[0, sl, :] = _modulated_norm(x1, g2_ref[...], sc_ref[0], sh_ref[0]).astype(BF16)


def _out_proj(x, gt1, a, bb, wa, wb, sc2, sh2, g2, *, tm):
    b, s, d = x.shape
    row = lambda w: pl.BlockSpec((1, tm, w), lambda bi, i: (bi, i, 0))
    mod = pl.BlockSpec((1, 1, d), lambda bi, i: (bi, 0, 0))
    return pl.pallas_call(
        _out_proj_kernel,
        grid=(b, s // tm),
        in_specs=[row(d), mod, row(a.shape[-1]), row(bb.shape[-1]), _const_spec(wa.shape),
                  _const_spec(wb.shape), mod, mod, _const_spec(g2.shape)],
        out_specs=[row(d), row(d)],
        out_shape=[jax.ShapeDtypeStruct((b, s, d), F32), jax.ShapeDtypeStruct((b, s, d), BF16)],
        compiler_params=_cparams(("parallel", "parallel")),
        name="out_proj",
    )(x, gt1, a, bb, wa, wb, sc2, sh2, g2)


def _ffn_kernel(x_ref, h_ref, gt_ref, wg_ref, wu_ref, wd_ref, o_ref, xs_ref, *, n_x_chunks):
    j = pl.program_id(2)
    rows = x_ref.shape[1]

    @pl.when(j == 0)
    def _():
        o_ref[...] = jnp.zeros(o_ref.shape, F32)

    chunk = jnp.minimum(j, n_x_chunks - 1)
    xs_ref[pl.ds(pl.multiple_of(chunk * rows, rows), rows), :] = x_ref[0]

    h = h_ref[0]
    g = jnp.dot(h, wg_ref[...], preferred_element_type=F32)
    u = jnp.dot(h, wu_ref[...], preferred_element_type=F32)
    act = (g * jax.nn.sigmoid(g) * u).astype(BF16)
    o_ref[0] += jnp.dot(act, wd_ref[...], preferred_element_type=F32)

    @pl.when(j == pl.num_programs(2) - 1)
    def _():
        o_ref[0] = xs_ref[...] + gt_ref[0] * o_ref[0]


def _ffn(x, h2, gt2, wg, wu, wd, *, tm, tf):
    b, s, d = x.shape
    dff = wg.shape[1]
    nj = dff // tf
    nxc = 1
    while nxc * 2 <= nj and (tm // (nxc * 2)) % SUBLANES == 0:
        nxc *= 2
    row = pl.BlockSpec((1, tm, d), lambda bi, i, j: (bi, i, 0))
    x_chunk = pl.BlockSpec((1, tm // nxc, d), lambda bi, i, j: (bi, i * nxc + jnp.minimum(j, nxc - 1), 0))
    return pl.pallas_call(
        functools.partial(_ffn_kernel, n_x_chunks=nxc),
        grid=(b, s // tm, nj),
        in_specs=[x_chunk, row, pl.BlockSpec((1, 1, d), lambda bi, i, j: (bi, 0, 0)),
                  pl.BlockSpec((d, tf), lambda bi, i, j: (0, j)),
                  pl.BlockSpec((d, tf), lambda bi, i, j: (0, j)),
                  pl.BlockSpec((tf, d), lambda bi, i, j: (j, 0))],
        out_specs=row,
        out_shape=jax.ShapeDtypeStruct((b, s, d), F32),
        scratch_shapes=[pltpu.VMEM((tm, d), F32)],
        compiler_params=_cparams(("parallel", "parallel", "arbitrary")),
        name="ffn",
    )(x, h2, gt2, wg, wu, wd)


def _rope_tables(s):
    pos = jnp.arange(s, dtype=jnp.float32)
    inv = 1.0 / (ROPE_THETA ** (jnp.arange(0, QK_ROPE, 2, dtype=jnp.float32) / QK_ROPE))
    ang = pos[:, None] * inv[None, :]
    return jnp.cos(ang), jnp.sin(ang)


def _pick_tile(n, pref):
    t = min(pref, n)
    while n % t:
        t //= 2
    return t


def kernel(x, c, rel_bias, w_ada, b_ada, g_norm1, w_in, g_q_diff, g_k_diff, lambda_vecs, g_subln, g_q_a, w_q_b, g_kv_a, w_kv_b, g_q_mla, g_k_mla, w_out, g_norm2, w_gate, w_up, w_down):
    b, s, d = x.shape
    depth = w_ada.shape[0]
    diff_width = d // 2
    n_hd = diff_width // DV_DIFF
    n_hm = (d - diff_width) // V_MLA
    assert n_hd == n_hm and n_hd % 2 == 0
    n_heads = n_hd
    wd = n_heads * LANES
    half = QK_ROPE // 2
    tk = _pick_tile(s, ATT_TK)
    nkt = s // tk
    t = tk
    assert t + 1 >= REL_MAX_DIST
    tm = _pick_tile(s, 512)

    cos, sin = _rope_tables(s)
    cosq, sinq = jnp.tile(cos, (1, n_heads)), jnp.tile(sin, (1, n_heads))
    cosk = jnp.tile(cos, (1, 4))
    sink = jnp.tile(jnp.concatenate([-sin, sin], axis=1), (1, 2))

    c_pad = jnp.pad(c, ((0, (-b) % 8), (0, 0)))

    for l in range(depth):
        lambda_init = 0.8 - 0.6 * math.exp(-0.3 * l)
        mod = _ada(c_pad, w_ada[l], b_ada[l][None, :])[:b]
        sh1, sc1, gt1, sh2, sc2, gt2 = [m[:, None, :] for m in jnp.split(mod, 6, axis=-1)]

        wi = w_in[l]
        o = 3 * wd + Q_LORA + KV_LORA
        w_kpe = wi[:, o:o + QK_ROPE]
        w_kpe_sw = jnp.concatenate([w_kpe[:, half:], w_kpe[:, :half]], axis=1)
        wqkv = wi.astype(BF16)
        wlat = jnp.concatenate([w_kpe, w_kpe, w_kpe_sw, w_kpe_sw], axis=1).astype(BF16)
        wq = w_q_b[l].reshape(Q_LORA, n_heads, QK_HEAD)
        wqb = jnp.concatenate([wq[:, :, :QK_NOPE].reshape(Q_LORA, -1),
                               wq[:, :, QK_NOPE:QK_NOPE + half].reshape(Q_LORA, -1),
                               wq[:, :, QK_NOPE + half:].reshape(Q_LORA, -1)], axis=1).astype(BF16)
        wkv = w_kv_b[l].reshape(KV_LORA, n_heads, QK_NOPE + V_MLA)
        wkvb = jnp.concatenate([wkv[:, :, :QK_NOPE].reshape(KV_LORA, -1),
                                wkv[:, :, QK_NOPE:].reshape(KV_LORA, -1)], axis=1).astype(BF16)
        gq, gk = g_q_mla[l], g_k_mla[l]
        gk_pe = gk[QK_NOPE:]
        gk_pe_sw = jnp.concatenate([gk_pe[half:], gk_pe[:half]])

        qt_d, kd, vt_d, qt_m, k_m, vt_m = _in_proj(
            x, sc1, sh1, g_norm1[l][None, :], wqkv, wlat,
            jnp.tile(g_q_diff[l], 2)[None, :], jnp.tile(g_k_diff[l], 2)[None, :],
            g_q_a[l][None, :], wqb, g_kv_a[l][None, :], wkvb,
            gq[None, :QK_NOPE], jnp.tile(gq[QK_NOPE:QK_NOPE + half], n_heads)[None, :],
            jnp.tile(gq[QK_NOPE + half:], n_heads)[None, :],
            gk[None, :QK_NOPE], jnp.tile(gk_pe, 2)[None, :], jnp.tile(gk_pe_sw, 2)[None, :],
            cosq, sinq, cosk, sink, n_heads=n_heads, tm=tk)

        bias = _bias_tiles(rel_bias, t)
        gmax = lambda g: jnp.max(jnp.abs(g))
        bound_d = (1.02 * LOG2E * DH_DIFF ** 0.5 * gmax(g_q_diff[l]) * gmax(g_k_diff[l])
                   + LOG2E * gmax(rel_bias))
        bound_m = 1.02 * LOG2E * QK_HEAD ** 0.5 * gmax(g_q_mla[l]) * gmax(g_k_mla[l])
        diff_args = (rel_bias, lambda_vecs[l], g_subln[l][:, None], qt_d, kd, vt_d, bias)
        a_out = lax.cond(
            bound_d <= SCORE_BOUND,
            lambda *a: _diff_attn(*a, lambda_init=lambda_init, bounded=True),
            lambda *a: _diff_attn(*a, lambda_init=lambda_init, bounded=False), *diff_args)
        b_out = lax.cond(
            bound_m <= SCORE_BOUND,
            lambda *a: _mla_attn(*a, bounded=True),
            lambda *a: _mla_attn(*a, bounded=False), qt_m, k_m, vt_m)

        wo = w_out[l].astype(BF16)
        x, h2 = _out_proj(x, gt1, a_out, b_out, wo[:diff_width], wo[diff_width:], sc2, sh2,
                          g_norm2[l][None, :], tm=tm)
        x = _ffn(x, h2, gt2, w_gate[l].astype(BF16), w_up[l].astype(BF16), w_down[l].astype(BF16),
                 tm=_pick_tile(s, FFN_TM), tf=_pick_tile(w_gate.shape[-1], FFN_TF))
    return x
```

```python
import functools
import math

import jax
import jax.numpy as jnp
from jax import lax
from jax.experimental import pallas as pl
from jax.experimental.pallas import tpu as pltpu

F32 = jnp.float32
BF16 = jnp.bfloat16

DH_DIFF = 64
DV_DIFF = 2 * DH_DIFF
QK_NOPE = 128
QK_ROPE = 64
QK_HEAD = QK_NOPE + QK_ROPE
V_MLA = 128
Q_LORA = 512
KV_LORA = 256
ROPE_THETA = 10000.0
REL_BUCKETS = 32
REL_MAX_DIST = 128
EPS = 1e-6
LOG2E = 1.4426950408889634

LANES = 128
SUBLANES = 8
VMEM_LIMIT_BYTES = 56 * 1024 * 1024

OUT_PROJ_SPLIT = 4
IN_PROJ_SPLIT = 2
FFN_TM = 1024
FFN_TF = 512
FFN_COL_SPLIT = 2
ATT_TK = 512
DIFF_TQ = 512
MLA_TQ = 512
KEY_CHUNK = 256
BOUNDED_DEPTH = 2
DIFF_BOUNDED_Q_TILES = 4
MLA_BOUNDED_Q_TILES = 8
SCORE_BOUND = 40.0


def _cparams(sem):
    return pltpu.CompilerParams(dimension_semantics=sem, vmem_limit_bytes=VMEM_LIMIT_BYTES)


def _const_spec(shape):
    nd = len(shape)
    return pl.BlockSpec(shape, lambda *_: (0,) * nd, pipeline_mode=pl.Buffered(1))


def _ada_kernel(c_ref, w_ref, b_ref, o_ref):
    c = c_ref[...]
    ca = c * jax.nn.sigmoid(c)
    o_ref[...] = jnp.dot(ca.astype(BF16), w_ref[...].astype(BF16), preferred_element_type=F32) + b_ref[...]


def _ada(c_pad, w, b, tn=1024):
    m, d = c_pad.shape
    n = w.shape[1]
    return pl.pallas_call(
        _ada_kernel,
        grid=(n // tn,),
        in_specs=[pl.BlockSpec((m, d), lambda j: (0, 0)),
                  pl.BlockSpec((d, tn), lambda j: (0, j)),
                  pl.BlockSpec((1, tn), lambda j: (0, j))],
        out_specs=pl.BlockSpec((m, tn), lambda j: (0, j)),
        out_shape=jax.ShapeDtypeStruct((m, n), F32),
        compiler_params=_cparams(("arbitrary",)),
        name="ada",
    )(c_pad, w, b)


def _modulated_norm(x, g, sc, sh):
    ms = jnp.mean(x * x, axis=-1, keepdims=True)
    return (x * lax.rsqrt(ms + EPS) * g) * (1.0 + sc) + sh


def _half_lane_norm(blk, g2, lo_mask, out_scale):
    sq = blk * blk
    s_lo = jnp.sum(jnp.where(lo_mask, sq, 0.0), axis=-1, keepdims=True)
    s_hi = jnp.sum(jnp.where(lo_mask, 0.0, sq), axis=-1, keepdims=True)
    inv = jnp.where(lo_mask, lax.rsqrt(s_lo * (1.0 / DH_DIFF) + EPS),
                    lax.rsqrt(s_hi * (1.0 / DH_DIFF) + EPS))
    return blk * inv * (g2 * out_scale)


def _in_proj_kernel(x_ref, sc_ref, sh_ref, g1_ref, wqkv_ref, wlat_ref, gqd_ref, gkd_ref,
                    gqa_ref, wqb_ref, gkva_ref, wkvb_ref, gqn_ref, gq1_ref, gq2_ref,
                    gkn_ref, gkp_ref, gkps_ref, cosq_ref, sinq_ref, cosk_ref, sink_ref,
                    qtd_ref, kd_ref, vtd_ref, qtm_ref, km_ref, vtm_ref, *, n_heads):
    tm = x_ref.shape[1]
    rows = tm // IN_PROJ_SPLIT
    wd = n_heads * LANES
    half = QK_ROPE // 2
    wx = n_heads * half
    lo_mask = lax.broadcasted_iota(jnp.int32, (rows, LANES), 1) < DH_DIFF
    lane_x = lax.broadcasted_iota(jnp.int32, (rows, wx), 1) // half
    qscale = DH_DIFF ** -0.5 * LOG2E
    mscale = QK_HEAD ** -0.5 * LOG2E

    def rms(v, g):
        return v * lax.rsqrt(jnp.mean(v * v, axis=-1, keepdims=True) + EPS) * g

    for sb in range(IN_PROJ_SPLIT):
        rs = slice(sb * rows, (sb + 1) * rows)
        h = _modulated_norm(x_ref[0, rs, :], g1_ref[...], sc_ref[0], sh_ref[0]).astype(BF16)
        lat = jnp.dot(h, wqkv_ref[:, 3 * wd:3 * wd + Q_LORA + KV_LORA], preferred_element_type=F32)
        cq = lat[:, 0:Q_LORA]
        ckv = lat[:, Q_LORA:Q_LORA + KV_LORA]
        kpe4 = jnp.dot(h, wlat_ref[...], preferred_element_type=F32)
        kpe2 = kpe4[:, 0:LANES]
        kpes2 = kpe4[:, LANES:2 * LANES]

        kv = jnp.dot(rms(ckv, gkva_ref[...]).astype(BF16), wkvb_ref[...], preferred_element_type=F32)
        for hh in range(n_heads):
            vtm_ref[0, hh, 0, :, rs] = kv[:, wd + hh * LANES:wd + (hh + 1) * LANES].T.astype(BF16)
        ss_pe = jnp.sum(jnp.where(lo_mask, kpe2 * kpe2, 0.0), axis=-1, keepdims=True)
        kr2 = kpe2 * gkp_ref[...] * cosk_ref[rs, :] + kpes2 * gkps_ref[...] * sink_ref[rs, :]
        for hh in range(n_heads):
            sl = slice(hh * LANES, (hh + 1) * LANES)
            kn = kv[:, sl]
            ss = jnp.sum(kn * kn, axis=-1, keepdims=True) + ss_pe
            r = lax.rsqrt(ss * (1.0 / QK_HEAD) + EPS)
            km_ref[0, hh, rs, 0:QK_NOPE] = (kn * r * gkn_ref[...]).astype(BF16)
            km_ref[0, hh, rs, QK_NOPE:QK_HEAD] = (kr2 * r)[:, 0:QK_ROPE].astype(BF16)

        qm = jnp.dot(rms(cq, gqa_ref[...]).astype(BF16), wqb_ref[...], preferred_element_type=F32)
        x1 = qm[:, wd:wd + wx]
        x2 = qm[:, wd + wx:wd + 2 * wx]
        sq_x = x1 * x1 + x2 * x2
        inv_x = jnp.zeros((rows, wx), F32)
        for hh in range(n_heads):
            sl = slice(hh * LANES, (hh + 1) * LANES)
            qn = qm[:, sl]
            ss = (jnp.sum(qn * qn, axis=-1, keepdims=True)
                  + jnp.sum(jnp.where(lane_x == hh, sq_x, 0.0), axis=-1, keepdims=True))
            r = lax.rsqrt(ss * (1.0 / QK_HEAD) + EPS) * mscale
            qtm_ref[0, hh, 0:QK_NOPE, rs] = (qn * r * gqn_ref[...]).T.astype(BF16)
            inv_x = jnp.where(lane_x == hh, r, inv_x)
        a1 = x1 * gq1_ref[...]
        a2 = x2 * gq2_ref[...]
        cq_t = cosq_ref[rs, :]
        sq_t = sinq_ref[rs, :]
        x1t = ((a1 * cq_t - a2 * sq_t) * inv_x).T.astype(BF16)
        x2t = ((a2 * cq_t + a1 * sq_t) * inv_x).T.astype(BF16)
        for hh in range(n_heads):
            qtm_ref[0, hh, QK_NOPE:QK_NOPE + half, rs] = x1t[hh * half:(hh + 1) * half]
            qtm_ref[0, hh, QK_NOPE + half:QK_HEAD, rs] = x2t[hh * half:(hh + 1) * half]

        vd = jnp.dot(h, wqkv_ref[:, 2 * wd:3 * wd], preferred_element_type=F32)
        for hh in range(n_heads):
            vtd_ref[0, hh, 0, :, rs] = vd[:, hh * LANES:(hh + 1) * LANES].T.astype(BF16)
        qd = jnp.dot(h, wqkv_ref[:, 0:wd], preferred_element_type=F32)
        for hh in range(n_heads):
            sl = slice(hh * LANES, (hh + 1) * LANES)
            qtd_ref[0, hh, :, rs] = _half_lane_norm(qd[:, sl], gqd_ref[...], lo_mask, qscale).T.astype(BF16)
        kd = jnp.dot(h, wqkv_ref[:, wd:2 * wd], preferred_element_type=F32)
        for hh in range(n_heads):
            sl = slice(hh * LANES, (hh + 1) * LANES)
            kd_ref[0, rs, sl] = _half_lane_norm(kd[:, sl], gkd_ref[...], lo_mask, 1.0).astype(BF16)


def _in_proj(x, sc1, sh1, g1, wqkv, wlat, gqd, gkd, gqa, wqb, gkva, wkvb, gqn, gq1, gq2,
             gkn, gkp, gkps, cosq, sinq, cosk, sink, *, n_heads, tm):
    b, s, d = x.shape
    wd = n_heads * LANES
    wx = n_heads * (QK_ROPE // 2)
    row = lambda w: pl.BlockSpec((1, tm, w), lambda bi, i: (bi, i, 0))
    mod = pl.BlockSpec((1, 1, d), lambda bi, i: (bi, 0, 0))
    tab = lambda w: pl.BlockSpec((tm, w), lambda bi, i: (i, 0))
    consts = [g1, wqkv, wlat, gqd, gkd, gqa, wqb, gkva, wkvb, gqn, gq1, gq2, gkn, gkp, gkps]
    return pl.pallas_call(
        functools.partial(_in_proj_kernel, n_heads=n_heads),
        grid=(b, s // tm),
        in_specs=[row(d), mod, mod] + [_const_spec(a.shape) for a in consts]
                 + [tab(wx), tab(wx), tab(LANES), tab(LANES)],
        out_specs=[pl.BlockSpec((1, n_heads, LANES, tm), lambda bi, i: (bi, 0, 0, i)),
                   row(wd),
                   pl.BlockSpec((1, n_heads, 1, DV_DIFF, tm), lambda bi, i: (bi, 0, i, 0, 0)),
                   pl.BlockSpec((1, n_heads, QK_HEAD, tm), lambda bi, i: (bi, 0, 0, i)),
                   pl.BlockSpec((1, n_heads, tm, QK_HEAD), lambda bi, i: (bi, 0, i, 0)),
                   pl.BlockSpec((1, n_heads, 1, V_MLA, tm), lambda bi, i: (bi, 0, i, 0, 0))],
        out_shape=[jax.ShapeDtypeStruct((b, n_heads, LANES, s), BF16),
                   jax.ShapeDtypeStruct((b, s, wd), BF16),
                   jax.ShapeDtypeStruct((b, n_heads, s // tm, DV_DIFF, tm), BF16),
                   jax.ShapeDtypeStruct((b, n_heads, QK_HEAD, s), BF16),
                   jax.ShapeDtypeStruct((b, n_heads, s, QK_HEAD), BF16),
                   jax.ShapeDtypeStruct((b, n_heads, s // tm, V_MLA, tm), BF16)],
        compiler_params=_cparams(("parallel", "parallel")),
        name="in_proj",
    )(x, sc1, sh1, *consts, cosq, sinq, cosk, sink)


def _t5_bucket(rel):
    nb = REL_BUCKETS // 2
    max_exact = nb // 2
    base = jnp.where(rel > 0, nb, 0)
    n = jnp.abs(rel)
    nf = jnp.maximum(n, 1).astype(jnp.float32)
    large = max_exact + (jnp.log(nf / max_exact) / math.log(REL_MAX_DIST / max_exact)
                         * (nb - max_exact)).astype(jnp.int32)
    large = jnp.minimum(large, nb - 1)
    return base + jnp.where(n < max_exact, n, large)


def _bias_kernel(rb_ref, bucket_ref, o_ref):
    hh = pl.program_id(0)
    t = o_ref.shape[-1]
    far_left = rb_ref[REL_BUCKETS // 2 - 1, hh]
    far_right = rb_ref[REL_BUCKETS - 1, hh]
    o_ref[0, 0] = jnp.full((t, t), far_left, F32) * LOG2E
    o_ref[0, 4] = jnp.full((t, t), far_right, F32) * LOG2E
    bk_strip = bucket_ref[...]
    strip = jnp.zeros(bk_strip.shape, F32)
    for bk in range(REL_BUCKETS):
        strip = jnp.where(bk_strip == bk, rb_ref[bk, hh], strip)
    strip = strip * LOG2E
    for d in (-1, 0, 1):
        diag = jnp.concatenate([strip[:, (2 - d) * t:(3 - d) * t], strip[:, (1 - d) * t:(2 - d) * t]], axis=1)
        rows = jnp.broadcast_to(diag, (t, 2 * t))
        o_ref[0, d + 2] = pltpu.roll(rows, 0, 1, stride=1, stride_axis=0)[:, :t]


def _bias_tiles(rel_bias, t):
    n_heads = rel_bias.shape[1]
    buckets = _t5_bucket(2 * t - jnp.arange(4 * t, dtype=jnp.int32))[None, :]
    return pl.pallas_call(
        _bias_kernel,
        grid=(n_heads,),
        in_specs=[pl.BlockSpec(memory_space=pltpu.SMEM),
                  pl.BlockSpec((1, 4 * t), lambda hh: (0, 0))],
        out_specs=pl.BlockSpec((1, 5, t, t), lambda hh: (hh, 0, 0, 0)),
        out_shape=jax.ShapeDtypeStruct((n_heads, 5, t, t), F32),
        compiler_params=_cparams(("arbitrary",)),
        name="rel_bias_tiles",
    )(rel_bias, buckets)


ONES_ROWS = 16


def _softmax_state_step(s, offset, vt, state):
    m_new, alpha, p = _softmax_weights(s, offset, state[0])
    return m_new, _accumulate(alpha, state[1], vt, p)


def _softmax_weights(s, offset, m_old):
    m_new = jnp.maximum(m_old, jnp.max(s, axis=0, keepdims=True) + offset)
    alpha = jnp.exp2(m_old - m_new)
    p = jnp.exp2(s - (m_new - offset)).astype(BF16)
    return m_new, alpha, p


def _accumulate(alpha, acc_old, vt, p):
    return alpha * acc_old + jnp.dot(vt, p, preferred_element_type=F32)


def _init_state(dv, t):
    return (jnp.full((1, t), -jnp.inf, F32), jnp.zeros((dv + ONES_ROWS, t), F32))


def _with_ones(vt):
    return jnp.concatenate([vt, jnp.ones((ONES_ROWS, vt.shape[1]), vt.dtype)], axis=0)


def _normalised(state, dv):
    _, acc = state
    return acc[:dv] / acc[dv:dv + 1]


def _diff_attn_kernel(rb_ref, lamv_ref, gsub_ref, qt_ref, k_ref, vt_ref, bias_ref, o_ref, *,
                      lambda_init, bounded):
    hh = pl.program_id(0)
    t = bias_ref.shape[-1]
    nq = qt_ref.shape[-1] // t
    nkt = vt_ref.shape[2]

    off_left = rb_ref[REL_BUCKETS // 2 - 1, hh] * LOG2E
    off_right = rb_ref[REL_BUCKETS - 1, hh] * LOG2E
    lv = lamv_ref[...]
    lam = (jnp.exp(jnp.sum(lv[0:1] * lv[1:2], axis=-1, keepdims=True))
           - jnp.exp(jnp.sum(lv[2:3] * lv[3:4], axis=-1, keepdims=True)) + lambda_init)

    n_near = min(nkt, 3)
    row = lax.broadcasted_iota(jnp.int32, (LANES, t), 0)
    zero = jnp.zeros((LANES, t), BF16)

    def query_maps(j):
        qt = qt_ref[0, 0, :, j * t:(j + 1) * t]
        return jnp.where(row < DH_DIFF, qt, zero), jnp.where(row < DH_DIFF, zero, qt)

    def tile_index(j, r):
        return lax.rem(pl.program_id(2) * nq + j - 1 + r + nkt, nkt)

    def far_offset(j, r):
        return jnp.where(tile_index(j, r) < pl.program_id(2) * nq + j, off_left, off_right)

    def biased_scores(j, qmap, r, k0, ck):
        kj = tile_index(j, r)
        k_t = k_ref[0, pl.ds(pl.multiple_of(kj * t + k0, ck), ck), :]
        s = jnp.dot(k_t, qmap, preferred_element_type=F32)
        saturated = (r >= n_near or (nkt >= 3 and r == 0 and k0 + ck - 1 - t <= -REL_MAX_DIST)
                     or (nkt >= 3 and r == 2 and k0 + 1 >= REL_MAX_DIST))
        if not (bounded and saturated) and r < n_near:
            s = s + bias_ref[0, jnp.clip(kj - (pl.program_id(2) * nq + j), -2, 2) + 2, k0:k0 + ck, :]
        elif bounded:
            s = s + far_offset(j, r)
        return s

    def finish(j, outs):
        o = outs[0] - lam * outs[1]
        ms = jnp.mean(o * o, axis=0, keepdims=True)
        y = o * lax.rsqrt(ms + EPS) * (gsub_ref[...] * (1.0 - lambda_init))
        o_ref[0, j * t:(j + 1) * t, :] = y.T.astype(BF16)

    if bounded:
        ck = min(t, KEY_CHUNK)
        npt = t // ck
        qmaps = [query_maps(j) for j in range(nq)]
        items = [(j, r, c, mp) for j in range(nq) for r in range(nkt) for c in range(npt) for mp in range(2)]
        score = lambda it: biased_scores(it[0], qmaps[it[0]][it[3]], it[1], it[2] * ck, ck)
        depth = min(BOUNDED_DEPTH, len(items))
        pending = [score(it) for it in items[:depth]]
        for n, (j, r, c, mp) in enumerate(items):
            if (r, c, mp) == (0, 0, 0):
                accs = [jnp.zeros((DV_DIFF, t), F32) for _ in range(2)]
                dens = [jnp.zeros((SUBLANES, t), F32) for _ in range(2)]
            s = pending.pop(0)
            if n + depth < len(items):
                pending.append(score(items[n + depth]))
            p = jnp.exp2(s)
            dens[mp] = dens[mp] + jnp.sum(p.reshape(ck // SUBLANES, SUBLANES, t), axis=0)
            accs[mp] = accs[mp] + jnp.dot(vt_ref[0, 0, tile_index(j, r), :, c * ck:(c + 1) * ck],
                                          p.astype(BF16), preferred_element_type=F32)
            if (r, c, mp) == (nkt - 1, npt - 1, 1):
                finish(j, [acc / jnp.sum(den, axis=0, keepdims=True) for acc, den in zip(accs, dens)])
    else:
        for j in range(nq):
            qmap = query_maps(j)
            states = [_init_state(DV_DIFF, t), _init_state(DV_DIFF, t)]
            score = lambda r: [biased_scores(j, qmap[mp], r, 0, t) for mp in range(2)]
            s_next = score(0)
            for r in range(nkt):
                s_cur = s_next
                if r + 1 < nkt:
                    s_next = score(r + 1)
                vt = _with_ones(vt_ref[0, 0, tile_index(j, r)])
                offset = 0.0 if r < n_near else far_offset(j, r)
                for mp in range(2):
                    states[mp] = _softmax_state_step(s_cur[mp], offset, vt, states[mp])
            finish(j, [_normalised(st, DV_DIFF) for st in states])


def _diff_attn(rel_bias, lamv, gsub_col, qt, k, vt, bias, *, lambda_init, bounded):
    b, n_heads, _, s = qt.shape
    t = bias.shape[-1]
    nkt, tk = vt.shape[2], vt.shape[4]
    assert tk == t
    tq = _pick_tile(s, t * (DIFF_BOUNDED_Q_TILES if bounded else 1))
    return pl.pallas_call(
        functools.partial(_diff_attn_kernel, lambda_init=lambda_init, bounded=bounded),
        grid=(n_heads, b, s // tq),
        in_specs=[pl.BlockSpec(memory_space=pltpu.SMEM),
                  pl.BlockSpec(lamv.shape, lambda hh, bi, qi: (0, 0)),
                  pl.BlockSpec(gsub_col.shape, lambda hh, bi, qi: (0, 0)),
                  pl.BlockSpec((1, 1, LANES, tq), lambda hh, bi, qi: (bi, hh, 0, qi)),
                  pl.BlockSpec((1, s, LANES), lambda hh, bi, qi: (bi, 0, hh)),
                  pl.BlockSpec((1, 1, nkt, DV_DIFF, tk), lambda hh, bi, qi: (bi, hh, 0, 0, 0)),
                  pl.BlockSpec((1, 5, t, t), lambda hh, bi, qi: (hh, 0, 0, 0))],
        out_specs=pl.BlockSpec((1, tq, DV_DIFF), lambda hh, bi, qi: (bi, qi, hh)),
        out_shape=jax.ShapeDtypeStruct((b, s, n_heads * DV_DIFF), BF16),
        compiler_params=_cparams(("parallel", "parallel", "parallel")),
        name="diff_attn",
    )(rel_bias, lamv, gsub_col, qt, k, vt, bias)


def _mla_attn_kernel(qt_ref, k_ref, vt_ref, o_ref):
    nkt = vt_ref.shape[2]
    tk = vt_ref.shape[-1]
    qt = qt_ref[0, 0]
    m, acc = _init_state(V_MLA, qt.shape[-1])
    scores = lambda kj: jnp.dot(k_ref[0, 0, kj * tk:(kj + 1) * tk, :], qt, preferred_element_type=F32)
    depth = 2
    pending = [scores(kj) for kj in range(min(depth, nkt))]
    for kj in range(nkt):
        s = pending.pop(0)
        if kj + depth < nkt:
            pending.append(scores(kj + depth))
        m, alpha, p = _softmax_weights(s, 0.0, m)
        acc = _accumulate(alpha, acc, _with_ones(vt_ref[0, 0, kj]), p)
    o_ref[0] = _normalised((m, acc), V_MLA).T.astype(BF16)


def _mla_attn_bounded_kernel(qt_ref, k_ref, vt_ref, o_ref):
    nkt = vt_ref.shape[2]
    tk = vt_ref.shape[-1]
    t = min(qt_ref.shape[-1], MLA_TQ)
    nq = qt_ref.shape[-1] // t
    ck = min(tk, KEY_CHUNK)
    npt = tk // ck
    n_chunks = nkt * npt
    items = [(j, i) for j in range(nq) for i in range(n_chunks)]

    def scores(item):
        j, i = item
        return jnp.dot(k_ref[0, 0, i * ck:(i + 1) * ck, :], qt_ref[0, 0, :, j * t:(j + 1) * t],
                       preferred_element_type=F32)

    depth = min(BOUNDED_DEPTH, len(items))
    pending = [scores(it) for it in items[:depth]]
    for n, (j, i) in enumerate(items):
        if i == 0:
            acc = jnp.zeros((V_MLA, t), F32)
            den = jnp.zeros((SUBLANES, t), F32)
        s = pending.pop(0)
        if n + depth < len(items):
            pending.append(scores(items[n + depth]))
        c = i % npt
        p = jnp.exp2(s)
        den = den + jnp.sum(p.reshape(ck // SUBLANES, SUBLANES, t), axis=0)
        acc = acc + jnp.dot(vt_ref[0, 0, i // npt, :, c * ck:(c + 1) * ck], p.astype(BF16),
                            preferred_element_type=F32)
        if i == n_chunks - 1:
            o_ref[0, j * t:(j + 1) * t, :] = (acc / jnp.sum(den, axis=0, keepdims=True)).T.astype(BF16)


def _mla_attn(qt, k, vt, *, bounded):
    b, n_heads, _, s = qt.shape
    t = _pick_tile(s, MLA_TQ * (MLA_BOUNDED_Q_TILES if bounded else 1))
    nkt, tk = vt.shape[2], vt.shape[4]
    return pl.pallas_call(
        _mla_attn_bounded_kernel if bounded else _mla_attn_kernel,
        grid=(b, n_heads, s // t),
        in_specs=[pl.BlockSpec((1, 1, QK_HEAD, t), lambda bi, hh, qi: (bi, hh, 0, qi)),
                  pl.BlockSpec((1, 1, s, QK_HEAD), lambda bi, hh, qi: (bi, hh, 0, 0)),
                  pl.BlockSpec((1, 1, nkt, V_MLA, tk), lambda bi, hh, qi: (bi, hh, 0, 0, 0))],
        out_specs=pl.BlockSpec((1, t, V_MLA), lambda bi, hh, qi: (bi, qi, hh)),
        out_shape=jax.ShapeDtypeStruct((b, s, n_heads * V_MLA), BF16),
        compiler_params=_cparams(("parallel", "parallel", "parallel")),
        name="mla_attn",
    )(qt, k, vt)


def _out_proj_kernel(x_ref, gt_ref, a_ref, b_ref, wa_ref, wb_ref, sc_ref, sh_ref, g2_ref, o_ref, h_ref):
    tm = x_ref.shape[1]
    rows = tm // OUT_PROJ_SPLIT
    mixes = []
    for i in range(OUT_PROJ_SPLIT):
        sl = slice(i * rows, (i + 1) * rows)
        mixes.append(jnp.dot(a_ref[0, sl, :], wa_ref[...], preferred_element_type=F32)
                     + jnp.dot(b_ref[0, sl, :], wb_ref[...], preferred_element_type=F32))
    for i in range(OUT_PROJ_SPLIT):
        sl = slice(i * rows, (i + 1) * rows)
        x1 = x_ref[0, sl, :] + gt_ref[0] * mixes[i]
        o_ref[0, sl, :] = x1
        h_ref[0, sl, :] = _modulated_norm(x1, g2_ref[...], sc_ref[0], sh_ref[0]).astype(BF16)


def _out_proj(x, gt1, a, bb, wa, wb, sc2, sh2, g2, *, tm):
    b, s, d = x.shape
    row = lambda w: pl.BlockSpec((1, tm, w), lambda bi, i: (bi, i, 0))
    mod = pl.BlockSpec((1, 1, d), lambda bi, i: (bi, 0, 0))
    return pl.pallas_call(
        _out_proj_kernel,
        grid=(b, s // tm),
        in_specs=[row(d), mod, row(a.shape[-1]), row(bb.shape[-1]), _const_spec(wa.shape),
                  _const_spec(wb.shape), mod, mod, _const_spec(g2.shape)],
        out_specs=[row(d), row(d)],
        out_shape=[jax.ShapeDtypeStruct((b, s, d), F32), jax.ShapeDtypeStruct((b, s, d), BF16)],
        compiler_params=_cparams(("parallel", "parallel")),
        name="out_proj",
    )(x, gt1, a, bb, wa, wb, sc2, sh2, g2)


def _ffn_kernel(x_ref, h_ref, gt_ref, wg_ref, wu_ref, wd_ref, o_ref, xs_ref, *, n_x_chunks):
    j = pl.program_id(2)
    rows = x_ref.shape[1]

    @pl.when(j == 0)
    def _():
        o_ref[...] = jnp.zeros(o_ref.shape, F32)

    chunk = jnp.minimum(j, n_x_chunks - 1)
    xs_ref[pl.ds(pl.multiple_of(chunk * rows, rows), rows), :] = x_ref[0]

    h = h_ref[0]
    tf = wg_ref.shape[1]
    cf = tf // FFN_COL_SPLIT
    gates = []
    for c in range(FFN_COL_SPLIT):
        cs = slice(c * cf, (c + 1) * cf)
        gates.append((jnp.dot(h, wg_ref[:, cs], preferred_element_type=F32),
                      jnp.dot(h, wu_ref[:, cs], preferred_element_type=F32)))
    part = None
    for c, (g, u) in enumerate(gates):
        act = (g * jax.nn.sigmoid(g) * u).astype(BF16)
        d = jnp.dot(act, wd_ref[c * cf:(c + 1) * cf, :], preferred_element_type=F32)
        part = d if part is None else part + d
    o_ref[0] += part

    @pl.when(j == pl.num_programs(2) - 1)
    def _():
        o_ref[0] = xs_ref[...] + gt_ref[0] * o_ref[0]


def _ffn(x, h2, gt2, wg, wu, wd, *, tm, tf):
    b, s, d = x.shape
    dff = wg.shape[1]
    nj = dff // tf
    nxc = 1
    while nxc * 2 <= nj and (tm // (nxc * 2)) % SUBLANES == 0:
        nxc *= 2
    row = pl.BlockSpec((1, tm, d), lambda bi, i, j: (bi, i, 0))
    x_chunk = pl.BlockSpec((1, tm // nxc, d), lambda bi, i, j: (bi, i * nxc + jnp.minimum(j, nxc - 1), 0))
    return pl.pallas_call(
        functools.partial(_ffn_kernel, n_x_chunks=nxc),
        grid=(b, s // tm, nj),
        in_specs=[x_chunk, row, pl.BlockSpec((1, 1, d), lambda bi, i, j: (bi, 0, 0)),
                  pl.BlockSpec((d, tf), lambda bi, i, j: (0, j)),
                  pl.BlockSpec((d, tf), lambda bi, i, j: (0, j)),
                  pl.BlockSpec((tf, d), lambda bi, i, j: (j, 0))],
        out_specs=row,
        out_shape=jax.ShapeDtypeStruct((b, s, d), F32),
        scratch_shapes=[pltpu.VMEM((tm, d), F32)],
        compiler_params=_cparams(("parallel", "parallel", "arbitrary")),
        name="ffn",
    )(x, h2, gt2, wg, wu, wd)


def _rope_tables(s):
    pos = jnp.arange(s, dtype=jnp.float32)
    inv = 1.0 / (ROPE_THETA ** (jnp.arange(0, QK_ROPE, 2, dtype=jnp.float32) / QK_ROPE))
    ang = pos[:, None] * inv[None, :]
    return jnp.cos(ang), jnp.sin(ang)


def _pick_tile(n, pref):
    t = min(pref, n)
    while n % t:
        t //= 2
    return t


def kernel(x, c, rel_bias, w_ada, b_ada, g_norm1, w_in, g_q_diff, g_k_diff, lambda_vecs, g_subln, g_q_a, w_q_b, g_kv_a, w_kv_b, g_q_mla, g_k_mla, w_out, g_norm2, w_gate, w_up, w_down):
    b, s, d = x.shape
    depth = w_ada.shape[0]
    diff_width = d // 2
    n_hd = diff_width // DV_DIFF
    n_hm = (d - diff_width) // V_MLA
    assert n_hd == n_hm and n_hd % 2 == 0
    n_heads = n_hd
    wd = n_heads * LANES
    half = QK_ROPE // 2
    tk = _pick_tile(s, ATT_TK)
    nkt = s // tk
    t = tk
    assert t + 1 >= REL_MAX_DIST
    tm = _pick_tile(s, 512)

    cos, sin = _rope_tables(s)
    cosq, sinq = jnp.tile(cos, (1, n_heads)), jnp.tile(sin, (1, n_heads))
    cosk = jnp.tile(cos, (1, 4))
    sink = jnp.tile(jnp.concatenate([-sin, sin], axis=1), (1, 2))

    c_pad = jnp.pad(c, ((0, (-b) % 8), (0, 0)))

    for l in range(depth):
        lambda_init = 0.8 - 0.6 * math.exp(-0.3 * l)
        mod = _ada(c_pad, w_ada[l], b_ada[l][None, :])[:b]
        sh1, sc1, gt1, sh2, sc2, gt2 = [m[:, None, :] for m in jnp.split(mod, 6, axis=-1)]

        wi = w_in[l]
        o = 3 * wd + Q_LORA + KV_LORA
        w_kpe = wi[:, o:o + QK_ROPE]
        w_kpe_sw = jnp.concatenate([w_kpe[:, half:], w_kpe[:, :half]], axis=1)
        wqkv = wi.astype(BF16)
        wlat = jnp.concatenate([w_kpe, w_kpe, w_kpe_sw, w_kpe_sw], axis=1).astype(BF16)
        wq = w_q_b[l].reshape(Q_LORA, n_heads, QK_HEAD)
        wqb = jnp.concatenate([wq[:, :, :QK_NOPE].reshape(Q_LORA, -1),
                               wq[:, :, QK_NOPE:QK_NOPE + half].reshape(Q_LORA, -1),
                               wq[:, :, QK_NOPE + half:].reshape(Q_LORA, -1)], axis=1).astype(BF16)
        wkv = w_kv_b[l].reshape(KV_LORA, n_heads, QK_NOPE + V_MLA)
        wkvb = jnp.concatenate([wkv[:, :, :QK_NOPE].reshape(KV_LORA, -1),
                                wkv[:, :, QK_NOPE:].reshape(KV_LORA, -1)], axis=1).astype(BF16)
        gq, gk = g_q_mla[l], g_k_mla[l]
        gk_pe = gk[QK_NOPE:]
        gk_pe_sw = jnp.concatenate([gk_pe[half:], gk_pe[:half]])

        qt_d, kd, vt_d, qt_m, k_m, vt_m = _in_proj(
            x, sc1, sh1, g_norm1[l][None, :], wqkv, wlat,
            jnp.tile(g_q_diff[l], 2)[None, :], jnp.tile(g_k_diff[l], 2)[None, :],
            g_q_a[l][None, :], wqb, g_kv_a[l][None, :], wkvb,
            gq[None, :QK_NOPE], jnp.tile(gq[QK_NOPE:QK_NOPE + half], n_heads)[None, :],
            jnp.tile(gq[QK_NOPE + half:], n_heads)[None, :],
            gk[None, :QK_NOPE], jnp.tile(gk_pe, 2)[None, :], jnp.tile(gk_pe_sw, 2)[None, :],
            cosq, sinq, cosk, sink, n_heads=n_heads, tm=tk)

        bias = _bias_tiles(rel_bias, t)
        gmax = lambda g: jnp.max(jnp.abs(g))
        bound_d = (1.02 * LOG2E * DH_DIFF ** 0.5 * gmax(g_q_diff[l]) * gmax(g_k_diff[l])
                   + LOG2E * gmax(rel_bias))
        bound_m = 1.02 * LOG2E * QK_HEAD ** 0.5 * gmax(g_q_mla[l]) * gmax(g_k_mla[l])
        diff_args = (rel_bias, lambda_vecs[l], g_subln[l][:, None], qt_d, kd, vt_d, bias)
        a_out = lax.cond(
            bound_d <= SCORE_BOUND,
            lambda *a: _diff_attn(*a, lambda_init=lambda_init, bounded=True),
            lambda *a: _diff_attn(*a, lambda_init=lambda_init, bounded=False), *diff_args)
        b_out = lax.cond(
            bound_m <= SCORE_BOUND,
            lambda *a: _mla_attn(*a, bounded=True),
            lambda *a: _mla_attn(*a, bounded=False), qt_m, k_m, vt_m)

        wo = w_out[l].astype(BF16)
        x, h2 = _out_proj(x, gt1, a_out, b_out, wo[:diff_width], wo[diff_width:], sc2, sh2,
                          g_norm2[l][None, :], tm=tm)
        x = _ffn(x, h2, gt2, w_gate[l].astype(BF16), w_up[l].astype(BF16), w_down[l].astype(BF16),
                 tm=_pick_tile(s, FFN_TM), tf=_pick_tile(w_gate.shape[-1], FFN_TF))
    return x
```

```python
import functools
import math

import jax
import jax.numpy as jnp
from jax import lax
from jax.experimental import pallas as pl
from jax.experimental.pallas import tpu as pltpu

F32 = jnp.float32
BF16 = jnp.bfloat16

DH_DIFF = 64
DV_DIFF = 2 * DH_DIFF
QK_NOPE = 128
QK_ROPE = 64
QK_HEAD = QK_NOPE + QK_ROPE
V_MLA = 128
Q_LORA = 512
KV_LORA = 256
ROPE_THETA = 10000.0
REL_BUCKETS = 32
REL_MAX_DIST = 128
EPS = 1e-6
LOG2E = 1.4426950408889634

LANES = 128
SUBLANES = 8
VMEM_LIMIT_BYTES = 56 * 1024 * 1024

OUT_PROJ_SPLIT = 4
IN_PROJ_SPLIT = 2
FFN_TM = 1024
FFN_TF = 512
FFN_COL_SPLIT = 2
ATT_TK = 512
DIFF_TQ = 512
MLA_TQ = 512
KEY_CHUNK = 256
BOUNDED_DEPTH = 2
DIFF_BOUNDED_Q_TILES = 8
MLA_BOUNDED_Q_TILES = 8
SCORE_BOUND = 40.0


def _cparams(sem):
    return pltpu.CompilerParams(dimension_semantics=sem, vmem_limit_bytes=VMEM_LIMIT_BYTES)


def _const_spec(shape):
    nd = len(shape)
    return pl.BlockSpec(shape, lambda *_: (0,) * nd, pipeline_mode=pl.Buffered(1))


def _ada_kernel(c_ref, w_ref, b_ref, o_ref):
    c = c_ref[...]
    ca = c * jax.nn.sigmoid(c)
    o_ref[...] = jnp.dot(ca.astype(BF16), w_ref[...].astype(BF16), preferred_element_type=F32) + b_ref[...]


def _ada(c_pad, w, b, tn=1024):
    m, d = c_pad.shape
    n = w.shape[1]
    return pl.pallas_call(
        _ada_kernel,
        grid=(n // tn,),
        in_specs=[pl.BlockSpec((m, d), lambda j: (0, 0)),
                  pl.BlockSpec((d, tn), lambda j: (0, j)),
                  pl.BlockSpec((1, tn), lambda j: (0, j))],
        out_specs=pl.BlockSpec((m, tn), lambda j: (0, j)),
        out_shape=jax.ShapeDtypeStruct((m, n), F32),
        compiler_params=_cparams(("arbitrary",)),
        name="ada",
    )(c_pad, w, b)


def _modulated_norm(x, g, sc, sh):
    ms = jnp.mean(x * x, axis=-1, keepdims=True)
    return (x * lax.rsqrt(ms + EPS) * g) * (1.0 + sc) + sh


def _half_lane_norm(blk, g2, lo_mask, out_scale):
    sq = blk * blk
    s_lo = jnp.sum(jnp.where(lo_mask, sq, 0.0), axis=-1, keepdims=True)
    s_hi = jnp.sum(jnp.where(lo_mask, 0.0, sq), axis=-1, keepdims=True)
    inv = jnp.where(lo_mask, lax.rsqrt(s_lo * (1.0 / DH_DIFF) + EPS),
                    lax.rsqrt(s_hi * (1.0 / DH_DIFF) + EPS))
    return blk * inv * (g2 * out_scale)


def _in_proj_kernel(x_ref, sc_ref, sh_ref, g1_ref, wqkv_ref, wlat_ref, gqd_ref, gkd_ref,
                    gqa_ref, wqb_ref, gkva_ref, wkvb_ref, gqn_ref, gq1_ref, gq2_ref,
                    gkn_ref, gkp_ref, gkps_ref, cosq_ref, sinq_ref, cosk_ref, sink_ref,
                    qtd_ref, kd_ref, vtd_ref, qtm_ref, km_ref, vtm_ref, *, n_heads):
    tm = x_ref.shape[1]
    rows = tm // IN_PROJ_SPLIT
    wd = n_heads * LANES
    half = QK_ROPE // 2
    wx = n_heads * half
    lo_mask = lax.broadcasted_iota(jnp.int32, (rows, LANES), 1) < DH_DIFF
    lane_x = lax.broadcasted_iota(jnp.int32, (rows, wx), 1) // half
    qscale = DH_DIFF ** -0.5 * LOG2E
    mscale = QK_HEAD ** -0.5 * LOG2E

    def rms(v, g):
        return v * lax.rsqrt(jnp.mean(v * v, axis=-1, keepdims=True) + EPS) * g

    for sb in range(IN_PROJ_SPLIT):
        rs = slice(sb * rows, (sb + 1) * rows)
        h = _modulated_norm(x_ref[0, rs, :], g1_ref[...], sc_ref[0], sh_ref[0]).astype(BF16)
        lat = jnp.dot(h, wqkv_ref[:, 3 * wd:3 * wd + Q_LORA + KV_LORA], preferred_element_type=F32)
        cq = lat[:, 0:Q_LORA]
        ckv = lat[:, Q_LORA:Q_LORA + KV_LORA]
        kpe4 = jnp.dot(h, wlat_ref[...], preferred_element_type=F32)
        kpe2 = kpe4[:, 0:LANES]
        kpes2 = kpe4[:, LANES:2 * LANES]

        kv = jnp.dot(rms(ckv, gkva_ref[...]).astype(BF16), wkvb_ref[...], preferred_element_type=F32)
        for hh in range(n_heads):
            vtm_ref[0, hh, 0, :, rs] = kv[:, wd + hh * LANES:wd + (hh + 1) * LANES].T.astype(BF16)
        ss_pe = jnp.sum(jnp.where(lo_mask, kpe2 * kpe2, 0.0), axis=-1, keepdims=True)
        kr2 = kpe2 * gkp_ref[...] * cosk_ref[rs, :] + kpes2 * gkps_ref[...] * sink_ref[rs, :]
        for hh in range(n_heads):
            sl = slice(hh * LANES, (hh + 1) * LANES)
            kn = kv[:, sl]
            ss = jnp.sum(kn * kn, axis=-1, keepdims=True) + ss_pe
            r = lax.rsqrt(ss * (1.0 / QK_HEAD) + EPS)
            km_ref[0, hh, rs, 0:QK_NOPE] = (kn * r * gkn_ref[...]).astype(BF16)
            km_ref[0, hh, rs, QK_NOPE:QK_HEAD] = (kr2 * r)[:, 0:QK_ROPE].astype(BF16)

        qm = jnp.dot(rms(cq, gqa_ref[...]).astype(BF16), wqb_ref[...], preferred_element_type=F32)
        x1 = qm[:, wd:wd + wx]
        x2 = qm[:, wd + wx:wd + 2 * wx]
        sq_x = x1 * x1 + x2 * x2
        inv_x = jnp.zeros((rows, wx), F32)
        for hh in range(n_heads):
            sl = slice(hh * LANES, (hh + 1) * LANES)
            qn = qm[:, sl]
            ss = (jnp.sum(qn * qn, axis=-1, keepdims=True)
                  + jnp.sum(jnp.where(lane_x == hh, sq_x, 0.0), axis=-1, keepdims=True))
            r = lax.rsqrt(ss * (1.0 / QK_HEAD) + EPS) * mscale
            qtm_ref[0, hh, 0:QK_NOPE, rs] = (qn * r * gqn_ref[...]).T.astype(BF16)
            inv_x = jnp.where(lane_x == hh, r, inv_x)
        a1 = x1 * gq1_ref[...]
        a2 = x2 * gq2_ref[...]
        cq_t = cosq_ref[rs, :]
        sq_t = sinq_ref[rs, :]
        x1t = ((a1 * cq_t - a2 * sq_t) * inv_x).T.astype(BF16)
        x2t = ((a2 * cq_t + a1 * sq_t) * inv_x).T.astype(BF16)
        for hh in range(n_heads):
            qtm_ref[0, hh, QK_NOPE:QK_NOPE + half, rs] = x1t[hh * half:(hh + 1) * half]
            qtm_ref[0, hh, QK_NOPE + half:QK_HEAD, rs] = x2t[hh * half:(hh + 1) * half]

        vd = jnp.dot(h, wqkv_ref[:, 2 * wd:3 * wd], preferred_element_type=F32)
        for hh in range(n_heads):
            vtd_ref[0, hh, 0, :, rs] = vd[:, hh * LANES:(hh + 1) * LANES].T.astype(BF16)
        qd = jnp.dot(h, wqkv_ref[:, 0:wd], preferred_element_type=F32)
        for hh in range(n_heads):
            sl = slice(hh * LANES, (hh + 1) * LANES)
            qtd_ref[0, hh, :, rs] = _half_lane_norm(qd[:, sl], gqd_ref[...], lo_mask, qscale).T.astype(BF16)
        kd = jnp.dot(h, wqkv_ref[:, wd:2 * wd], preferred_element_type=F32)
        for hh in range(n_heads):
            sl = slice(hh * LANES, (hh + 1) * LANES)
            kd_ref[0, rs, sl] = _half_lane_norm(kd[:, sl], gkd_ref[...], lo_mask, 1.0).astype(BF16)


def _in_proj(x, sc1, sh1, g1, wqkv, wlat, gqd, gkd, gqa, wqb, gkva, wkvb, gqn, gq1, gq2,
             gkn, gkp, gkps, cosq, sinq, cosk, sink, *, n_heads, tm):
    b, s, d = x.shape
    wd = n_heads * LANES
    wx = n_heads * (QK_ROPE // 2)
    row = lambda w: pl.BlockSpec((1, tm, w), lambda bi, i: (bi, i, 0))
    mod = pl.BlockSpec((1, 1, d), lambda bi, i: (bi, 0, 0))
    tab = lambda w: pl.BlockSpec((tm, w), lambda bi, i: (i, 0))
    consts = [g1, wqkv, wlat, gqd, gkd, gqa, wqb, gkva, wkvb, gqn, gq1, gq2, gkn, gkp, gkps]
    return pl.pallas_call(
        functools.partial(_in_proj_kernel, n_heads=n_heads),
        grid=(b, s // tm),
        in_specs=[row(d), mod, mod] + [_const_spec(a.shape) for a in consts]
                 + [tab(wx), tab(wx), tab(LANES), tab(LANES)],
        out_specs=[pl.BlockSpec((1, n_heads, LANES, tm), lambda bi, i: (bi, 0, 0, i)),
                   row(wd),
                   pl.BlockSpec((1, n_heads, 1, DV_DIFF, tm), lambda bi, i: (bi, 0, i, 0, 0)),
                   pl.BlockSpec((1, n_heads, QK_HEAD, tm), lambda bi, i: (bi, 0, 0, i)),
                   pl.BlockSpec((1, n_heads, tm, QK_HEAD), lambda bi, i: (bi, 0, i, 0)),
                   pl.BlockSpec((1, n_heads, 1, V_MLA, tm), lambda bi, i: (bi, 0, i, 0, 0))],
        out_shape=[jax.ShapeDtypeStruct((b, n_heads, LANES, s), BF16),
                   jax.ShapeDtypeStruct((b, s, wd), BF16),
                   jax.ShapeDtypeStruct((b, n_heads, s // tm, DV_DIFF, tm), BF16),
                   jax.ShapeDtypeStruct((b, n_heads, QK_HEAD, s), BF16),
                   jax.ShapeDtypeStruct((b, n_heads, s, QK_HEAD), BF16),
                   jax.ShapeDtypeStruct((b, n_heads, s // tm, V_MLA, tm), BF16)],
        compiler_params=_cparams(("parallel", "parallel")),
        name="in_proj",
    )(x, sc1, sh1, *consts, cosq, sinq, cosk, sink)


def _t5_bucket(rel):
    nb = REL_BUCKETS // 2
    max_exact = nb // 2
    base = jnp.where(rel > 0, nb, 0)
    n = jnp.abs(rel)
    nf = jnp.maximum(n, 1).astype(jnp.float32)
    large = max_exact + (jnp.log(nf / max_exact) / math.log(REL_MAX_DIST / max_exact)
                         * (nb - max_exact)).astype(jnp.int32)
    large = jnp.minimum(large, nb - 1)
    return base + jnp.where(n < max_exact, n, large)


def _bias_kernel(rb_ref, bucket_ref, o_ref):
    hh = pl.program_id(0)
    t = o_ref.shape[-1]
    far_left = rb_ref[REL_BUCKETS // 2 - 1, hh]
    far_right = rb_ref[REL_BUCKETS - 1, hh]
    o_ref[0, 0] = jnp.full((t, t), far_left, F32) * LOG2E
    o_ref[0, 4] = jnp.full((t, t), far_right, F32) * LOG2E
    bk_strip = bucket_ref[...]
    strip = jnp.zeros(bk_strip.shape, F32)
    for bk in range(REL_BUCKETS):
        strip = jnp.where(bk_strip == bk, rb_ref[bk, hh], strip)
    strip = strip * LOG2E
    for d in (-1, 0, 1):
        diag = jnp.concatenate([strip[:, (2 - d) * t:(3 - d) * t], strip[:, (1 - d) * t:(2 - d) * t]], axis=1)
        rows = jnp.broadcast_to(diag, (t, 2 * t))
        o_ref[0, d + 2] = pltpu.roll(rows, 0, 1, stride=1, stride_axis=0)[:, :t]


def _bias_tiles(rel_bias, t):
    n_heads = rel_bias.shape[1]
    buckets = _t5_bucket(2 * t - jnp.arange(4 * t, dtype=jnp.int32))[None, :]
    return pl.pallas_call(
        _bias_kernel,
        grid=(n_heads,),
        in_specs=[pl.BlockSpec(memory_space=pltpu.SMEM),
                  pl.BlockSpec((1, 4 * t), lambda hh: (0, 0))],
        out_specs=pl.BlockSpec((1, 5, t, t), lambda hh: (hh, 0, 0, 0)),
        out_shape=jax.ShapeDtypeStruct((n_heads, 5, t, t), F32),
        compiler_params=_cparams(("arbitrary",)),
        name="rel_bias_tiles",
    )(rel_bias, buckets)


ONES_ROWS = 16


def _softmax_state_step(s, offset, vt, state):
    m_new, alpha, p = _softmax_weights(s, offset, state[0])
    return m_new, _accumulate(alpha, state[1], vt, p)


def _softmax_weights(s, offset, m_old):
    m_new = jnp.maximum(m_old, jnp.max(s, axis=0, keepdims=True) + offset)
    alpha = jnp.exp2(m_old - m_new)
    p = jnp.exp2(s - (m_new - offset)).astype(BF16)
    return m_new, alpha, p


def _accumulate(alpha, acc_old, vt, p):
    return alpha * acc_old + jnp.dot(vt, p, preferred_element_type=F32)


def _init_state(dv, t):
    return (jnp.full((1, t), -jnp.inf, F32), jnp.zeros((dv + ONES_ROWS, t), F32))


def _with_ones(vt):
    return jnp.concatenate([vt, jnp.ones((ONES_ROWS, vt.shape[1]), vt.dtype)], axis=0)


def _normalised(state, dv):
    _, acc = state
    return acc[:dv] / acc[dv:dv + 1]


def _diff_attn_kernel(rb_ref, lamv_ref, gsub_ref, qt_ref, k_ref, vt_ref, bias_ref, o_ref, *,
                      lambda_init, bounded):
    hh = pl.program_id(0)
    t = bias_ref.shape[-1]
    nq = qt_ref.shape[-1] // t
    nkt = vt_ref.shape[2]

    off_left = rb_ref[REL_BUCKETS // 2 - 1, hh] * LOG2E
    off_right = rb_ref[REL_BUCKETS - 1, hh] * LOG2E
    lv = lamv_ref[...]
    lam = (jnp.exp(jnp.sum(lv[0:1] * lv[1:2], axis=-1, keepdims=True))
           - jnp.exp(jnp.sum(lv[2:3] * lv[3:4], axis=-1, keepdims=True)) + lambda_init)

    n_near = min(nkt, 3)
    row = lax.broadcasted_iota(jnp.int32, (LANES, t), 0)
    zero = jnp.zeros((LANES, t), BF16)

    def query_maps(j):
        qt = qt_ref[0, 0, :, j * t:(j + 1) * t]
        return jnp.where(row < DH_DIFF, qt, zero), jnp.where(row < DH_DIFF, zero, qt)

    def tile_index(j, r):
        return lax.rem(pl.program_id(2) * nq + j - 1 + r + nkt, nkt)

    def far_offset(j, r):
        return jnp.where(tile_index(j, r) < pl.program_id(2) * nq + j, off_left, off_right)

    def biased_scores(j, qmap, r, k0, ck):
        kj = tile_index(j, r)
        k_t = k_ref[0, pl.ds(pl.multiple_of(kj * t + k0, ck), ck), :]
        s = jnp.dot(k_t, qmap, preferred_element_type=F32)
        saturated = (r >= n_near or (nkt >= 3 and r == 0 and k0 + ck - 1 - t <= -REL_MAX_DIST)
                     or (nkt >= 3 and r == 2 and k0 + 1 >= REL_MAX_DIST))
        if not (bounded and saturated) and r < n_near:
            s = s + bias_ref[0, jnp.clip(kj - (pl.program_id(2) * nq + j), -2, 2) + 2, k0:k0 + ck, :]
        elif bounded:
            s = s + far_offset(j, r)
        return s

    def finish(j, outs):
        o = outs[0] - lam * outs[1]
        ms = jnp.mean(o * o, axis=0, keepdims=True)
        y = o * lax.rsqrt(ms + EPS) * (gsub_ref[...] * (1.0 - lambda_init))
        o_ref[0, j * t:(j + 1) * t, :] = y.T.astype(BF16)

    if bounded:
        ck = min(t, KEY_CHUNK)
        npt = t // ck
        qmaps = [query_maps(j) for j in range(nq)]
        items = [(j, r, c, mp) for j in range(nq) for r in range(nkt) for c in range(npt) for mp in range(2)]
        score = lambda it: biased_scores(it[0], qmaps[it[0]][it[3]], it[1], it[2] * ck, ck)
        depth = min(BOUNDED_DEPTH, len(items))
        pending = [score(it) for it in items[:depth]]
        for n, (j, r, c, mp) in enumerate(items):
            if (r, c, mp) == (0, 0, 0):
                accs = [jnp.zeros((DV_DIFF, t), F32) for _ in range(2)]
                dens = [jnp.zeros((SUBLANES, t), F32) for _ in range(2)]
            s = pending.pop(0)
            if n + depth < len(items):
                pending.append(score(items[n + depth]))
            p = jnp.exp2(s)
            dens[mp] = dens[mp] + jnp.sum(p.reshape(ck // SUBLANES, SUBLANES, t), axis=0)
            accs[mp] = accs[mp] + jnp.dot(vt_ref[0, 0, tile_index(j, r), :, c * ck:(c + 1) * ck],
                                          p.astype(BF16), preferred_element_type=F32)
            if (r, c, mp) == (nkt - 1, npt - 1, 1):
                finish(j, [acc / jnp.sum(den, axis=0, keepdims=True) for acc, den in zip(accs, dens)])
    else:
        for j in range(nq):
            qmap = query_maps(j)
            states = [_init_state(DV_DIFF, t), _init_state(DV_DIFF, t)]
            score = lambda r: [biased_scores(j, qmap[mp], r, 0, t) for mp in range(2)]
            s_next = score(0)
            for r in range(nkt):
                s_cur = s_next
                if r + 1 < nkt:
                    s_next = score(r + 1)
                vt = _with_ones(vt_ref[0, 0, tile_index(j, r)])
                offset = 0.0 if r < n_near else far_offset(j, r)
                for mp in range(2):
                    states[mp] = _softmax_state_step(s_cur[mp], offset, vt, states[mp])
            finish(j, [_normalised(st, DV_DIFF) for st in states])


def _diff_attn(rel_bias, lamv, gsub_col, qt, k, vt, bias, *, lambda_init, bounded):
    b, n_heads, _, s = qt.shape
    t = bias.shape[-1]
    nkt, tk = vt.shape[2], vt.shape[4]
    assert tk == t
    tq = _pick_tile(s, t * (DIFF_BOUNDED_Q_TILES if bounded else 1))
    return pl.pallas_call(
        functools.partial(_diff_attn_kernel, lambda_init=lambda_init, bounded=bounded),
        grid=(n_heads, b, s // tq),
        in_specs=[pl.BlockSpec(memory_space=pltpu.SMEM),
                  pl.BlockSpec(lamv.shape, lambda hh, bi, qi: (0, 0)),
                  pl.BlockSpec(gsub_col.shape, lambda hh, bi, qi: (0, 0)),
                  pl.BlockSpec((1, 1, LANES, tq), lambda hh, bi, qi: (bi, hh, 0, qi)),
                  pl.BlockSpec((1, s, LANES), lambda hh, bi, qi: (bi, 0, hh)),
                  pl.BlockSpec((1, 1, nkt, DV_DIFF, tk), lambda hh, bi, qi: (bi, hh, 0, 0, 0)),
                  pl.BlockSpec((1, 5, t, t), lambda hh, bi, qi: (hh, 0, 0, 0))],
        out_specs=pl.BlockSpec((1, tq, DV_DIFF), lambda hh, bi, qi: (bi, qi, hh)),
        out_shape=jax.ShapeDtypeStruct((b, s, n_heads * DV_DIFF), BF16),
        compiler_params=_cparams(("parallel", "parallel", "parallel")),
        name="diff_attn",
    )(rel_bias, lamv, gsub_col, qt, k, vt, bias)


def _mla_attn_kernel(qt_ref, k_ref, vt_ref, o_ref):
    nkt = vt_ref.shape[2]
    tk = vt_ref.shape[-1]
    qt = qt_ref[0, 0]
    m, acc = _init_state(V_MLA, qt.shape[-1])
    scores = lambda kj: jnp.dot(k_ref[0, 0, kj * tk:(kj + 1) * tk, :], qt, preferred_element_type=F32)
    depth = 2
    pending = [scores(kj) for kj in range(min(depth, nkt))]
    for kj in range(nkt):
        s = pending.pop(0)
        if kj + depth < nkt:
            pending.append(scores(kj + depth))
        m, alpha, p = _softmax_weights(s, 0.0, m)
        acc = _accumulate(alpha, acc, _with_ones(vt_ref[0, 0, kj]), p)
    o_ref[0] = _normalised((m, acc), V_MLA).T.astype(BF16)


def _mla_attn_bounded_kernel(qt_ref, k_ref, vt_ref, o_ref):
    nkt = vt_ref.shape[2]
    tk = vt_ref.shape[-1]
    t = min(qt_ref.shape[-1], MLA_TQ)
    nq = qt_ref.shape[-1] // t
    ck = min(tk, KEY_CHUNK)
    npt = tk // ck
    n_chunks = nkt * npt
    items = [(j, i) for j in range(nq) for i in range(n_chunks)]

    def scores(item):
        j, i = item
        return jnp.dot(k_ref[0, 0, i * ck:(i + 1) * ck, :], qt_ref[0, 0, :, j * t:(j + 1) * t],
                       preferred_element_type=F32)

    depth = min(BOUNDED_DEPTH, len(items))
    pending = [scores(it) for it in items[:depth]]
    for n, (j, i) in enumerate(items):
        if i == 0:
            acc = jnp.zeros((V_MLA, t), F32)
            den = jnp.zeros((SUBLANES, t), F32)
        s = pending.pop(0)
        if n + depth < len(items):
            pending.append(scores(items[n + depth]))
        c = i % npt
        p = jnp.exp2(s)
        den = den + jnp.sum(p.reshape(ck // SUBLANES, SUBLANES, t), axis=0)
        acc = acc + jnp.dot(vt_ref[0, 0, i // npt, :, c * ck:(c + 1) * ck], p.astype(BF16),
                            preferred_element_type=F32)
        if i == n_chunks - 1:
            o_ref[0, j * t:(j + 1) * t, :] = (acc / jnp.sum(den, axis=0, keepdims=True)).T.astype(BF16)


def _mla_attn(qt, k, vt, *, bounded):
    b, n_heads, _, s = qt.shape
    t = _pick_tile(s, MLA_TQ * (MLA_BOUNDED_Q_TILES if bounded else 1))
    nkt, tk = vt.shape[2], vt.shape[4]
    return pl.pallas_call(
        _mla_attn_bounded_kernel if bounded else _mla_attn_kernel,
        grid=(b, n_heads, s // t),
        in_specs=[pl.BlockSpec((1, 1, QK_HEAD, t), lambda bi, hh, qi: (bi, hh, 0, qi)),
                  pl.BlockSpec((1, 1, s, QK_HEAD), lambda bi, hh, qi: (bi, hh, 0, 0)),
                  pl.BlockSpec((1, 1, nkt, V_MLA, tk), lambda bi, hh, qi: (bi, hh, 0, 0, 0))],
        out_specs=pl.BlockSpec((1, t, V_MLA), lambda bi, hh, qi: (bi, qi, hh)),
        out_shape=jax.ShapeDtypeStruct((b, s, n_heads * V_MLA), BF16),
        compiler_params=_cparams(("parallel", "parallel", "parallel")),
        name="mla_attn",
    )(qt, k, vt)


def _out_proj_kernel(x_ref, gt_ref, a_ref, b_ref, wa_ref, wb_ref, sc_ref, sh_ref, g2_ref, o_ref, h_ref):
    tm = x_ref.shape[1]
    rows = tm // OUT_PROJ_SPLIT
    mixes = []
    for i in range(OUT_PROJ_SPLIT):
        sl = slice(i * rows, (i + 1) * rows)
        mixes.append(jnp.dot(a_ref[0, sl, :], wa_ref[...], preferred_element_type=F32)
                     + jnp.dot(b_ref[0, sl, :], wb_ref[...], preferred_element_type=F32))
    for i in range(OUT_PROJ_SPLIT):
        sl = slice(i * rows, (i + 1) * rows)
        x1 = x_ref[0, sl, :] + gt_ref[0] * mixes[i]
        o_ref[0, sl, :] = x1
        h_ref[0, sl, :] = _modulated_norm(x1, g2_ref[...], sc_ref[0], sh_ref[0]).astype(BF16)


def _out_proj(x, gt1, a, bb, wa, wb, sc2, sh2, g2, *, tm):
    b, s, d = x.shape
    row = lambda w: pl.BlockSpec((1, tm, w), lambda bi, i: (bi, i, 0))
    mod = pl.BlockSpec((1, 1, d), lambda bi, i: (bi, 0, 0))
    return pl.pallas_call(
        _out_proj_kernel,
        grid=(b, s // tm),
        in_specs=[row(d), mod, row(a.shape[-1]), row(bb.shape[-1]), _const_spec(wa.shape),
                  _const_spec(wb.shape), mod, mod, _const_spec(g2.shape)],
        out_specs=[row(d), row(d)],
        out_shape=[jax.ShapeDtypeStruct((b, s, d), F32), jax.ShapeDtypeStruct((b, s, d), BF16)],
        compiler_params=_cparams(("parallel", "parallel")),
        name="out_proj",
    )(x, gt1, a, bb, wa, wb, sc2, sh2, g2)


def _ffn_kernel(x_ref, h_ref, gt_ref, wg_ref, wu_ref, wd_ref, o_ref, xs_ref, *, n_x_chunks):
    j = pl.program_id(2)
    rows = x_ref.shape[1]

    @pl.when(j == 0)
    def _():
        o_ref[...] = jnp.zeros(o_ref.shape, F32)

    chunk = jnp.minimum(j, n_x_chunks - 1)
    xs_ref[pl.ds(pl.multiple_of(chunk * rows, rows), rows), :] = x_ref[0]

    h = h_ref[0]
    tf = wg_ref.shape[1]
    cf = tf // FFN_COL_SPLIT
    gates = []
    for c in range(FFN_COL_SPLIT):
        cs = slice(c * cf, (c + 1) * cf)
        gates.append((jnp.dot(h, wg_ref[:, cs], preferred_element_type=F32),
                      jnp.dot(h, wu_ref[:, cs], preferred_element_type=F32)))
    part = None
    for c, (g, u) in enumerate(gates):
        act = (g * jax.nn.sigmoid(g) * u).astype(BF16)
        d = jnp.dot(act, wd_ref[c * cf:(c + 1) * cf, :], preferred_element_type=F32)
        part = d if part is None else part + d
    o_ref[0] += part

    @pl.when(j == pl.num_programs(2) - 1)
    def _():
        o_ref[0] = xs_ref[...] + gt_ref[0] * o_ref[0]


def _ffn(x, h2, gt2, wg, wu, wd, *, tm, tf):
    b, s, d = x.shape
    dff = wg.shape[1]
    nj = dff // tf
    nxc = 1
    while nxc * 2 <= nj and (tm // (nxc * 2)) % SUBLANES == 0:
        nxc *= 2
    row = pl.BlockSpec((1, tm, d), lambda bi, i, j: (bi, i, 0))
    x_chunk = pl.BlockSpec((1, tm // nxc, d), lambda bi, i, j: (bi, i * nxc + jnp.minimum(j, nxc - 1), 0))
    return pl.pallas_call(
        functools.partial(_ffn_kernel, n_x_chunks=nxc),
        grid=(b, s // tm, nj),
        in_specs=[x_chunk, row, pl.BlockSpec((1, 1, d), lambda bi, i, j: (bi, 0, 0)),
                  pl.BlockSpec((d, tf), lambda bi, i, j: (0, j)),
                  pl.BlockSpec((d, tf), lambda bi, i, j: (0, j)),
                  pl.BlockSpec((tf, d), lambda bi, i, j: (j, 0))],
        out_specs=row,
        out_shape=jax.ShapeDtypeStruct((b, s, d), F32),
        scratch_shapes=[pltpu.VMEM((tm, d), F32)],
        compiler_params=_cparams(("parallel", "parallel", "arbitrary")),
        name="ffn",
    )(x, h2, gt2, wg, wu, wd)


def _rope_tables(s):
    pos = jnp.arange(s, dtype=jnp.float32)
    inv = 1.0 / (ROPE_THETA ** (jnp.arange(0, QK_ROPE, 2, dtype=jnp.float32) / QK_ROPE))
    ang = pos[:, None] * inv[None, :]
    return jnp.cos(ang), jnp.sin(ang)


def _pick_tile(n, pref):
    t = min(pref, n)
    while n % t:
        t //= 2
    return t


def kernel(x, c, rel_bias, w_ada, b_ada, g_norm1, w_in, g_q_diff, g_k_diff, lambda_vecs, g_subln, g_q_a, w_q_b, g_kv_a, w_kv_b, g_q_mla, g_k_mla, w_out, g_norm2, w_gate, w_up, w_down):
    b, s, d = x.shape
    depth = w_ada.shape[0]
    diff_width = d // 2
    n_hd = diff_width // DV_DIFF
    n_hm = (d - diff_width) // V_MLA
    assert n_hd == n_hm and n_hd % 2 == 0
    n_heads = n_hd
    wd = n_heads * LANES
    half = QK_ROPE // 2
    tk = _pick_tile(s, ATT_TK)
    nkt = s // tk
    t = tk
    assert t + 1 >= REL_MAX_DIST
    tm = _pick_tile(s, 512)

    cos, sin = _rope_tables(s)
    cosq, sinq = jnp.tile(cos, (1, n_heads)), jnp.tile(sin, (1, n_heads))
    cosk = jnp.tile(cos, (1, 4))
    sink = jnp.tile(jnp.concatenate([-sin, sin], axis=1), (1, 2))

    c_pad = jnp.pad(c, ((0, (-b) % 8), (0, 0)))

    for l in range(depth):
        lambda_init = 0.8 - 0.6 * math.exp(-0.3 * l)
        mod = _ada(c_pad, w_ada[l], b_ada[l][None, :])[:b]
        sh1, sc1, gt1, sh2, sc2, gt2 = [m[:, None, :] for m in jnp.split(mod, 6, axis=-1)]

        wi = w_in[l]
        o = 3 * wd + Q_LORA + KV_LORA
        w_kpe = wi[:, o:o + QK_ROPE]
        w_kpe_sw = jnp.concatenate([w_kpe[:, half:], w_kpe[:, :half]], axis=1)
        wqkv = wi.astype(BF16)
        wlat = jnp.concatenate([w_kpe, w_kpe, w_kpe_sw, w_kpe_sw], axis=1).astype(BF16)
        wq = w_q_b[l].reshape(Q_LORA, n_heads, QK_HEAD)
        wqb = jnp.concatenate([wq[:, :, :QK_NOPE].reshape(Q_LORA, -1),
                               wq[:, :, QK_NOPE:QK_NOPE + half].reshape(Q_LORA, -1),
                               wq[:, :, QK_NOPE + half:].reshape(Q_LORA, -1)], axis=1).astype(BF16)
        wkv = w_kv_b[l].reshape(KV_LORA, n_heads, QK_NOPE + V_MLA)
        wkvb = jnp.concatenate([wkv[:, :, :QK_NOPE].reshape(KV_LORA, -1),
                                wkv[:, :, QK_NOPE:].reshape(KV_LORA, -1)], axis=1).astype(BF16)
        gq, gk = g_q_mla[l], g_k_mla[l]
        gk_pe = gk[QK_NOPE:]
        gk_pe_sw = jnp.concatenate([gk_pe[half:], gk_pe[:half]])

        qt_d, kd, vt_d, qt_m, k_m, vt_m = _in_proj(
            x, sc1, sh1, g_norm1[l][None, :], wqkv, wlat,
            jnp.tile(g_q_diff[l], 2)[None, :], jnp.tile(g_k_diff[l], 2)[None, :],
            g_q_a[l][None, :], wqb, g_kv_a[l][None, :], wkvb,
            gq[None, :QK_NOPE], jnp.tile(gq[QK_NOPE:QK_NOPE + half], n_heads)[None, :],
            jnp.tile(gq[QK_NOPE + half:], n_heads)[None, :],
            gk[None, :QK_NOPE], jnp.tile(gk_pe, 2)[None, :], jnp.tile(gk_pe_sw, 2)[None, :],
            cosq, sinq, cosk, sink, n_heads=n_heads, tm=tk)

        bias = _bias_tiles(rel_bias, t)
        gmax = lambda g: jnp.max(jnp.abs(g))
        bound_d = (1.02 * LOG2E * DH_DIFF ** 0.5 * gmax(g_q_diff[l]) * gmax(g_k_diff[l])
                   + LOG2E * gmax(rel_bias))
        bound_m = 1.02 * LOG2E * QK_HEAD ** 0.5 * gmax(g_q_mla[l]) * gmax(g_k_mla[l])
        diff_args = (rel_bias, lambda_vecs[l], g_subln[l][:, None], qt_d, kd, vt_d, bias)
        a_out = lax.cond(
            bound_d <= SCORE_BOUND,
            lambda *a: _diff_attn(*a, lambda_init=lambda_init, bounded=True),
            lambda *a: _diff_attn(*a, lambda_init=lambda_init, bounded=False), *diff_args)
        b_out = lax.cond(
            bound_m <= SCORE_BOUND,
            lambda *a: _mla_attn(*a, bounded=True),
            lambda *a: _mla_attn(*a, bounded=False), qt_m, k_m, vt_m)

        wo = w_out[l].astype(BF16)
        x, h2 = _out_proj(x, gt1, a_out, b_out, wo[:diff_width], wo[diff_width:], sc2, sh2,
                          g_norm2[l][None, :], tm=tm)
        x = _ffn(x, h2, gt2, w_gate[l].astype(BF16), w_up[l].astype(BF16), w_down[l].astype(BF16),
                 tm=_pick_tile(s, FFN_TM), tf=_pick_tile(w_gate.shape[-1], FFN_TF))
    return x
```

```python
import functools
import math

import jax
import jax.numpy as jnp
from jax import lax
from jax.experimental import pallas as pl
from jax.experimental.pallas import tpu as pltpu

F32 = jnp.float32
BF16 = jnp.bfloat16

DH_DIFF = 64
DV_DIFF = 2 * DH_DIFF
QK_NOPE = 128
QK_ROPE = 64
QK_HEAD = QK_NOPE + QK_ROPE
V_MLA = 128
Q_LORA = 512
KV_LORA = 256
ROPE_THETA = 10000.0
REL_BUCKETS = 32
REL_MAX_DIST = 128
EPS = 1e-6
LOG2E = 1.4426950408889634

LANES = 128
SUBLANES = 8
VMEM_LIMIT_BYTES = 56 * 1024 * 1024

OUT_PROJ_SPLIT = 4
IN_PROJ_SPLIT = 2
FFN_TM = 1024
FFN_TF = 512
FFN_COL_SPLIT = 2
ATT_TK = 512
DIFF_TQ = 512
MLA_TQ = 512
KEY_CHUNK = 256
BOUNDED_DEPTH = 2
DIFF_BOUNDED_Q_TILES = 8
MLA_BOUNDED_Q_TILES = 8
SCORE_BOUND = 40.0


def _cparams(sem):
    return pltpu.CompilerParams(dimension_semantics=sem, vmem_limit_bytes=VMEM_LIMIT_BYTES)


def _const_spec(shape):
    nd = len(shape)
    return pl.BlockSpec(shape, lambda *_: (0,) * nd, pipeline_mode=pl.Buffered(1))


def _ada_kernel(c_ref, w_ref, b_ref, o_ref):
    c = c_ref[...]
    ca = c * jax.nn.sigmoid(c)
    o_ref[...] = jnp.dot(ca.astype(BF16), w_ref[...].astype(BF16), preferred_element_type=F32) + b_ref[...]


def _ada(c_pad, w, b, tn=1024):
    m, d = c_pad.shape
    n = w.shape[1]
    return pl.pallas_call(
        _ada_kernel,
        grid=(n // tn,),
        in_specs=[pl.BlockSpec((m, d), lambda j: (0, 0)),
                  pl.BlockSpec((d, tn), lambda j: (0, j)),
                  pl.BlockSpec((1, tn), lambda j: (0, j))],
        out_specs=pl.BlockSpec((m, tn), lambda j: (0, j)),
        out_shape=jax.ShapeDtypeStruct((m, n), F32),
        compiler_params=_cparams(("arbitrary",)),
        name="ada",
    )(c_pad, w, b)


def _modulated_norm(x, g, sc, sh):
    ms = jnp.mean(x * x, axis=-1, keepdims=True)
    return (x * lax.rsqrt(ms + EPS) * g) * (1.0 + sc) + sh


def _half_lane_norm(blk, g2, lo_mask, out_scale):
    sq = blk * blk
    s_lo = jnp.sum(jnp.where(lo_mask, sq, 0.0), axis=-1, keepdims=True)
    s_hi = jnp.sum(jnp.where(lo_mask, 0.0, sq), axis=-1, keepdims=True)
    inv = jnp.where(lo_mask, lax.rsqrt(s_lo * (1.0 / DH_DIFF) + EPS),
                    lax.rsqrt(s_hi * (1.0 / DH_DIFF) + EPS))
    return blk * inv * (g2 * out_scale)


def _in_proj_kernel(x_ref, sc_ref, sh_ref, g1_ref, wqkv_ref, wlat_ref, gqd_ref, gkd_ref,
                    gqa_ref, wqb_ref, gkva_ref, wkvb_ref, gqn_ref, gq1_ref, gq2_ref,
                    gkn_ref, gkp_ref, gkps_ref, cosq_ref, sinq_ref, cosk_ref, sink_ref,
                    qtd_ref, kd_ref, vtd_ref, qtm_ref, km_ref, vtm_ref, *, n_heads):
    tm = x_ref.shape[1]
    rows = tm // IN_PROJ_SPLIT
    wd = n_heads * LANES
    half = QK_ROPE // 2
    wx = n_heads * half
    lo_mask = lax.broadcasted_iota(jnp.int32, (rows, LANES), 1) < DH_DIFF
    lane_x = lax.broadcasted_iota(jnp.int32, (rows, wx), 1) // half
    qscale = DH_DIFF ** -0.5 * LOG2E
    mscale = QK_HEAD ** -0.5 * LOG2E

    def rms(v, g):
        return v * lax.rsqrt(jnp.mean(v * v, axis=-1, keepdims=True) + EPS) * g

    for sb in range(IN_PROJ_SPLIT):
        rs = slice(sb * rows, (sb + 1) * rows)
        h = _modulated_norm(x_ref[0, rs, :], g1_ref[...], sc_ref[0], sh_ref[0]).astype(BF16)
        lat = jnp.dot(h, wqkv_ref[:, 3 * wd:3 * wd + Q_LORA + KV_LORA], preferred_element_type=F32)
        cq = lat[:, 0:Q_LORA]
        ckv = lat[:, Q_LORA:Q_LORA + KV_LORA]
        kpe4 = jnp.dot(h, wlat_ref[...], preferred_element_type=F32)
        kpe2 = kpe4[:, 0:LANES]
        kpes2 = kpe4[:, LANES:2 * LANES]

        kv = jnp.dot(rms(ckv, gkva_ref[...]).astype(BF16), wkvb_ref[...], preferred_element_type=F32)
        for hh in range(n_heads):
            vtm_ref[0, hh, 0, :, rs] = kv[:, wd + hh * LANES:wd + (hh + 1) * LANES].T.astype(BF16)
        ss_pe = jnp.sum(jnp.where(lo_mask, kpe2 * kpe2, 0.0), axis=-1, keepdims=True)
        kr2 = kpe2 * gkp_ref[...] * cosk_ref[rs, :] + kpes2 * gkps_ref[...] * sink_ref[rs, :]
        for hh in range(n_heads):
            sl = slice(hh * LANES, (hh + 1) * LANES)
            kn = kv[:, sl]
            ss = jnp.sum(kn * kn, axis=-1, keepdims=True) + ss_pe
            r = lax.rsqrt(ss * (1.0 / QK_HEAD) + EPS)
            km_ref[0, hh, rs, 0:QK_NOPE] = (kn * r * gkn_ref[...]).astype(BF16)
            km_ref[0, hh, rs, QK_NOPE:QK_HEAD] = (kr2 * r)[:, 0:QK_ROPE].astype(BF16)

        qm = jnp.dot(rms(cq, gqa_ref[...]).astype(BF16), wqb_ref[...], preferred_element_type=F32)
        x1 = qm[:, wd:wd + wx]
        x2 = qm[:, wd + wx:wd + 2 * wx]
        sq_x = x1 * x1 + x2 * x2
        inv_x = jnp.zeros((rows, wx), F32)
        for hh in range(n_heads):
            sl = slice(hh * LANES, (hh + 1) * LANES)
            qn = qm[:, sl]
            ss = (jnp.sum(qn * qn, axis=-1, keepdims=True)
                  + jnp.sum(jnp.where(lane_x == hh, sq_x, 0.0), axis=-1, keepdims=True))
            r = lax.rsqrt(ss * (1.0 / QK_HEAD) + EPS) * mscale
            qtm_ref[0, hh, 0:QK_NOPE, rs] = (qn * r * gqn_ref[...]).T.astype(BF16)
            inv_x = jnp.where(lane_x == hh, r, inv_x)
        a1 = x1 * gq1_ref[...]
        a2 = x2 * gq2_ref[...]
        cq_t = cosq_ref[rs, :]
        sq_t = sinq_ref[rs, :]
        x1t = ((a1 * cq_t - a2 * sq_t) * inv_x).T.astype(BF16)
        x2t = ((a2 * cq_t + a1 * sq_t) * inv_x).T.astype(BF16)
        for hh in range(n_heads):
            qtm_ref[0, hh, QK_NOPE:QK_NOPE + half, rs] = x1t[hh * half:(hh + 1) * half]
            qtm_ref[0, hh, QK_NOPE + half:QK_HEAD, rs] = x2t[hh * half:(hh + 1) * half]

        vd = jnp.dot(h, wqkv_ref[:, 2 * wd:3 * wd], preferred_element_type=F32)
        for hh in range(n_heads):
            vtd_ref[0, hh, 0, :, rs] = vd[:, hh * LANES:(hh + 1) * LANES].T.astype(BF16)
        qd = jnp.dot(h, wqkv_ref[:, 0:wd], preferred_element_type=F32)
        for hh in range(n_heads):
            sl = slice(hh * LANES, (hh + 1) * LANES)
            qtd_ref[0, hh, :, rs] = _half_lane_norm(qd[:, sl], gqd_ref[...], lo_mask, qscale).T.astype(BF16)
        kd = jnp.dot(h, wqkv_ref[:, wd:2 * wd], preferred_element_type=F32)
        for hh in range(n_heads):
            sl = slice(hh * LANES, (hh + 1) * LANES)
            kd_ref[0, rs, sl] = _half_lane_norm(kd[:, sl], gkd_ref[...], lo_mask, 1.0).astype(BF16)


def _in_proj(x, sc1, sh1, g1, wqkv, wlat, gqd, gkd, gqa, wqb, gkva, wkvb, gqn, gq1, gq2,
             gkn, gkp, gkps, cosq, sinq, cosk, sink, *, n_heads, tm):
    b, s, d = x.shape
    wd = n_heads * LANES
    wx = n_heads * (QK_ROPE // 2)
    row = lambda w: pl.BlockSpec((1, tm, w), lambda bi, i: (bi, i, 0))
    mod = pl.BlockSpec((1, 1, d), lambda bi, i: (bi, 0, 0))
    tab = lambda w: pl.BlockSpec((tm, w), lambda bi, i: (i, 0))
    consts = [g1, wqkv, wlat, gqd, gkd, gqa, wqb, gkva, wkvb, gqn, gq1, gq2, gkn, gkp, gkps]
    return pl.pallas_call(
        functools.partial(_in_proj_kernel, n_heads=n_heads),
        grid=(b, s // tm),
        in_specs=[row(d), mod, mod] + [_const_spec(a.shape) for a in consts]
                 + [tab(wx), tab(wx), tab(LANES), tab(LANES)],
        out_specs=[pl.BlockSpec((1, n_heads, LANES, tm), lambda bi, i: (bi, 0, 0, i)),
                   row(wd),
                   pl.BlockSpec((1, n_heads, 1, DV_DIFF, tm), lambda bi, i: (bi, 0, i, 0, 0)),
                   pl.BlockSpec((1, n_heads, QK_HEAD, tm), lambda bi, i: (bi, 0, 0, i)),
                   pl.BlockSpec((1, n_heads, tm, QK_HEAD), lambda bi, i: (bi, 0, i, 0)),
                   pl.BlockSpec((1, n_heads, 1, V_MLA, tm), lambda bi, i: (bi, 0, i, 0, 0))],
        out_shape=[jax.ShapeDtypeStruct((b, n_heads, LANES, s), BF16),
                   jax.ShapeDtypeStruct((b, s, wd), BF16),
                   jax.ShapeDtypeStruct((b, n_heads, s // tm, DV_DIFF, tm), BF16),
                   jax.ShapeDtypeStruct((b, n_heads, QK_HEAD, s), BF16),
                   jax.ShapeDtypeStruct((b, n_heads, s, QK_HEAD), BF16),
                   jax.ShapeDtypeStruct((b, n_heads, s // tm, V_MLA, tm), BF16)],
        compiler_params=_cparams(("parallel", "parallel")),
        name="in_proj",
    )(x, sc1, sh1, *consts, cosq, sinq, cosk, sink)


def _t5_bucket(rel):
    nb = REL_BUCKETS // 2
    max_exact = nb // 2
    base = jnp.where(rel > 0, nb, 0)
    n = jnp.abs(rel)
    nf = jnp.maximum(n, 1).astype(jnp.float32)
    large = max_exact + (jnp.log(nf / max_exact) / math.log(REL_MAX_DIST / max_exact)
                         * (nb - max_exact)).astype(jnp.int32)
    large = jnp.minimum(large, nb - 1)
    return base + jnp.where(n < max_exact, n, large)


def _bias_kernel(rb_ref, bucket_ref, o_ref):
    hh = pl.program_id(0)
    t = o_ref.shape[-1]
    far_left = rb_ref[REL_BUCKETS // 2 - 1, hh]
    far_right = rb_ref[REL_BUCKETS - 1, hh]
    o_ref[0, 0] = jnp.full((t, t), far_left, F32) * LOG2E
    o_ref[0, 4] = jnp.full((t, t), far_right, F32) * LOG2E
    bk_strip = bucket_ref[...]
    strip = jnp.zeros(bk_strip.shape, F32)
    for bk in range(REL_BUCKETS):
        strip = jnp.where(bk_strip == bk, rb_ref[bk, hh], strip)
    strip = strip * LOG2E
    for d in (-1, 0, 1):
        diag = jnp.concatenate([strip[:, (2 - d) * t:(3 - d) * t], strip[:, (1 - d) * t:(2 - d) * t]], axis=1)
        rows = jnp.broadcast_to(diag, (t, 2 * t))
        o_ref[0, d + 2] = pltpu.roll(rows, 0, 1, stride=1, stride_axis=0)[:, :t]


def _bias_tiles(rel_bias, t):
    n_heads = rel_bias.shape[1]
    buckets = _t5_bucket(2 * t - jnp.arange(4 * t, dtype=jnp.int32))[None, :]
    return pl.pallas_call(
        _bias_kernel,
        grid=(n_heads,),
        in_specs=[pl.BlockSpec(memory_space=pltpu.SMEM),
                  pl.BlockSpec((1, 4 * t), lambda hh: (0, 0))],
        out_specs=pl.BlockSpec((1, 5, t, t), lambda hh: (hh, 0, 0, 0)),
        out_shape=jax.ShapeDtypeStruct((n_heads, 5, t, t), F32),
        compiler_params=_cparams(("arbitrary",)),
        name="rel_bias_tiles",
    )(rel_bias, buckets)


ONES_ROWS = 16


def _softmax_state_step(s, offset, vt, state):
    m_new, alpha, p = _softmax_weights(s, offset, state[0])
    return m_new, _accumulate(alpha, state[1], vt, p)


def _softmax_weights(s, offset, m_old):
    m_new = jnp.maximum(m_old, jnp.max(s, axis=0, keepdims=True) + offset)
    alpha = jnp.exp2(m_old - m_new)
    p = jnp.exp2(s - (m_new - offset)).astype(BF16)
    return m_new, alpha, p


def _accumulate(alpha, acc_old, vt, p):
    return alpha * acc_old + jnp.dot(vt, p, preferred_element_type=F32)


def _init_state(dv, t):
    return (jnp.full((1, t), -jnp.inf, F32), jnp.zeros((dv + ONES_ROWS, t), F32))


def _with_ones(vt):
    return jnp.concatenate([vt, jnp.ones((ONES_ROWS, vt.shape[1]), vt.dtype)], axis=0)


def _normalised(state, dv):
    _, acc = state
    return acc[:dv] / acc[dv:dv + 1]


def _diff_attn_kernel(rb_ref, lamv_ref, gsub_ref, qt_ref, k_ref, vt_ref, bias_ref, o_ref, *,
                      lambda_init, bounded, q_steps):
    hh = pl.program_id(0)
    t = bias_ref.shape[-1]
    nq = qt_ref.shape[-1] // t
    nkt = vt_ref.shape[2]

    off_left = rb_ref[REL_BUCKETS // 2 - 1, hh] * LOG2E
    off_right = rb_ref[REL_BUCKETS - 1, hh] * LOG2E
    lv = lamv_ref[...]
    lam = (jnp.exp(jnp.sum(lv[0:1] * lv[1:2], axis=-1, keepdims=True))
           - jnp.exp(jnp.sum(lv[2:3] * lv[3:4], axis=-1, keepdims=True)) + lambda_init)

    n_near = min(nkt, 3)
    row = lax.broadcasted_iota(jnp.int32, (LANES, t), 0)
    zero = jnp.zeros((LANES, t), BF16)

    def query_maps(j):
        qt = qt_ref[0, 0, :, j * t:(j + 1) * t]
        return jnp.where(row < DH_DIFF, qt, zero), jnp.where(row < DH_DIFF, zero, qt)

    static_q = q_steps == 1

    def query_tile(j):
        return j if static_q else pl.program_id(2) * nq + j

    def tile_index(j, r):
        v = query_tile(j) - 1 + r + nkt
        return v % nkt if static_q else lax.rem(v, nkt)

    def far_offset(j, r):
        if static_q:
            return off_left if tile_index(j, r) < j else off_right
        return jnp.where(tile_index(j, r) < query_tile(j), off_left, off_right)

    def bias_tile_index(j, kj):
        d = kj - query_tile(j)
        return (min(max(d, -2), 2) if static_q else jnp.clip(d, -2, 2)) + 2

    def biased_scores(j, qmap, r, k0, ck):
        kj = tile_index(j, r)
        start = kj * t + k0
        k_t = k_ref[0, pl.ds(start if static_q else pl.multiple_of(start, ck), ck), :]
        s = jnp.dot(k_t, qmap, preferred_element_type=F32)
        saturated = (r >= n_near or (nkt >= 3 and r == 0 and k0 + ck - 1 - t <= -REL_MAX_DIST)
                     or (nkt >= 3 and r == 2 and k0 + 1 >= REL_MAX_DIST))
        if not (bounded and saturated) and r < n_near:
            s = s + bias_ref[0, bias_tile_index(j, kj), k0:k0 + ck, :]
        elif bounded:
            s = s + far_offset(j, r)
        return s

    def finish(j, outs):
        o = outs[0] - lam * outs[1]
        ms = jnp.mean(o * o, axis=0, keepdims=True)
        y = o * lax.rsqrt(ms + EPS) * (gsub_ref[...] * (1.0 - lambda_init))
        o_ref[0, j * t:(j + 1) * t, :] = y.T.astype(BF16)

    if bounded:
        ck = min(t, KEY_CHUNK)
        npt = t // ck
        qmaps = [query_maps(j) for j in range(nq)]
        items = [(j, r, c, mp) for j in range(nq) for r in range(nkt) for c in range(npt) for mp in range(2)]
        score = lambda it: biased_scores(it[0], qmaps[it[0]][it[3]], it[1], it[2] * ck, ck)
        depth = min(BOUNDED_DEPTH, len(items))
        pending = [score(it) for it in items[:depth]]
        for n, (j, r, c, mp) in enumerate(items):
            if (r, c, mp) == (0, 0, 0):
                accs = [jnp.zeros((DV_DIFF, t), F32) for _ in range(2)]
                dens = [jnp.zeros((SUBLANES, t), F32) for _ in range(2)]
            s = pending.pop(0)
            if n + depth < len(items):
                pending.append(score(items[n + depth]))
            p = jnp.exp2(s)
            dens[mp] = dens[mp] + jnp.sum(p.reshape(ck // SUBLANES, SUBLANES, t), axis=0)
            accs[mp] = accs[mp] + jnp.dot(vt_ref[0, 0, tile_index(j, r), :, c * ck:(c + 1) * ck],
                                          p.astype(BF16), preferred_element_type=F32)
            if (r, c, mp) == (nkt - 1, npt - 1, 1):
                finish(j, [acc / jnp.sum(den, axis=0, keepdims=True) for acc, den in zip(accs, dens)])
    else:
        for j in range(nq):
            qmap = query_maps(j)
            states = [_init_state(DV_DIFF, t), _init_state(DV_DIFF, t)]
            score = lambda r: [biased_scores(j, qmap[mp], r, 0, t) for mp in range(2)]
            s_next = score(0)
            for r in range(nkt):
                s_cur = s_next
                if r + 1 < nkt:
                    s_next = score(r + 1)
                vt = _with_ones(vt_ref[0, 0, tile_index(j, r)])
                offset = 0.0 if r < n_near else far_offset(j, r)
                for mp in range(2):
                    states[mp] = _softmax_state_step(s_cur[mp], offset, vt, states[mp])
            finish(j, [_normalised(st, DV_DIFF) for st in states])


def _diff_attn(rel_bias, lamv, gsub_col, qt, k, vt, bias, *, lambda_init, bounded):
    b, n_heads, _, s = qt.shape
    t = bias.shape[-1]
    nkt, tk = vt.shape[2], vt.shape[4]
    assert tk == t
    tq = _pick_tile(s, t * (DIFF_BOUNDED_Q_TILES if bounded else 1))
    return pl.pallas_call(
        functools.partial(_diff_attn_kernel, lambda_init=lambda_init, bounded=bounded, q_steps=s // tq),
        grid=(n_heads, b, s // tq),
        in_specs=[pl.BlockSpec(memory_space=pltpu.SMEM),
                  pl.BlockSpec(lamv.shape, lambda hh, bi, qi: (0, 0)),
                  pl.BlockSpec(gsub_col.shape, lambda hh, bi, qi: (0, 0)),
                  pl.BlockSpec((1, 1, LANES, tq), lambda hh, bi, qi: (bi, hh, 0, qi)),
                  pl.BlockSpec((1, s, LANES), lambda hh, bi, qi: (bi, 0, hh)),
                  pl.BlockSpec((1, 1, nkt, DV_DIFF, tk), lambda hh, bi, qi: (bi, hh, 0, 0, 0)),
                  pl.BlockSpec((1, 5, t, t), lambda hh, bi, qi: (hh, 0, 0, 0))],
        out_specs=pl.BlockSpec((1, tq, DV_DIFF), lambda hh, bi, qi: (bi, qi, hh)),
        out_shape=jax.ShapeDtypeStruct((b, s, n_heads * DV_DIFF), BF16),
        compiler_params=_cparams(("parallel", "parallel", "parallel")),
        name="diff_attn",
    )(rel_bias, lamv, gsub_col, qt, k, vt, bias)


def _mla_attn_kernel(qt_ref, k_ref, vt_ref, o_ref):
    nkt = vt_ref.shape[2]
    tk = vt_ref.shape[-1]
    qt = qt_ref[0, 0]
    m, acc = _init_state(V_MLA, qt.shape[-1])
    scores = lambda kj: jnp.dot(k_ref[0, 0, kj * tk:(kj + 1) * tk, :], qt, preferred_element_type=F32)
    depth = 2
    pending = [scores(kj) for kj in range(min(depth, nkt))]
    for kj in range(nkt):
        s = pending.pop(0)
        if kj + depth < nkt:
            pending.append(scores(kj + depth))
        m, alpha, p = _softmax_weights(s, 0.0, m)
        acc = _accumulate(alpha, acc, _with_ones(vt_ref[0, 0, kj]), p)
    o_ref[0] = _normalised((m, acc), V_MLA).T.astype(BF16)


def _mla_attn_bounded_kernel(qt_ref, k_ref, vt_ref, o_ref):
    nkt = vt_ref.shape[2]
    tk = vt_ref.shape[-1]
    t = min(qt_ref.shape[-1], MLA_TQ)
    nq = qt_ref.shape[-1] // t
    ck = min(tk, KEY_CHUNK)
    npt = tk // ck
    n_chunks = nkt * npt
    items = [(j, i) for j in range(nq) for i in range(n_chunks)]

    def scores(item):
        j, i = item
        return jnp.dot(k_ref[0, 0, i * ck:(i + 1) * ck, :], qt_ref[0, 0, :, j * t:(j + 1) * t],
                       preferred_element_type=F32)

    depth = min(BOUNDED_DEPTH, len(items))
    pending = [scores(it) for it in items[:depth]]
    for n, (j, i) in enumerate(items):
        if i == 0:
            acc = jnp.zeros((V_MLA, t), F32)
            den = jnp.zeros((SUBLANES, t), F32)
        s = pending.pop(0)
        if n + depth < len(items):
            pending.append(scores(items[n + depth]))
        c = i % npt
        p = jnp.exp2(s)
        den = den + jnp.sum(p.reshape(ck // SUBLANES, SUBLANES, t), axis=0)
        acc = acc + jnp.dot(vt_ref[0, 0, i // npt, :, c * ck:(c + 1) * ck], p.astype(BF16),
                            preferred_element_type=F32)
        if i == n_chunks - 1:
            o_ref[0, j * t:(j + 1) * t, :] = (acc / jnp.sum(den, axis=0, keepdims=True)).T.astype(BF16)


def _mla_attn(qt, k, vt, *, bounded):
    b, n_heads, _, s = qt.shape
    t = _pick_tile(s, MLA_TQ * (MLA_BOUNDED_Q_TILES if bounded else 1))
    nkt, tk = vt.shape[2], vt.shape[4]
    return pl.pallas_call(
        _mla_attn_bounded_kernel if bounded else _mla_attn_kernel,
        grid=(b, n_heads, s // t),
        in_specs=[pl.BlockSpec((1, 1, QK_HEAD, t), lambda bi, hh, qi: (bi, hh, 0, qi)),
                  pl.BlockSpec((1, 1, s, QK_HEAD), lambda bi, hh, qi: (bi, hh, 0, 0)),
                  pl.BlockSpec((1, 1, nkt, V_MLA, tk), lambda bi, hh, qi: (bi, hh, 0, 0, 0))],
        out_specs=pl.BlockSpec((1, t, V_MLA), lambda bi, hh, qi: (bi, qi, hh)),
        out_shape=jax.ShapeDtypeStruct((b, s, n_heads * V_MLA), BF16),
        compiler_params=_cparams(("parallel", "parallel", "parallel")),
        name="mla_attn",
    )(qt, k, vt)


def _out_proj_kernel(x_ref, gt_ref, a_ref, b_ref, wa_ref, wb_ref, sc_ref, sh_ref, g2_ref, o_ref, h_ref):
    tm = x_ref.shape[1]
    rows = tm // OUT_PROJ_SPLIT
    mixes = []
    for i in range(OUT_PROJ_SPLIT):
        sl = slice(i * rows, (i + 1) * rows)
        mixes.append(jnp.dot(a_ref[0, sl, :], wa_ref[...], preferred_element_type=F32)
                     + jnp.dot(b_ref[0, sl, :], wb_ref[...], preferred_element_type=F32))
    for i in range(OUT_PROJ_SPLIT):
        sl = slice(i * rows, (i + 1) * rows)
        x1 = x_ref[0, sl, :] + gt_ref[0] * mixes[i]
        o_ref[0, sl, :] = x1
        h_ref[0, sl, :] = _modulated_norm(x1, g2_ref[...], sc_ref[0], sh_ref[0]).astype(BF16)


def _out_proj(x, gt1, a, bb, wa, wb, sc2, sh2, g2, *, tm):
    b, s, d = x.shape
    row = lambda w: pl.BlockSpec((1, tm, w), lambda bi, i: (bi, i, 0))
    mod = pl.BlockSpec((1, 1, d), lambda bi, i: (bi, 0, 0))
    return pl.pallas_call(
        _out_proj_kernel,
        grid=(b, s // tm),
        in_specs=[row(d), mod, row(a.shape[-1]), row(bb.shape[-1]), _const_spec(wa.shape),
                  _const_spec(wb.shape), mod, mod, _const_spec(g2.shape)],
        out_specs=[row(d), row(d)],
        out_shape=[jax.ShapeDtypeStruct((b, s, d), F32), jax.ShapeDtypeStruct((b, s, d), BF16)],
        compiler_params=_cparams(("parallel", "parallel")),
        name="out_proj",
    )(x, gt1, a, bb, wa, wb, sc2, sh2, g2)


def _ffn_kernel(x_ref, h_ref, gt_ref, wg_ref, wu_ref, wd_ref, o_ref, xs_ref, *, n_x_chunks):
    j = pl.program_id(2)
    rows = x_ref.shape[1]

    @pl.when(j == 0)
    def _():
        o_ref[...] = jnp.zeros(o_ref.shape, F32)

    chunk = jnp.minimum(j, n_x_chunks - 1)
    xs_ref[pl.ds(pl.multiple_of(chunk * rows, rows), rows), :] = x_ref[0]

    h = h_ref[0]
    tf = wg_ref.shape[1]
    cf = tf // FFN_COL_SPLIT
    gates = []
    for c in range(FFN_COL_SPLIT):
        cs = slice(c * cf, (c + 1) * cf)
        gates.append((jnp.dot(h, wg_ref[:, cs], preferred_element_type=F32),
                      jnp.dot(h, wu_ref[:, cs], preferred_element_type=F32)))
    part = None
    for c, (g, u) in enumerate(gates):
        act = (g * jax.nn.sigmoid(g) * u).astype(BF16)
        d = jnp.dot(act, wd_ref[c * cf:(c + 1) * cf, :], preferred_element_type=F32)
        part = d if part is None else part + d
    o_ref[0] += part

    @pl.when(j == pl.num_programs(2) - 1)
    def _():
        o_ref[0] = xs_ref[...] + gt_ref[0] * o_ref[0]


def _ffn(x, h2, gt2, wg, wu, wd, *, tm, tf):
    b, s, d = x.shape
    dff = wg.shape[1]
    nj = dff // tf
    nxc = 1
    while nxc * 2 <= nj and (tm // (nxc * 2)) % SUBLANES == 0:
        nxc *= 2
    row = pl.BlockSpec((1, tm, d), lambda bi, i, j: (bi, i, 0))
    x_chunk = pl.BlockSpec((1, tm // nxc, d), lambda bi, i, j: (bi, i * nxc + jnp.minimum(j, nxc - 1), 0))
    return pl.pallas_call(
        functools.partial(_ffn_kernel, n_x_chunks=nxc),
        grid=(b, s // tm, nj),
        in_specs=[x_chunk, row, pl.BlockSpec((1, 1, d), lambda bi, i, j: (bi, 0, 0)),
                  pl.BlockSpec((d, tf), lambda bi, i, j: (0, j)),
                  pl.BlockSpec((d, tf), lambda bi, i, j: (0, j)),
                  pl.BlockSpec((tf, d), lambda bi, i, j: (j, 0))],
        out_specs=row,
        out_shape=jax.ShapeDtypeStruct((b, s, d), F32),
        scratch_shapes=[pltpu.VMEM((tm, d), F32)],
        compiler_params=_cparams(("parallel", "parallel", "arbitrary")),
        name="ffn",
    )(x, h2, gt2, wg, wu, wd)


def _rope_tables(s):
    pos = jnp.arange(s, dtype=jnp.float32)
    inv = 1.0 / (ROPE_THETA ** (jnp.arange(0, QK_ROPE, 2, dtype=jnp.float32) / QK_ROPE))
    ang = pos[:, None] * inv[None, :]
    return jnp.cos(ang), jnp.sin(ang)


def _pick_tile(n, pref):
    t = min(pref, n)
    while n % t:
        t //= 2
    return t


def kernel(x, c, rel_bias, w_ada, b_ada, g_norm1, w_in, g_q_diff, g_k_diff, lambda_vecs, g_subln, g_q_a, w_q_b, g_kv_a, w_kv_b, g_q_mla, g_k_mla, w_out, g_norm2, w_gate, w_up, w_down):
    b, s, d = x.shape
    depth = w_ada.shape[0]
    diff_width = d // 2
    n_hd = diff_width // DV_DIFF
    n_hm = (d - diff_width) // V_MLA
    assert n_hd == n_hm and n_hd % 2 == 0
    n_heads = n_hd
    wd = n_heads * LANES
    half = QK_ROPE // 2
    tk = _pick_tile(s, ATT_TK)
    nkt = s // tk
    t = tk
    assert t + 1 >= REL_MAX_DIST
    tm = _pick_tile(s, 512)

    cos, sin = _rope_tables(s)
    cosq, sinq = jnp.tile(cos, (1, n_heads)), jnp.tile(sin, (1, n_heads))
    cosk = jnp.tile(cos, (1, 4))
    sink = jnp.tile(jnp.concatenate([-sin, sin], axis=1), (1, 2))

    c_pad = jnp.pad(c, ((0, (-b) % 8), (0, 0)))

    for l in range(depth):
        lambda_init = 0.8 - 0.6 * math.exp(-0.3 * l)
        mod = _ada(c_pad, w_ada[l], b_ada[l][None, :])[:b]
        sh1, sc1, gt1, sh2, sc2, gt2 = [m[:, None, :] for m in jnp.split(mod, 6, axis=-1)]

        wi = w_in[l]
        o = 3 * wd + Q_LORA + KV_LORA
        w_kpe = wi[:, o:o + QK_ROPE]
        w_kpe_sw = jnp.concatenate([w_kpe[:, half:], w_kpe[:, :half]], axis=1)
        wqkv = wi.astype(BF16)
        wlat = jnp.concatenate([w_kpe, w_kpe, w_kpe_sw, w_kpe_sw], axis=1).astype(BF16)
        wq = w_q_b[l].reshape(Q_LORA, n_heads, QK_HEAD)
        wqb = jnp.concatenate([wq[:, :, :QK_NOPE].reshape(Q_LORA, -1),
                               wq[:, :, QK_NOPE:QK_NOPE + half].reshape(Q_LORA, -1),
                               wq[:, :, QK_NOPE + half:].reshape(Q_LORA, -1)], axis=1).astype(BF16)
        wkv = w_kv_b[l].reshape(KV_LORA, n_heads, QK_NOPE + V_MLA)
        wkvb = jnp.concatenate([wkv[:, :, :QK_NOPE].reshape(KV_LORA, -1),
                                wkv[:, :, QK_NOPE:].reshape(KV_LORA, -1)], axis=1).astype(BF16)
        gq, gk = g_q_mla[l], g_k_mla[l]
        gk_pe = gk[QK_NOPE:]
        gk_pe_sw = jnp.concatenate([gk_pe[half:], gk_pe[:half]])

        qt_d, kd, vt_d, qt_m, k_m, vt_m = _in_proj(
            x, sc1, sh1, g_norm1[l][None, :], wqkv, wlat,
            jnp.tile(g_q_diff[l], 2)[None, :], jnp.tile(g_k_diff[l], 2)[None, :],
            g_q_a[l][None, :], wqb, g_kv_a[l][None, :], wkvb,
            gq[None, :QK_NOPE], jnp.tile(gq[QK_NOPE:QK_NOPE + half], n_heads)[None, :],
            jnp.tile(gq[QK_NOPE + half:], n_heads)[None, :],
            gk[None, :QK_NOPE], jnp.tile(gk_pe, 2)[None, :], jnp.tile(gk_pe_sw, 2)[None, :],
            cosq, sinq, cosk, sink, n_heads=n_heads, tm=tk)

        bias = _bias_tiles(rel_bias, t)
        gmax = lambda g: jnp.max(jnp.abs(g))
        bound_d = (1.02 * LOG2E * DH_DIFF ** 0.5 * gmax(g_q_diff[l]) * gmax(g_k_diff[l])
                   + LOG2E * gmax(rel_bias))
        bound_m = 1.02 * LOG2E * QK_HEAD ** 0.5 * gmax(g_q_mla[l]) * gmax(g_k_mla[l])
        diff_args = (rel_bias, lambda_vecs[l], g_subln[l][:, None], qt_d, kd, vt_d, bias)
        a_out = lax.cond(
            bound_d <= SCORE_BOUND,
            lambda *a: _diff_attn(*a, lambda_init=lambda_init, bounded=True),
            lambda *a: _diff_attn(*a, lambda_init=lambda_init, bounded=False), *diff_args)
        b_out = lax.cond(
            bound_m <= SCORE_BOUND,
            lambda *a: _mla_attn(*a, bounded=True),
            lambda *a: _mla_attn(*a, bounded=False), qt_m, k_m, vt_m)

        wo = w_out[l].astype(BF16)
        x, h2 = _out_proj(x, gt1, a_out, b_out, wo[:diff_width], wo[diff_width:], sc2, sh2,
                          g_norm2[l][None, :], tm=tm)
        x = _ffn(x, h2, gt2, w_gate[l].astype(BF16), w_up[l].astype(BF16), w_down[l].astype(BF16),
                 tm=_pick_tile(s, FFN_TM), tf=_pick_tile(w_gate.shape[-1], FFN_TF))
    return x
```

```python
import functools
import math

import jax
import jax.numpy as jnp
from jax import lax
from jax.experimental import pallas as pl
from jax.experimental.pallas import tpu as pltpu

F32 = jnp.float32
BF16 = jnp.bfloat16

DH_DIFF = 64
DV_DIFF = 2 * DH_DIFF
QK_NOPE = 128
QK_ROPE = 64
QK_HEAD = QK_NOPE + QK_ROPE
V_MLA = 128
Q_LORA = 512
KV_LORA = 256
ROPE_THETA = 10000.0
REL_BUCKETS = 32
REL_MAX_DIST = 128
EPS = 1e-6
LOG2E = 1.4426950408889634

LANES = 128
SUBLANES = 8
VMEM_LIMIT_BYTES = 56 * 1024 * 1024

OUT_PROJ_SPLIT = 4
IN_PROJ_SPLIT = 2
FFN_TM = 1024
FFN_TF = 512
FFN_COL_SPLIT = 2
ATT_TK = 512
DIFF_TQ = 512
MLA_TQ = 512
KEY_CHUNK = 256
BOUNDED_DEPTH = 2
DIFF_BOUNDED_Q_TILES = 8
MLA_BOUNDED_Q_TILES = 8
SCORE_BOUND = 40.0


def _cparams(sem):
    return pltpu.CompilerParams(dimension_semantics=sem, vmem_limit_bytes=VMEM_LIMIT_BYTES)


def _const_spec(shape):
    nd = len(shape)
    return pl.BlockSpec(shape, lambda *_: (0,) * nd, pipeline_mode=pl.Buffered(1))


def _ada_kernel(c_ref, w_ref, b_ref, o_ref):
    c = c_ref[...]
    ca = c * jax.nn.sigmoid(c)
    o_ref[...] = jnp.dot(ca.astype(BF16), w_ref[...].astype(BF16), preferred_element_type=F32) + b_ref[...]


def _ada(c_pad, w, b, tn=1024):
    m, d = c_pad.shape
    n = w.shape[1]
    return pl.pallas_call(
        _ada_kernel,
        grid=(n // tn,),
        in_specs=[pl.BlockSpec((m, d), lambda j: (0, 0)),
                  pl.BlockSpec((d, tn), lambda j: (0, j)),
                  pl.BlockSpec((1, tn), lambda j: (0, j))],
        out_specs=pl.BlockSpec((m, tn), lambda j: (0, j)),
        out_shape=jax.ShapeDtypeStruct((m, n), F32),
        compiler_params=_cparams(("arbitrary",)),
        name="ada",
    )(c_pad, w, b)


def _modulated_norm(x, g, sc, sh):
    ms = jnp.mean(x * x, axis=-1, keepdims=True)
    return (x * lax.rsqrt(ms + EPS) * g) * (1.0 + sc) + sh


def _half_lane_norm(blk, g2, lo_mask, out_scale):
    sq = blk * blk
    s_lo = jnp.sum(jnp.where(lo_mask, sq, 0.0), axis=-1, keepdims=True)
    s_hi = jnp.sum(jnp.where(lo_mask, 0.0, sq), axis=-1, keepdims=True)
    inv = jnp.where(lo_mask, lax.rsqrt(s_lo * (1.0 / DH_DIFF) + EPS),
                    lax.rsqrt(s_hi * (1.0 / DH_DIFF) + EPS))
    return blk * inv * (g2 * out_scale)


def _in_proj_kernel(x_ref, sc_ref, sh_ref, g1_ref, wqkv_ref, wlat_ref, gqd_ref, gkd_ref,
                    gqa_ref, wqb_ref, gkva_ref, wkvb_ref, gqn_ref, gq1_ref, gq2_ref,
                    gkn_ref, gkp_ref, gkps_ref, cosq_ref, sinq_ref, cosk_ref, sink_ref,
                    qtd_ref, kd_ref, vtd_ref, qtm_ref, km_ref, vtm_ref, *, n_heads):
    tm = x_ref.shape[1]
    rows = tm // IN_PROJ_SPLIT
    wd = n_heads * LANES
    half = QK_ROPE // 2
    wx = n_heads * half
    lo_mask = lax.broadcasted_iota(jnp.int32, (rows, LANES), 1) < DH_DIFF
    lane_x = lax.broadcasted_iota(jnp.int32, (rows, wx), 1) // half
    qscale = DH_DIFF ** -0.5 * LOG2E
    mscale = QK_HEAD ** -0.5 * LOG2E

    def rms(v, g):
        return v * lax.rsqrt(jnp.mean(v * v, axis=-1, keepdims=True) + EPS) * g

    for sb in range(IN_PROJ_SPLIT):
        rs = slice(sb * rows, (sb + 1) * rows)
        h = _modulated_norm(x_ref[0, rs, :], g1_ref[...], sc_ref[0], sh_ref[0]).astype(BF16)
        lat = jnp.dot(h, wqkv_ref[:, 3 * wd:3 * wd + Q_LORA + KV_LORA], preferred_element_type=F32)
        cq = lat[:, 0:Q_LORA]
        ckv = lat[:, Q_LORA:Q_LORA + KV_LORA]
        kpe4 = jnp.dot(h, wlat_ref[...], preferred_element_type=F32)
        kpe2 = kpe4[:, 0:LANES]
        kpes2 = kpe4[:, LANES:2 * LANES]

        vd = jnp.dot(h, wqkv_ref[:, 2 * wd:3 * wd], preferred_element_type=F32)
        for hh in range(n_heads):
            vtd_ref[0, hh, 0, :, rs] = vd[:, hh * LANES:(hh + 1) * LANES].T.astype(BF16)

        kv = jnp.dot(rms(ckv, gkva_ref[...]).astype(BF16), wkvb_ref[...], preferred_element_type=F32)
        for hh in range(n_heads):
            vtm_ref[0, hh, 0, :, rs] = kv[:, wd + hh * LANES:wd + (hh + 1) * LANES].T.astype(BF16)
        ss_pe = jnp.sum(jnp.where(lo_mask, kpe2 * kpe2, 0.0), axis=-1, keepdims=True)
        kr2 = kpe2 * gkp_ref[...] * cosk_ref[rs, :] + kpes2 * gkps_ref[...] * sink_ref[rs, :]
        for hh in range(n_heads):
            sl = slice(hh * LANES, (hh + 1) * LANES)
            kn = kv[:, sl]
            ss = jnp.sum(kn * kn, axis=-1, keepdims=True) + ss_pe
            r = lax.rsqrt(ss * (1.0 / QK_HEAD) + EPS)
            km_ref[0, hh, rs, 0:QK_NOPE] = (kn * r * gkn_ref[...]).astype(BF16)
            km_ref[0, hh, rs, QK_NOPE:QK_HEAD] = (kr2 * r)[:, 0:QK_ROPE].astype(BF16)

        qd = jnp.dot(h, wqkv_ref[:, 0:wd], preferred_element_type=F32)
        for hh in range(n_heads):
            sl = slice(hh * LANES, (hh + 1) * LANES)
            qtd_ref[0, hh, :, rs] = _half_lane_norm(qd[:, sl], gqd_ref[...], lo_mask, qscale).T.astype(BF16)

        qm = jnp.dot(rms(cq, gqa_ref[...]).astype(BF16), wqb_ref[...], preferred_element_type=F32)
        x1 = qm[:, wd:wd + wx]
        x2 = qm[:, wd + wx:wd + 2 * wx]
        sq_x = x1 * x1 + x2 * x2
        inv_x = jnp.zeros((rows, wx), F32)
        for hh in range(n_heads):
            sl = slice(hh * LANES, (hh + 1) * LANES)
            qn = qm[:, sl]
            ss = (jnp.sum(qn * qn, axis=-1, keepdims=True)
                  + jnp.sum(jnp.where(lane_x == hh, sq_x, 0.0), axis=-1, keepdims=True))
            r = lax.rsqrt(ss * (1.0 / QK_HEAD) + EPS) * mscale
            qtm_ref[0, hh, 0:QK_NOPE, rs] = (qn * r * gqn_ref[...]).T.astype(BF16)
            inv_x = jnp.where(lane_x == hh, r, inv_x)
        a1 = x1 * gq1_ref[...]
        a2 = x2 * gq2_ref[...]
        cq_t = cosq_ref[rs, :]
        sq_t = sinq_ref[rs, :]
        x1t = ((a1 * cq_t - a2 * sq_t) * inv_x).T.astype(BF16)
        x2t = ((a2 * cq_t + a1 * sq_t) * inv_x).T.astype(BF16)
        for hh in range(n_heads):
            qtm_ref[0, hh, QK_NOPE:QK_NOPE + half, rs] = x1t[hh * half:(hh + 1) * half]
            qtm_ref[0, hh, QK_NOPE + half:QK_HEAD, rs] = x2t[hh * half:(hh + 1) * half]

        kd = jnp.dot(h, wqkv_ref[:, wd:2 * wd], preferred_element_type=F32)
        for hh in range(n_heads):
            sl = slice(hh * LANES, (hh + 1) * LANES)
            kd_ref[0, rs, sl] = _half_lane_norm(kd[:, sl], gkd_ref[...], lo_mask, 1.0).astype(BF16)


def _in_proj(x, sc1, sh1, g1, wqkv, wlat, gqd, gkd, gqa, wqb, gkva, wkvb, gqn, gq1, gq2,
             gkn, gkp, gkps, cosq, sinq, cosk, sink, *, n_heads, tm):
    b, s, d = x.shape
    wd = n_heads * LANES
    wx = n_heads * (QK_ROPE // 2)
    row = lambda w: pl.BlockSpec((1, tm, w), lambda bi, i: (bi, i, 0))
    mod = pl.BlockSpec((1, 1, d), lambda bi, i: (bi, 0, 0))
    tab = lambda w: pl.BlockSpec((tm, w), lambda bi, i: (i, 0))
    consts = [g1, wqkv, wlat, gqd, gkd, gqa, wqb, gkva, wkvb, gqn, gq1, gq2, gkn, gkp, gkps]
    return pl.pallas_call(
        functools.partial(_in_proj_kernel, n_heads=n_heads),
        grid=(b, s // tm),
        in_specs=[row(d), mod, mod] + [_const_spec(a.shape) for a in consts]
                 + [tab(wx), tab(wx), tab(LANES), tab(LANES)],
        out_specs=[pl.BlockSpec((1, n_heads, LANES, tm), lambda bi, i: (bi, 0, 0, i)),
                   row(wd),
                   pl.BlockSpec((1, n_heads, 1, DV_DIFF, tm), lambda bi, i: (bi, 0, i, 0, 0)),
                   pl.BlockSpec((1, n_heads, QK_HEAD, tm), lambda bi, i: (bi, 0, 0, i)),
                   pl.BlockSpec((1, n_heads, tm, QK_HEAD), lambda bi, i: (bi, 0, i, 0)),
                   pl.BlockSpec((1, n_heads, 1, V_MLA, tm), lambda bi, i: (bi, 0, i, 0, 0))],
        out_shape=[jax.ShapeDtypeStruct((b, n_heads, LANES, s), BF16),
                   jax.ShapeDtypeStruct((b, s, wd), BF16),
                   jax.ShapeDtypeStruct((b, n_heads, s // tm, DV_DIFF, tm), BF16),
                   jax.ShapeDtypeStruct((b, n_heads, QK_HEAD, s), BF16),
                   jax.ShapeDtypeStruct((b, n_heads, s, QK_HEAD), BF16),
                   jax.ShapeDtypeStruct((b, n_heads, s // tm, V_MLA, tm), BF16)],
        compiler_params=_cparams(("parallel", "parallel")),
        name="in_proj",
    )(x, sc1, sh1, *consts, cosq, sinq, cosk, sink)


def _t5_bucket(rel):
    nb = REL_BUCKETS // 2
    max_exact = nb // 2
    base = jnp.where(rel > 0, nb, 0)
    n = jnp.abs(rel)
    nf = jnp.maximum(n, 1).astype(jnp.float32)
    large = max_exact + (jnp.log(nf / max_exact) / math.log(REL_MAX_DIST / max_exact)
                         * (nb - max_exact)).astype(jnp.int32)
    large = jnp.minimum(large, nb - 1)
    return base + jnp.where(n < max_exact, n, large)


def _bias_kernel(rb_ref, bucket_ref, o_ref):
    hh = pl.program_id(0)
    t = o_ref.shape[-1]
    far_left = rb_ref[REL_BUCKETS // 2 - 1, hh]
    far_right = rb_ref[REL_BUCKETS - 1, hh]
    o_ref[0, 0] = jnp.full((t, t), far_left, F32) * LOG2E
    o_ref[0, 4] = jnp.full((t, t), far_right, F32) * LOG2E
    bk_strip = bucket_ref[...]
    strip = jnp.zeros(bk_strip.shape, F32)
    for bk in range(REL_BUCKETS):
        strip = jnp.where(bk_strip == bk, rb_ref[bk, hh], strip)
    strip = strip * LOG2E
    for d in (-1, 0, 1):
        diag = jnp.concatenate([strip[:, (2 - d) * t:(3 - d) * t], strip[:, (1 - d) * t:(2 - d) * t]], axis=1)
        rows = jnp.broadcast_to(diag, (t, 2 * t))
        o_ref[0, d + 2] = pltpu.roll(rows, 0, 1, stride=1, stride_axis=0)[:, :t]


def _bias_tiles(rel_bias, t):
    n_heads = rel_bias.shape[1]
    buckets = _t5_bucket(2 * t - jnp.arange(4 * t, dtype=jnp.int32))[None, :]
    return pl.pallas_call(
        _bias_kernel,
        grid=(n_heads,),
        in_specs=[pl.BlockSpec(memory_space=pltpu.SMEM),
                  pl.BlockSpec((1, 4 * t), lambda hh: (0, 0))],
        out_specs=pl.BlockSpec((1, 5, t, t), lambda hh: (hh, 0, 0, 0)),
        out_shape=jax.ShapeDtypeStruct((n_heads, 5, t, t), F32),
        compiler_params=_cparams(("arbitrary",)),
        name="rel_bias_tiles",
    )(rel_bias, buckets)


ONES_ROWS = 16


def _softmax_state_step(s, offset, vt, state):
    m_new, alpha, p = _softmax_weights(s, offset, state[0])
    return m_new, _accumulate(alpha, state[1], vt, p)


def _softmax_weights(s, offset, m_old):
    m_new = jnp.maximum(m_old, jnp.max(s, axis=0, keepdims=True) + offset)
    alpha = jnp.exp2(m_old - m_new)
    p = jnp.exp2(s - (m_new - offset)).astype(BF16)
    return m_new, alpha, p


def _accumulate(alpha, acc_old, vt, p):
    return alpha * acc_old + jnp.dot(vt, p, preferred_element_type=F32)


def _init_state(dv, t):
    return (jnp.full((1, t), -jnp.inf, F32), jnp.zeros((dv + ONES_ROWS, t), F32))


def _with_ones(vt):
    return jnp.concatenate([vt, jnp.ones((ONES_ROWS, vt.shape[1]), vt.dtype)], axis=0)


def _normalised(state, dv):
    _, acc = state
    return acc[:dv] / acc[dv:dv + 1]


def _diff_attn_kernel(rb_ref, lamv_ref, gsub_ref, qt_ref, k_ref, vt_ref, bias_ref, o_ref, *,
                      lambda_init, bounded):
    hh = pl.program_id(0)
    t = bias_ref.shape[-1]
    nq = qt_ref.shape[-1] // t
    nkt = vt_ref.shape[2]

    off_left = rb_ref[REL_BUCKETS // 2 - 1, hh] * LOG2E
    off_right = rb_ref[REL_BUCKETS - 1, hh] * LOG2E
    lv = lamv_ref[...]
    lam = (jnp.exp(jnp.sum(lv[0:1] * lv[1:2], axis=-1, keepdims=True))
           - jnp.exp(jnp.sum(lv[2:3] * lv[3:4], axis=-1, keepdims=True)) + lambda_init)

    n_near = min(nkt, 3)
    row = lax.broadcasted_iota(jnp.int32, (LANES, t), 0)
    zero = jnp.zeros((LANES, t), BF16)

    def query_maps(j):
        qt = qt_ref[0, 0, :, j * t:(j + 1) * t]
        return jnp.where(row < DH_DIFF, qt, zero), jnp.where(row < DH_DIFF, zero, qt)

    def tile_index(j, r):
        return lax.rem(pl.program_id(2) * nq + j - 1 + r + nkt, nkt)

    def far_offset(j, r):
        return jnp.where(tile_index(j, r) < pl.program_id(2) * nq + j, off_left, off_right)

    def biased_scores(j, qmap, r, k0, ck):
        kj = tile_index(j, r)
        k_t = k_ref[0, pl.ds(pl.multiple_of(kj * t + k0, ck), ck), :]
        s = jnp.dot(k_t, qmap, preferred_element_type=F32)
        saturated = (r >= n_near or (nkt >= 3 and r == 0 and k0 + ck - 1 - t <= -REL_MAX_DIST)
                     or (nkt >= 3 and r == 2 and k0 + 1 >= REL_MAX_DIST))
        if not (bounded and saturated) and r < n_near:
            s = s + bias_ref[0, jnp.clip(kj - (pl.program_id(2) * nq + j), -2, 2) + 2, k0:k0 + ck, :]
        elif bounded:
            s = s + far_offset(j, r)
        return s

    def finish(j, outs):
        o = outs[0] - lam * outs[1]
        ms = jnp.mean(o * o, axis=0, keepdims=True)
        y = o * lax.rsqrt(ms + EPS) * (gsub_ref[...] * (1.0 - lambda_init))
        o_ref[0, j * t:(j + 1) * t, :] = y.T.astype(BF16)

    if bounded:
        ck = min(t, KEY_CHUNK)
        npt = t // ck
        qmaps = [query_maps(j) for j in range(nq)]
        items = [(j, r, c, mp) for j in range(nq) for r in range(nkt) for c in range(npt) for mp in range(2)]
        score = lambda it: biased_scores(it[0], qmaps[it[0]][it[3]], it[1], it[2] * ck, ck)
        depth = min(BOUNDED_DEPTH, len(items))
        pending = [score(it) for it in items[:depth]]
        for n, (j, r, c, mp) in enumerate(items):
            if (r, c, mp) == (0, 0, 0):
                accs = [jnp.zeros((DV_DIFF, t), F32) for _ in range(2)]
                dens = [jnp.zeros((SUBLANES, t), F32) for _ in range(2)]
            s = pending.pop(0)
            if n + depth < len(items):
                pending.append(score(items[n + depth]))
            p = jnp.exp2(s)
            dens[mp] = dens[mp] + jnp.sum(p.reshape(ck // SUBLANES, SUBLANES, t), axis=0)
            accs[mp] = accs[mp] + jnp.dot(vt_ref[0, 0, tile_index(j, r), :, c * ck:(c + 1) * ck],
                                          p.astype(BF16), preferred_element_type=F32)
            if (r, c, mp) == (nkt - 1, npt - 1, 1):
                finish(j, [acc / jnp.sum(den, axis=0, keepdims=True) for acc, den in zip(accs, dens)])
    else:
        for j in range(nq):
            qmap = query_maps(j)
            states = [_init_state(DV_DIFF, t), _init_state(DV_DIFF, t)]
            score = lambda r: [biased_scores(j, qmap[mp], r, 0, t) for mp in range(2)]
            s_next = score(0)
            for r in range(nkt):
                s_cur = s_next
                if r + 1 < nkt:
                    s_next = score(r + 1)
                vt = _with_ones(vt_ref[0, 0, tile_index(j, r)])
                offset = 0.0 if r < n_near else far_offset(j, r)
                for mp in range(2):
                    states[mp] = _softmax_state_step(s_cur[mp], offset, vt, states[mp])
            finish(j, [_normalised(st, DV_DIFF) for st in states])


def _diff_attn(rel_bias, lamv, gsub_col, qt, k, vt, bias, *, lambda_init, bounded):
    b, n_heads, _, s = qt.shape
    t = bias.shape[-1]
    nkt, tk = vt.shape[2], vt.shape[4]
    assert tk == t
    tq = _pick_tile(s, t * (DIFF_BOUNDED_Q_TILES if bounded else 1))
    return pl.pallas_call(
        functools.partial(_diff_attn_kernel, lambda_init=lambda_init, bounded=bounded),
        grid=(n_heads, b, s // tq),
        in_specs=[pl.BlockSpec(memory_space=pltpu.SMEM),
                  pl.BlockSpec(lamv.shape, lambda hh, bi, qi: (0, 0)),
                  pl.BlockSpec(gsub_col.shape, lambda hh, bi, qi: (0, 0)),
                  pl.BlockSpec((1, 1, LANES, tq), lambda hh, bi, qi: (bi, hh, 0, qi)),
                  pl.BlockSpec((1, s, LANES), lambda hh, bi, qi: (bi, 0, hh)),
                  pl.BlockSpec((1, 1, nkt, DV_DIFF, tk), lambda hh, bi, qi: (bi, hh, 0, 0, 0)),
                  pl.BlockSpec((1, 5, t, t), lambda hh, bi, qi: (hh, 0, 0, 0))],
        out_specs=pl.BlockSpec((1, tq, DV_DIFF), lambda hh, bi, qi: (bi, qi, hh)),
        out_shape=jax.ShapeDtypeStruct((b, s, n_heads * DV_DIFF), BF16),
        compiler_params=_cparams(("parallel", "parallel", "parallel")),
        name="diff_attn",
    )(rel_bias, lamv, gsub_col, qt, k, vt, bias)


def _mla_attn_kernel(qt_ref, k_ref, vt_ref, o_ref):
    nkt = vt_ref.shape[2]
    tk = vt_ref.shape[-1]
    qt = qt_ref[0, 0]
    m, acc = _init_state(V_MLA, qt.shape[-1])
    scores = lambda kj: jnp.dot(k_ref[0, 0, kj * tk:(kj + 1) * tk, :], qt, preferred_element_type=F32)
    depth = 2
    pending = [scores(kj) for kj in range(min(depth, nkt))]
    for kj in range(nkt):
        s = pending.pop(0)
        if kj + depth < nkt:
            pending.append(scores(kj + depth))
        m, alpha, p = _softmax_weights(s, 0.0, m)
        acc = _accumulate(alpha, acc, _with_ones(vt_ref[0, 0, kj]), p)
    o_ref[0] = _normalised((m, acc), V_MLA).T.astype(BF16)


def _mla_attn_bounded_kernel(qt_ref, k_ref, vt_ref, o_ref):
    nkt = vt_ref.shape[2]
    tk = vt_ref.shape[-1]
    t = min(qt_ref.shape[-1], MLA_TQ)
    nq = qt_ref.shape[-1] // t
    ck = min(tk, KEY_CHUNK)
    npt = tk // ck
    n_chunks = nkt * npt
    items = [(j, i) for j in range(nq) for i in range(n_chunks)]

    def scores(item):
        j, i = item
        return jnp.dot(k_ref[0, 0, i * ck:(i + 1) * ck, :], qt_ref[0, 0, :, j * t:(j + 1) * t],
                       preferred_element_type=F32)

    depth = min(BOUNDED_DEPTH, len(items))
    pending = [scores(it) for it in items[:depth]]
    for n, (j, i) in enumerate(items):
        if i == 0:
            acc = jnp.zeros((V_MLA, t), F32)
            den = jnp.zeros((SUBLANES, t), F32)
        s = pending.pop(0)
        if n + depth < len(items):
            pending.append(scores(items[n + depth]))
        c = i % npt
        p = jnp.exp2(s)
        den = den + jnp.sum(p.reshape(ck // SUBLANES, SUBLANES, t), axis=0)
        acc = acc + jnp.dot(vt_ref[0, 0, i // npt, :, c * ck:(c + 1) * ck], p.astype(BF16),
                            preferred_element_type=F32)
        if i == n_chunks - 1:
            o_ref[0, j * t:(j + 1) * t, :] = (acc / jnp.sum(den, axis=0, keepdims=True)).T.astype(BF16)


def _mla_attn(qt, k, vt, *, bounded):
    b, n_heads, _, s = qt.shape
    t = _pick_tile(s, MLA_TQ * (MLA_BOUNDED_Q_TILES if bounded else 1))
    nkt, tk = vt.shape[2], vt.shape[4]
    return pl.pallas_call(
        _mla_attn_bounded_kernel if bounded else _mla_attn_kernel,
        grid=(b, n_heads, s // t),
        in_specs=[pl.BlockSpec((1, 1, QK_HEAD, t), lambda bi, hh, qi: (bi, hh, 0, qi)),
                  pl.BlockSpec((1, 1, s, QK_HEAD), lambda bi, hh, qi: (bi, hh, 0, 0)),
                  pl.BlockSpec((1, 1, nkt, V_MLA, tk), lambda bi, hh, qi: (bi, hh, 0, 0, 0))],
        out_specs=pl.BlockSpec((1, t, V_MLA), lambda bi, hh, qi: (bi, qi, hh)),
        out_shape=jax.ShapeDtypeStruct((b, s, n_heads * V_MLA), BF16),
        compiler_params=_cparams(("parallel", "parallel", "parallel")),
        name="mla_attn",
    )(qt, k, vt)


def _out_proj_kernel(x_ref, gt_ref, a_ref, b_ref, wa_ref, wb_ref, sc_ref, sh_ref, g2_ref, o_ref, h_ref):
    tm = x_ref.shape[1]
    rows = tm // OUT_PROJ_SPLIT
    mixes = []
    for i in range(OUT_PROJ_SPLIT):
        sl = slice(i * rows, (i + 1) * rows)
        mixes.append(jnp.dot(a_ref[0, sl, :], wa_ref[...], preferred_element_type=F32)
                     + jnp.dot(b_ref[0, sl, :], wb_ref[...], preferred_element_type=F32))
    for i in range(OUT_PROJ_SPLIT):
        sl = slice(i * rows, (i + 1) * rows)
        x1 = x_ref[0, sl, :] + gt_ref[0] * mixes[i]
        o_ref[0, sl, :] = x1
        h_ref[0, sl, :] = _modulated_norm(x1, g2_ref[...], sc_ref[0], sh_ref[0]).astype(BF16)


def _out_proj(x, gt1, a, bb, wa, wb, sc2, sh2, g2, *, tm):
    b, s, d = x.shape
    row = lambda w: pl.BlockSpec((1, tm, w), lambda bi, i: (bi, i, 0))
    mod = pl.BlockSpec((1, 1, d), lambda bi, i: (bi, 0, 0))
    return pl.pallas_call(
        _out_proj_kernel,
        grid=(b, s // tm),
        in_specs=[row(d), mod, row(a.shape[-1]), row(bb.shape[-1]), _const_spec(wa.shape),
                  _const_spec(wb.shape), mod, mod, _const_spec(g2.shape)],
        out_specs=[row(d), row(d)],
        out_shape=[jax.ShapeDtypeStruct((b, s, d), F32), jax.ShapeDtypeStruct((b, s, d), BF16)],
        compiler_params=_cparams(("parallel", "parallel")),
        name="out_proj",
    )(x, gt1, a, bb, wa, wb, sc2, sh2, g2)


def _ffn_kernel(x_ref, h_ref, gt_ref, wg_ref, wu_ref, wd_ref, o_ref, xs_ref, *, n_x_chunks):
    j = pl.program_id(2)
    rows = x_ref.shape[1]

    @pl.when(j == 0)
    def _():
        o_ref[...] = jnp.zeros(o_ref.shape, F32)

    chunk = jnp.minimum(j, n_x_chunks - 1)
    xs_ref[pl.ds(pl.multiple_of(chunk * rows, rows), rows), :] = x_ref[0]

    h = h_ref[0]
    tf = wg_ref.shape[1]
    cf = tf // FFN_COL_SPLIT
    gates = []
    for c in range(FFN_COL_SPLIT):
        cs = slice(c * cf, (c + 1) * cf)
        gates.append((jnp.dot(h, wg_ref[:, cs], preferred_element_type=F32),
                      jnp.dot(h, wu_ref[:, cs], preferred_element_type=F32)))
    part = None
    for c, (g, u) in enumerate(gates):
        act = (g * jax.nn.sigmoid(g) * u).astype(BF16)
        d = jnp.dot(act, wd_ref[c * cf:(c + 1) * cf, :], preferred_element_type=F32)
        part = d if part is None else part + d
    o_ref[0] += part

    @pl.when(j == pl.num_programs(2) - 1)
    def _():
        o_ref[0] = xs_ref[...] + gt_ref[0] * o_ref[0]


def _ffn(x, h2, gt2, wg, wu, wd, *, tm, tf):
    b, s, d = x.shape
    dff = wg.shape[1]
    nj = dff // tf
    nxc = 1
    while nxc * 2 <= nj and (tm // (nxc * 2)) % SUBLANES == 0:
        nxc *= 2
    row = pl.BlockSpec((1, tm, d), lambda bi, i, j: (bi, i, 0))
    x_chunk = pl.BlockSpec((1, tm // nxc, d), lambda bi, i, j: (bi, i * nxc + jnp.minimum(j, nxc - 1), 0))
    return pl.pallas_call(
        functools.partial(_ffn_kernel, n_x_chunks=nxc),
        grid=(b, s // tm, nj),
        in_specs=[x_chunk, row, pl.BlockSpec((1, 1, d), lambda bi, i, j: (bi, 0, 0)),
                  pl.BlockSpec((d, tf), lambda bi, i, j: (0, j)),
                  pl.BlockSpec((d, tf), lambda bi, i, j: (0, j)),
                  pl.BlockSpec((tf, d), lambda bi, i, j: (j, 0))],
        out_specs=row,
        out_shape=jax.ShapeDtypeStruct((b, s, d), F32),
        scratch_shapes=[pltpu.VMEM((tm, d), F32)],
        compiler_params=_cparams(("parallel", "parallel", "arbitrary")),
        name="ffn",
    )(x, h2, gt2, wg, wu, wd)


def _rope_tables(s):
    pos = jnp.arange(s, dtype=jnp.float32)
    inv = 1.0 / (ROPE_THETA ** (jnp.arange(0, QK_ROPE, 2, dtype=jnp.float32) / QK_ROPE))
    ang = pos[:, None] * inv[None, :]
    return jnp.cos(ang), jnp.sin(ang)


def _pick_tile(n, pref):
    t = min(pref, n)
    while n % t:
        t //= 2
    return t


def kernel(x, c, rel_bias, w_ada, b_ada, g_norm1, w_in, g_q_diff, g_k_diff, lambda_vecs, g_subln, g_q_a, w_q_b, g_kv_a, w_kv_b, g_q_mla, g_k_mla, w_out, g_norm2, w_gate, w_up, w_down):
    b, s, d = x.shape
    depth = w_ada.shape[0]
    diff_width = d // 2
    n_hd = diff_width // DV_DIFF
    n_hm = (d - diff_width) // V_MLA
    assert n_hd == n_hm and n_hd % 2 == 0
    n_heads = n_hd
    wd = n_heads * LANES
    half = QK_ROPE // 2
    tk = _pick_tile(s, ATT_TK)
    nkt = s // tk
    t = tk
    assert t + 1 >= REL_MAX_DIST
    tm = _pick_tile(s, 512)

    cos, sin = _rope_tables(s)
    cosq, sinq = jnp.tile(cos, (1, n_heads)), jnp.tile(sin, (1, n_heads))
    cosk = jnp.tile(cos, (1, 4))
    sink = jnp.tile(jnp.concatenate([-sin, sin], axis=1), (1, 2))

    c_pad = jnp.pad(c, ((0, (-b) % 8), (0, 0)))

    for l in range(depth):
        lambda_init = 0.8 - 0.6 * math.exp(-0.3 * l)
        mod = _ada(c_pad, w_ada[l], b_ada[l][None, :])[:b]
        sh1, sc1, gt1, sh2, sc2, gt2 = [m[:, None, :] for m in jnp.split(mod, 6, axis=-1)]

        wi = w_in[l]
        o = 3 * wd + Q_LORA + KV_LORA
        w_kpe = wi[:, o:o + QK_ROPE]
        w_kpe_sw = jnp.concatenate([w_kpe[:, half:], w_kpe[:, :half]], axis=1)
        wqkv = wi.astype(BF16)
        wlat = jnp.concatenate([w_kpe, w_kpe, w_kpe_sw, w_kpe_sw], axis=1).astype(BF16)
        wq = w_q_b[l].reshape(Q_LORA, n_heads, QK_HEAD)
        wqb = jnp.concatenate([wq[:, :, :QK_NOPE].reshape(Q_LORA, -1),
                               wq[:, :, QK_NOPE:QK_NOPE + half].reshape(Q_LORA, -1),
                               wq[:, :, QK_NOPE + half:].reshape(Q_LORA, -1)], axis=1).astype(BF16)
        wkv = w_kv_b[l].reshape(KV_LORA, n_heads, QK_NOPE + V_MLA)
        wkvb = jnp.concatenate([wkv[:, :, :QK_NOPE].reshape(KV_LORA, -1),
                                wkv[:, :, QK_NOPE:].reshape(KV_LORA, -1)], axis=1).astype(BF16)
        gq, gk = g_q_mla[l], g_k_mla[l]
        gk_pe = gk[QK_NOPE:]
        gk_pe_sw = jnp.concatenate([gk_pe[half:], gk_pe[:half]])

        qt_d, kd, vt_d, qt_m, k_m, vt_m = _in_proj(
            x, sc1, sh1, g_norm1[l][None, :], wqkv, wlat,
            jnp.tile(g_q_diff[l], 2)[None, :], jnp.tile(g_k_diff[l], 2)[None, :],
            g_q_a[l][None, :], wqb, g_kv_a[l][None, :], wkvb,
            gq[None, :QK_NOPE], jnp.tile(gq[QK_NOPE:QK_NOPE + half], n_heads)[None, :],
            jnp.tile(gq[QK_NOPE + half:], n_heads)[None, :],
            gk[None, :QK_NOPE], jnp.tile(gk_pe, 2)[None, :], jnp.tile(gk_pe_sw, 2)[None, :],
            cosq, sinq, cosk, sink, n_heads=n_heads, tm=tk)

        bias = _bias_tiles(rel_bias, t)
        gmax = lambda g: jnp.max(jnp.abs(g))
        bound_d = (1.02 * LOG2E * DH_DIFF ** 0.5 * gmax(g_q_diff[l]) * gmax(g_k_diff[l])
                   + LOG2E * gmax(rel_bias))
        bound_m = 1.02 * LOG2E * QK_HEAD ** 0.5 * gmax(g_q_mla[l]) * gmax(g_k_mla[l])
        diff_args = (rel_bias, lambda_vecs[l], g_subln[l][:, None], qt_d, kd, vt_d, bias)
        a_out = lax.cond(
            bound_d <= SCORE_BOUND,
            lambda *a: _diff_attn(*a, lambda_init=lambda_init, bounded=True),
            lambda *a: _diff_attn(*a, lambda_init=lambda_init, bounded=False), *diff_args)
        b_out = lax.cond(
            bound_m <= SCORE_BOUND,
            lambda *a: _mla_attn(*a, bounded=True),
            lambda *a: _mla_attn(*a, bounded=False), qt_m, k_m, vt_m)

        wo = w_out[l].astype(BF16)
        x, h2 = _out_proj(x, gt1, a_out, b_out, wo[:diff_width], wo[diff_width:], sc2, sh2,
                          g_norm2[l][None, :], tm=tm)
        x = _ffn(x, h2, gt2, w_gate[l].astype(BF16), w_up[l].astype(BF16), w_down[l].astype(BF16),
                 tm=_pick_tile(s, FFN_TM), tf=_pick_tile(w_gate.shape[-1], FFN_TF))
    return x
```

```python
import functools
import math

import jax
import jax.numpy as jnp
from jax import lax
from jax.experimental import pallas as pl
from jax.experimental.pallas import tpu as pltpu

F32 = jnp.float32
BF16 = jnp.bfloat16

DH_DIFF = 64
DV_DIFF = 2 * DH_DIFF
QK_NOPE = 128
QK_ROPE = 64
QK_HEAD = QK_NOPE + QK_ROPE
V_MLA = 128
Q_LORA = 512
KV_LORA = 256
ROPE_THETA = 10000.0
REL_BUCKETS = 32
REL_MAX_DIST = 128
EPS = 1e-6
LOG2E = 1.4426950408889634

LANES = 128
SUBLANES = 8
VMEM_LIMIT_BYTES = 56 * 1024 * 1024

OUT_PROJ_SPLIT = 4
IN_PROJ_SPLIT = 2
FFN_TM = 1024
FFN_TF = 512
FFN_COL_SPLIT = 2
ATT_TK = 512
MLA_TQ = 512
KEY_CHUNK = 256
BOUNDED_DEPTH = 2
DIFF_BOUNDED_Q_TILES = 8
MLA_BOUNDED_Q_TILES = 8
SCORE_BOUND = 40.0
NORM_SLACK = 1.02


def _cparams(sem):
    return pltpu.CompilerParams(dimension_semantics=sem, vmem_limit_bytes=VMEM_LIMIT_BYTES)


def _const_spec(shape):
    nd = len(shape)
    return pl.BlockSpec(shape, lambda *_: (0,) * nd, pipeline_mode=pl.Buffered(1))


def _ada_kernel(c_ref, w_ref, b_ref, o_ref):
    c = c_ref[...]
    ca = c * jax.nn.sigmoid(c)
    o_ref[...] = jnp.dot(ca.astype(BF16), w_ref[...].astype(BF16), preferred_element_type=F32) + b_ref[...]


def _ada(c_pad, w, b, tn=1024):
    m, d = c_pad.shape
    n = w.shape[1]
    return pl.pallas_call(
        _ada_kernel,
        grid=(n // tn,),
        in_specs=[pl.BlockSpec((m, d), lambda j: (0, 0)),
                  pl.BlockSpec((d, tn), lambda j: (0, j)),
                  pl.BlockSpec((1, tn), lambda j: (0, j))],
        out_specs=pl.BlockSpec((m, tn), lambda j: (0, j)),
        out_shape=jax.ShapeDtypeStruct((m, n), F32),
        compiler_params=_cparams(("arbitrary",)),
        name="ada",
    )(c_pad, w, b)


def _modulated_norm(x, g, sc, sh):
    ms = jnp.mean(x * x, axis=-1, keepdims=True)
    return (x * lax.rsqrt(ms + EPS) * g) * (1.0 + sc) + sh


def _half_lane_norm(blk, g2, lo_mask, out_scale):
    sq = blk * blk
    s_lo = jnp.sum(jnp.where(lo_mask, sq, 0.0), axis=-1, keepdims=True)
    s_hi = jnp.sum(jnp.where(lo_mask, 0.0, sq), axis=-1, keepdims=True)
    inv = jnp.where(lo_mask, lax.rsqrt(s_lo * (1.0 / DH_DIFF) + EPS),
                    lax.rsqrt(s_hi * (1.0 / DH_DIFF) + EPS))
    return blk * inv * (g2 * out_scale)


def _in_proj_kernel(x_ref, sc_ref, sh_ref, g1_ref, wqkv_ref, wlat_ref, gqd_ref, gkd_ref,
                    gqa_ref, wqb_ref, gkva_ref, wkvb_ref, gqn_ref, gq1_ref, gq2_ref,
                    gkn_ref, gkp_ref, gkps_ref, cosq_ref, sinq_ref, cosk_ref, sink_ref,
                    qtd_ref, kd_ref, vtd_ref, qtm_ref, km_ref, vtm_ref, *, n_heads):
    tm = x_ref.shape[1]
    rows = tm // IN_PROJ_SPLIT
    wd = n_heads * LANES
    half = QK_ROPE // 2
    wx = n_heads * half
    lo_mask = lax.broadcasted_iota(jnp.int32, (rows, LANES), 1) < DH_DIFF
    lane_x = lax.broadcasted_iota(jnp.int32, (rows, wx), 1) // half
    qscale = DH_DIFF ** -0.5 * LOG2E
    mscale = QK_HEAD ** -0.5 * LOG2E

    def rms(v, g):
        return v * lax.rsqrt(jnp.mean(v * v, axis=-1, keepdims=True) + EPS) * g

    for sb in range(IN_PROJ_SPLIT):
        rs = slice(sb * rows, (sb + 1) * rows)
        h = _modulated_norm(x_ref[0, rs, :], g1_ref[...], sc_ref[0], sh_ref[0]).astype(BF16)
        lat = jnp.dot(h, wqkv_ref[:, 3 * wd:3 * wd + Q_LORA + KV_LORA], preferred_element_type=F32)
        cq = lat[:, 0:Q_LORA]
        ckv = lat[:, Q_LORA:Q_LORA + KV_LORA]
        kpe4 = jnp.dot(h, wlat_ref[...], preferred_element_type=F32)
        kpe2 = kpe4[:, 0:LANES]
        kpes2 = kpe4[:, LANES:2 * LANES]

        vd = jnp.dot(h, wqkv_ref[:, 2 * wd:3 * wd], preferred_element_type=F32)
        for hh in range(n_heads):
            vtd_ref[0, hh, 0, :, rs] = vd[:, hh * LANES:(hh + 1) * LANES].T.astype(BF16)

        kv = jnp.dot(rms(ckv, gkva_ref[...]).astype(BF16), wkvb_ref[...], preferred_element_type=F32)
        for hh in range(n_heads):
            vtm_ref[0, hh, 0, :, rs] = kv[:, wd + hh * LANES:wd + (hh + 1) * LANES].T.astype(BF16)
        ss_pe = jnp.sum(jnp.where(lo_mask, kpe2 * kpe2, 0.0), axis=-1, keepdims=True)
        kr2 = kpe2 * gkp_ref[...] * cosk_ref[rs, :] + kpes2 * gkps_ref[...] * sink_ref[rs, :]
        for hh in range(n_heads):
            sl = slice(hh * LANES, (hh + 1) * LANES)
            kn = kv[:, sl]
            ss = jnp.sum(kn * kn, axis=-1, keepdims=True) + ss_pe
            r = lax.rsqrt(ss * (1.0 / QK_HEAD) + EPS)
            km_ref[0, hh, rs, 0:QK_NOPE] = (kn * r * gkn_ref[...]).astype(BF16)
            km_ref[0, hh, rs, QK_NOPE:QK_HEAD] = (kr2 * r)[:, 0:QK_ROPE].astype(BF16)

        qd = jnp.dot(h, wqkv_ref[:, 0:wd], preferred_element_type=F32)
        for hh in range(n_heads):
            sl = slice(hh * LANES, (hh + 1) * LANES)
            qtd_ref[0, hh, :, rs] = _half_lane_norm(qd[:, sl], gqd_ref[...], lo_mask, qscale).T.astype(BF16)

        qm = jnp.dot(rms(cq, gqa_ref[...]).astype(BF16), wqb_ref[...], preferred_element_type=F32)
        x1 = qm[:, wd:wd + wx]
        x2 = qm[:, wd + wx:wd + 2 * wx]
        sq_x = x1 * x1 + x2 * x2
        inv_x = jnp.zeros((rows, wx), F32)
        for hh in range(n_heads):
            sl = slice(hh * LANES, (hh + 1) * LANES)
            qn = qm[:, sl]
            ss = (jnp.sum(qn * qn, axis=-1, keepdims=True)
                  + jnp.sum(jnp.where(lane_x == hh, sq_x, 0.0), axis=-1, keepdims=True))
            r = lax.rsqrt(ss * (1.0 / QK_HEAD) + EPS) * mscale
            qtm_ref[0, hh, 0:QK_NOPE, rs] = (qn * r * gqn_ref[...]).T.astype(BF16)
            inv_x = jnp.where(lane_x == hh, r, inv_x)
        a1 = x1 * gq1_ref[...]
        a2 = x2 * gq2_ref[...]
        cq_t = cosq_ref[rs, :]
        sq_t = sinq_ref[rs, :]
        x1t = ((a1 * cq_t - a2 * sq_t) * inv_x).T.astype(BF16)
        x2t = ((a2 * cq_t + a1 * sq_t) * inv_x).T.astype(BF16)
        for hh in range(n_heads):
            qtm_ref[0, hh, QK_NOPE:QK_NOPE + half, rs] = x1t[hh * half:(hh + 1) * half]
            qtm_ref[0, hh, QK_NOPE + half:QK_HEAD, rs] = x2t[hh * half:(hh + 1) * half]

        kd = jnp.dot(h, wqkv_ref[:, wd:2 * wd], preferred_element_type=F32)
        for hh in range(n_heads):
            sl = slice(hh * LANES, (hh + 1) * LANES)
            kd_ref[0, rs, sl] = _half_lane_norm(kd[:, sl], gkd_ref[...], lo_mask, 1.0).astype(BF16)


def _in_proj(x, sc1, sh1, g1, wqkv, wlat, gqd, gkd, gqa, wqb, gkva, wkvb, gqn, gq1, gq2,
             gkn, gkp, gkps, cosq, sinq, cosk, sink, *, n_heads, tm):
    b, s, d = x.shape
    wd = n_heads * LANES
    wx = n_heads * (QK_ROPE // 2)
    row = lambda w: pl.BlockSpec((1, tm, w), lambda bi, i: (bi, i, 0))
    mod = pl.BlockSpec((1, 1, d), lambda bi, i: (bi, 0, 0))
    tab = lambda w: pl.BlockSpec((tm, w), lambda bi, i: (i, 0))
    consts = [g1, wqkv, wlat, gqd, gkd, gqa, wqb, gkva, wkvb, gqn, gq1, gq2, gkn, gkp, gkps]
    return pl.pallas_call(
        functools.partial(_in_proj_kernel, n_heads=n_heads),
        grid=(b, s // tm),
        in_specs=[row(d), mod, mod] + [_const_spec(a.shape) for a in consts]
                 + [tab(wx), tab(wx), tab(LANES), tab(LANES)],
        out_specs=[pl.BlockSpec((1, n_heads, LANES, tm), lambda bi, i: (bi, 0, 0, i)),
                   row(wd),
                   pl.BlockSpec((1, n_heads, 1, DV_DIFF, tm), lambda bi, i: (bi, 0, i, 0, 0)),
                   pl.BlockSpec((1, n_heads, QK_HEAD, tm), lambda bi, i: (bi, 0, 0, i)),
                   pl.BlockSpec((1, n_heads, tm, QK_HEAD), lambda bi, i: (bi, 0, i, 0)),
                   pl.BlockSpec((1, n_heads, 1, V_MLA, tm), lambda bi, i: (bi, 0, i, 0, 0))],
        out_shape=[jax.ShapeDtypeStruct((b, n_heads, LANES, s), BF16),
                   jax.ShapeDtypeStruct((b, s, wd), BF16),
                   jax.ShapeDtypeStruct((b, n_heads, s // tm, DV_DIFF, tm), BF16),
                   jax.ShapeDtypeStruct((b, n_heads, QK_HEAD, s), BF16),
                   jax.ShapeDtypeStruct((b, n_heads, s, QK_HEAD), BF16),
                   jax.ShapeDtypeStruct((b, n_heads, s // tm, V_MLA, tm), BF16)],
        compiler_params=_cparams(("parallel", "parallel")),
        name="in_proj",
    )(x, sc1, sh1, *consts, cosq, sinq, cosk, sink)


def _t5_bucket(rel):
    nb = REL_BUCKETS // 2
    max_exact = nb // 2
    base = jnp.where(rel > 0, nb, 0)
    n = jnp.abs(rel)
    nf = jnp.maximum(n, 1).astype(jnp.float32)
    large = max_exact + (jnp.log(nf / max_exact) / math.log(REL_MAX_DIST / max_exact)
                         * (nb - max_exact)).astype(jnp.int32)
    large = jnp.minimum(large, nb - 1)
    return base + jnp.where(n < max_exact, n, large)


def _bias_kernel(rb_ref, bucket_ref, o_ref):
    hh = pl.program_id(0)
    t = o_ref.shape[-1]
    far_left = rb_ref[REL_BUCKETS // 2 - 1, hh]
    far_right = rb_ref[REL_BUCKETS - 1, hh]
    o_ref[0, 0] = jnp.full((t, t), far_left, F32) * LOG2E
    o_ref[0, 4] = jnp.full((t, t), far_right, F32) * LOG2E
    bk_strip = bucket_ref[...]
    strip = jnp.zeros(bk_strip.shape, F32)
    for bk in range(REL_BUCKETS):
        strip = jnp.where(bk_strip == bk, rb_ref[bk, hh], strip)
    strip = strip * LOG2E
    for d in (-1, 0, 1):
        diag = jnp.concatenate([strip[:, (2 - d) * t:(3 - d) * t], strip[:, (1 - d) * t:(2 - d) * t]], axis=1)
        rows = jnp.broadcast_to(diag, (t, 2 * t))
        o_ref[0, d + 2] = pltpu.roll(rows, 0, 1, stride=1, stride_axis=0)[:, :t]


def _bias_tiles(rel_bias, t):
    n_heads = rel_bias.shape[1]
    buckets = _t5_bucket(2 * t - jnp.arange(4 * t, dtype=jnp.int32))[None, :]
    return pl.pallas_call(
        _bias_kernel,
        grid=(n_heads,),
        in_specs=[pl.BlockSpec(memory_space=pltpu.SMEM),
                  pl.BlockSpec((1, 4 * t), lambda hh: (0, 0))],
        out_specs=pl.BlockSpec((1, 5, t, t), lambda hh: (hh, 0, 0, 0)),
        out_shape=jax.ShapeDtypeStruct((n_heads, 5, t, t), F32),
        compiler_params=_cparams(("arbitrary",)),
        name="rel_bias_tiles",
    )(rel_bias, buckets)


ONES_ROWS = 16


def _softmax_state_step(s, offset, vt, state):
    m_new, alpha, p = _softmax_weights(s, offset, state[0])
    return m_new, _accumulate(alpha, state[1], vt, p)


def _softmax_weights(s, offset, m_old):
    m_new = jnp.maximum(m_old, jnp.max(s, axis=0, keepdims=True) + offset)
    alpha = jnp.exp2(m_old - m_new)
    p = jnp.exp2(s - (m_new - offset)).astype(BF16)
    return m_new, alpha, p


def _accumulate(alpha, acc_old, vt, p):
    return alpha * acc_old + jnp.dot(vt, p, preferred_element_type=F32)


def _init_state(dv, t):
    return (jnp.full((1, t), -jnp.inf, F32), jnp.zeros((dv + ONES_ROWS, t), F32))


def _with_ones(vt):
    return jnp.concatenate([vt, jnp.ones((ONES_ROWS, vt.shape[1]), vt.dtype)], axis=0)


def _normalised(state, dv):
    _, acc = state
    return acc[:dv] / acc[dv:dv + 1]


def _diff_attn_kernel(rb_ref, lamv_ref, gsub_ref, qt_ref, k_ref, vt_ref, bias_ref, o_ref, *,
                      lambda_init, bounded):
    hh = pl.program_id(0)
    t = bias_ref.shape[-1]
    nq = qt_ref.shape[-1] // t
    nkt = vt_ref.shape[2]

    off_left = rb_ref[REL_BUCKETS // 2 - 1, hh] * LOG2E
    off_right = rb_ref[REL_BUCKETS - 1, hh] * LOG2E
    lv = lamv_ref[...]
    lam = (jnp.exp(jnp.sum(lv[0:1] * lv[1:2], axis=-1, keepdims=True))
           - jnp.exp(jnp.sum(lv[2:3] * lv[3:4], axis=-1, keepdims=True)) + lambda_init)

    n_near = min(nkt, 3)
    row = lax.broadcasted_iota(jnp.int32, (LANES, t), 0)
    zero = jnp.zeros((LANES, t), BF16)

    def query_maps(j):
        qt = qt_ref[0, 0, :, j * t:(j + 1) * t]
        return jnp.where(row < DH_DIFF, qt, zero), jnp.where(row < DH_DIFF, zero, qt)

    def tile_index(j, r):
        return lax.rem(pl.program_id(2) * nq + j - 1 + r + nkt, nkt)

    def far_offset(j, r):
        return jnp.where(tile_index(j, r) < pl.program_id(2) * nq + j, off_left, off_right)

    def biased_scores(j, qmap, r, k0, ck):
        kj = tile_index(j, r)
        k_t = k_ref[0, pl.ds(pl.multiple_of(kj * t + k0, ck), ck), :]
        s = jnp.dot(k_t, qmap, preferred_element_type=F32)
        saturated = (r >= n_near or (nkt >= 3 and r == 0 and k0 + ck - 1 - t <= -REL_MAX_DIST)
                     or (nkt >= 3 and r == 2 and k0 + 1 >= REL_MAX_DIST))
        if not (bounded and saturated) and r < n_near:
            s = s + bias_ref[0, jnp.clip(kj - (pl.program_id(2) * nq + j), -2, 2) + 2, k0:k0 + ck, :]
        elif bounded:
            s = s + far_offset(j, r)
        return s

    def finish(j, outs):
        o = outs[0] - lam * outs[1]
        ms = jnp.mean(o * o, axis=0, keepdims=True)
        y = o * lax.rsqrt(ms + EPS) * (gsub_ref[...] * (1.0 - lambda_init))
        o_ref[0, j * t:(j + 1) * t, :] = y.T.astype(BF16)

    if bounded:
        ck = min(t, KEY_CHUNK)
        npt = t // ck
        qmaps = [query_maps(j) for j in range(nq)]
        items = [(j, r, c, mp) for j in range(nq) for r in range(nkt) for c in range(npt) for mp in range(2)]
        score = lambda it: biased_scores(it[0], qmaps[it[0]][it[3]], it[1], it[2] * ck, ck)
        depth = min(BOUNDED_DEPTH, len(items))
        pending = [score(it) for it in items[:depth]]
        for n, (j, r, c, mp) in enumerate(items):
            if (r, c, mp) == (0, 0, 0):
                accs = [jnp.zeros((DV_DIFF, t), F32) for _ in range(2)]
                dens = [jnp.zeros((SUBLANES, t), F32) for _ in range(2)]
            s = pending.pop(0)
            if n + depth < len(items):
                pending.append(score(items[n + depth]))
            p = jnp.exp2(s)
            dens[mp] = dens[mp] + jnp.sum(p.reshape(ck // SUBLANES, SUBLANES, t), axis=0)
            accs[mp] = accs[mp] + jnp.dot(vt_ref[0, 0, tile_index(j, r), :, c * ck:(c + 1) * ck],
                                          p.astype(BF16), preferred_element_type=F32)
            if (r, c, mp) == (nkt - 1, npt - 1, 1):
                finish(j, [acc / jnp.sum(den, axis=0, keepdims=True) for acc, den in zip(accs, dens)])
    else:
        for j in range(nq):
            qmap = query_maps(j)
            states = [_init_state(DV_DIFF, t), _init_state(DV_DIFF, t)]
            score = lambda r: [biased_scores(j, qmap[mp], r, 0, t) for mp in range(2)]
            s_next = score(0)
            for r in range(nkt):
                s_cur = s_next
                if r + 1 < nkt:
                    s_next = score(r + 1)
                vt = _with_ones(vt_ref[0, 0, tile_index(j, r)])
                offset = 0.0 if r < n_near else far_offset(j, r)
                for mp in range(2):
                    states[mp] = _softmax_state_step(s_cur[mp], offset, vt, states[mp])
            finish(j, [_normalised(st, DV_DIFF) for st in states])


def _diff_attn(rel_bias, lamv, gsub_col, qt, k, vt, bias, *, lambda_init, bounded):
    b, n_heads, _, s = qt.shape
    t = bias.shape[-1]
    nkt, tk = vt.shape[2], vt.shape[4]
    assert tk == t
    tq = _pick_tile(s, t * (DIFF_BOUNDED_Q_TILES if bounded else 1))
    return pl.pallas_call(
        functools.partial(_diff_attn_kernel, lambda_init=lambda_init, bounded=bounded),
        grid=(n_heads, b, s // tq),
        in_specs=[pl.BlockSpec(memory_space=pltpu.SMEM),
                  pl.BlockSpec(lamv.shape, lambda hh, bi, qi: (0, 0)),
                  pl.BlockSpec(gsub_col.shape, lambda hh, bi, qi: (0, 0)),
                  pl.BlockSpec((1, 1, LANES, tq), lambda hh, bi, qi: (bi, hh, 0, qi)),
                  pl.BlockSpec((1, s, LANES), lambda hh, bi, qi: (bi, 0, hh)),
                  pl.BlockSpec((1, 1, nkt, DV_DIFF, tk), lambda hh, bi, qi: (bi, hh, 0, 0, 0)),
                  pl.BlockSpec((1, 5, t, t), lambda hh, bi, qi: (hh, 0, 0, 0))],
        out_specs=pl.BlockSpec((1, tq, DV_DIFF), lambda hh, bi, qi: (bi, qi, hh)),
        out_shape=jax.ShapeDtypeStruct((b, s, n_heads * DV_DIFF), BF16),
        compiler_params=_cparams(("parallel", "parallel", "parallel")),
        name="diff_attn",
    )(rel_bias, lamv, gsub_col, qt, k, vt, bias)


def _mla_attn_kernel(qt_ref, k_ref, vt_ref, o_ref):
    nkt = vt_ref.shape[2]
    tk = vt_ref.shape[-1]
    qt = qt_ref[0, 0]
    m, acc = _init_state(V_MLA, qt.shape[-1])
    scores = lambda kj: jnp.dot(k_ref[0, 0, kj * tk:(kj + 1) * tk, :], qt, preferred_element_type=F32)
    depth = 2
    pending = [scores(kj) for kj in range(min(depth, nkt))]
    for kj in range(nkt):
        s = pending.pop(0)
        if kj + depth < nkt:
            pending.append(scores(kj + depth))
        m, alpha, p = _softmax_weights(s, 0.0, m)
        acc = _accumulate(alpha, acc, _with_ones(vt_ref[0, 0, kj]), p)
    o_ref[0] = _normalised((m, acc), V_MLA).T.astype(BF16)


def _mla_attn_bounded_kernel(qt_ref, k_ref, vt_ref, o_ref):
    nkt = vt_ref.shape[2]
    tk = vt_ref.shape[-1]
    t = min(qt_ref.shape[-1], MLA_TQ)
    nq = qt_ref.shape[-1] // t
    ck = min(tk, KEY_CHUNK)
    npt = tk // ck
    n_chunks = nkt * npt
    items = [(j, i) for j in range(nq) for i in range(n_chunks)]

    def scores(item):
        j, i = item
        return jnp.dot(k_ref[0, 0, i * ck:(i + 1) * ck, :], qt_ref[0, 0, :, j * t:(j + 1) * t],
                       preferred_element_type=F32)

    depth = min(BOUNDED_DEPTH, len(items))
    pending = [scores(it) for it in items[:depth]]
    for n, (j, i) in enumerate(items):
        if i == 0:
            acc = jnp.zeros((V_MLA, t), F32)
            den = jnp.zeros((SUBLANES, t), F32)
        s = pending.pop(0)
        if n + depth < len(items):
            pending.append(scores(items[n + depth]))
        c = i % npt
        p = jnp.exp2(s)
        den = den + jnp.sum(p.reshape(ck // SUBLANES, SUBLANES, t), axis=0)
        acc = acc + jnp.dot(vt_ref[0, 0, i // npt, :, c * ck:(c + 1) * ck], p.astype(BF16),
                            preferred_element_type=F32)
        if i == n_chunks - 1:
            o_ref[0, j * t:(j + 1) * t, :] = (acc / jnp.sum(den, axis=0, keepdims=True)).T.astype(BF16)


def _mla_attn(qt, k, vt, *, bounded):
    b, n_heads, _, s = qt.shape
    t = _pick_tile(s, MLA_TQ * (MLA_BOUNDED_Q_TILES if bounded else 1))
    nkt, tk = vt.shape[2], vt.shape[4]
    return pl.pallas_call(
        _mla_attn_bounded_kernel if bounded else _mla_attn_kernel,
        grid=(b, n_heads, s // t),
        in_specs=[pl.BlockSpec((1, 1, QK_HEAD, t), lambda bi, hh, qi: (bi, hh, 0, qi)),
                  pl.BlockSpec((1, 1, s, QK_HEAD), lambda bi, hh, qi: (bi, hh, 0, 0)),
                  pl.BlockSpec((1, 1, nkt, V_MLA, tk), lambda bi, hh, qi: (bi, hh, 0, 0, 0))],
        out_specs=pl.BlockSpec((1, t, V_MLA), lambda bi, hh, qi: (bi, qi, hh)),
        out_shape=jax.ShapeDtypeStruct((b, s, n_heads * V_MLA), BF16),
        compiler_params=_cparams(("parallel", "parallel", "parallel")),
        name="mla_attn",
    )(qt, k, vt)


def _out_proj_kernel(x_ref, gt_ref, a_ref, b_ref, wa_ref, wb_ref, sc_ref, sh_ref, g2_ref, o_ref, h_ref):
    tm = x_ref.shape[1]
    rows = tm // OUT_PROJ_SPLIT
    mixes = []
    for i in range(OUT_PROJ_SPLIT):
        sl = slice(i * rows, (i + 1) * rows)
        mixes.append(jnp.dot(a_ref[0, sl, :], wa_ref[...], preferred_element_type=F32)
                     + jnp.dot(b_ref[0, sl, :], wb_ref[...], preferred_element_type=F32))
    for i in range(OUT_PROJ_SPLIT):
        sl = slice(i * rows, (i + 1) * rows)
        x1 = x_ref[0, sl, :] + gt_ref[0] * mixes[i]
        o_ref[0, sl, :] = x1
        h_ref[0, sl, :] = _modulated_norm(x1, g2_ref[...], sc_ref[0], sh_ref[0]).astype(BF16)


def _out_proj(x, gt1, a, bb, wa, wb, sc2, sh2, g2, *, tm):
    b, s, d = x.shape
    row = lambda w: pl.BlockSpec((1, tm, w), lambda bi, i: (bi, i, 0))
    mod = pl.BlockSpec((1, 1, d), lambda bi, i: (bi, 0, 0))
    return pl.pallas_call(
        _out_proj_kernel,
        grid=(b, s // tm),
        in_specs=[row(d), mod, row(a.shape[-1]), row(bb.shape[-1]), _const_spec(wa.shape),
                  _const_spec(wb.shape), mod, mod, _const_spec(g2.shape)],
        out_specs=[row(d), row(d)],
        out_shape=[jax.ShapeDtypeStruct((b, s, d), F32), jax.ShapeDtypeStruct((b, s, d), BF16)],
        compiler_params=_cparams(("parallel", "parallel")),
        name="out_proj",
    )(x, gt1, a, bb, wa, wb, sc2, sh2, g2)


def _ffn_kernel(x_ref, h_ref, gt_ref, wg_ref, wu_ref, wd_ref, o_ref, xs_ref, *, n_x_chunks):
    j = pl.program_id(2)
    rows = x_ref.shape[1]

    @pl.when(j == 0)
    def _():
        o_ref[...] = jnp.zeros(o_ref.shape, F32)

    chunk = jnp.minimum(j, n_x_chunks - 1)
    xs_ref[pl.ds(pl.multiple_of(chunk * rows, rows), rows), :] = x_ref[0]

    h = h_ref[0]
    tf = wg_ref.shape[1]
    cf = tf // FFN_COL_SPLIT
    gates = []
    for c in range(FFN_COL_SPLIT):
        cs = slice(c * cf, (c + 1) * cf)
        gates.append((jnp.dot(h, wg_ref[:, cs], preferred_element_type=F32),
                      jnp.dot(h, wu_ref[:, cs], preferred_element_type=F32)))
    part = None
    for c, (g, u) in enumerate(gates):
        act = (g * jax.nn.sigmoid(g) * u).astype(BF16)
        d = jnp.dot(act, wd_ref[c * cf:(c + 1) * cf, :], preferred_element_type=F32)
        part = d if part is None else part + d
    o_ref[0] += part

    @pl.when(j == pl.num_programs(2) - 1)
    def _():
        o_ref[0] = xs_ref[...] + gt_ref[0] * o_ref[0]


def _ffn(x, h2, gt2, wg, wu, wd, *, tm, tf):
    b, s, d = x.shape
    dff = wg.shape[1]
    nj = dff // tf
    nxc = 1
    while nxc * 2 <= nj and (tm // (nxc * 2)) % SUBLANES == 0:
        nxc *= 2
    row = pl.BlockSpec((1, tm, d), lambda bi, i, j: (bi, i, 0))
    x_chunk = pl.BlockSpec((1, tm // nxc, d), lambda bi, i, j: (bi, i * nxc + jnp.minimum(j, nxc - 1), 0))
    return pl.pallas_call(
        functools.partial(_ffn_kernel, n_x_chunks=nxc),
        grid=(b, s // tm, nj),
        in_specs=[x_chunk, row, pl.BlockSpec((1, 1, d), lambda bi, i, j: (bi, 0, 0)),
                  pl.BlockSpec((d, tf), lambda bi, i, j: (0, j)),
                  pl.BlockSpec((d, tf), lambda bi, i, j: (0, j)),
                  pl.BlockSpec((tf, d), lambda bi, i, j: (j, 0))],
        out_specs=row,
        out_shape=jax.ShapeDtypeStruct((b, s, d), F32),
        scratch_shapes=[pltpu.VMEM((tm, d), F32)],
        compiler_params=_cparams(("parallel", "parallel", "arbitrary")),
        name="ffn",
    )(x, h2, gt2, wg, wu, wd)


def _rope_tables(s):
    pos = jnp.arange(s, dtype=jnp.float32)
    inv = 1.0 / (ROPE_THETA ** (jnp.arange(0, QK_ROPE, 2, dtype=jnp.float32) / QK_ROPE))
    ang = pos[:, None] * inv[None, :]
    return jnp.cos(ang), jnp.sin(ang)


def _pick_tile(n, pref):
    t = min(pref, n)
    while n % t:
        t //= 2
    return t


def kernel(x, c, rel_bias, w_ada, b_ada, g_norm1, w_in, g_q_diff, g_k_diff, lambda_vecs, g_subln, g_q_a, w_q_b, g_kv_a, w_kv_b, g_q_mla, g_k_mla, w_out, g_norm2, w_gate, w_up, w_down):
    b, s, d = x.shape
    depth = w_ada.shape[0]
    diff_width = d // 2
    n_hd = diff_width // DV_DIFF
    n_hm = (d - diff_width) // V_MLA
    assert n_hd == n_hm and n_hd % 2 == 0
    n_heads = n_hd
    wd = n_heads * LANES
    half = QK_ROPE // 2
    tk = _pick_tile(s, ATT_TK)
    nkt = s // tk
    t = tk
    assert t + 1 >= REL_MAX_DIST
    tm = _pick_tile(s, 512)

    cos, sin = _rope_tables(s)
    cosq, sinq = jnp.tile(cos, (1, n_heads)), jnp.tile(sin, (1, n_heads))
    cosk = jnp.tile(cos, (1, 4))
    sink = jnp.tile(jnp.concatenate([-sin, sin], axis=1), (1, 2))

    c_pad = jnp.pad(c, ((0, (-b) % 8), (0, 0)))

    for l in range(depth):
        lambda_init = 0.8 - 0.6 * math.exp(-0.3 * l)
        mod = _ada(c_pad, w_ada[l], b_ada[l][None, :])[:b]
        sh1, sc1, gt1, sh2, sc2, gt2 = [m[:, None, :] for m in jnp.split(mod, 6, axis=-1)]

        wi = w_in[l]
        o = 3 * wd + Q_LORA + KV_LORA
        w_kpe = wi[:, o:o + QK_ROPE]
        w_kpe_sw = jnp.concatenate([w_kpe[:, half:], w_kpe[:, :half]], axis=1)
        wqkv = wi[:, :o].astype(BF16)
        wlat = jnp.concatenate([w_kpe, w_kpe, w_kpe_sw, w_kpe_sw], axis=1).astype(BF16)
        wq = w_q_b[l].reshape(Q_LORA, n_heads, QK_HEAD)
        wqb = jnp.concatenate([wq[:, :, :QK_NOPE].reshape(Q_LORA, -1),
                               wq[:, :, QK_NOPE:QK_NOPE + half].reshape(Q_LORA, -1),
                               wq[:, :, QK_NOPE + half:].reshape(Q_LORA, -1)], axis=1).astype(BF16)
        wkv = w_kv_b[l].reshape(KV_LORA, n_heads, QK_NOPE + V_MLA)
        wkvb = jnp.concatenate([wkv[:, :, :QK_NOPE].reshape(KV_LORA, -1),
                                wkv[:, :, QK_NOPE:].reshape(KV_LORA, -1)], axis=1).astype(BF16)
        gq, gk = g_q_mla[l], g_k_mla[l]
        gk_pe = gk[QK_NOPE:]
        gk_pe_sw = jnp.concatenate([gk_pe[half:], gk_pe[:half]])

        qt_d, kd, vt_d, qt_m, k_m, vt_m = _in_proj(
            x, sc1, sh1, g_norm1[l][None, :], wqkv, wlat,
            jnp.tile(g_q_diff[l], 2)[None, :], jnp.tile(g_k_diff[l], 2)[None, :],
            g_q_a[l][None, :], wqb, g_kv_a[l][None, :], wkvb,
            gq[None, :QK_NOPE], jnp.tile(gq[QK_NOPE:QK_NOPE + half], n_heads)[None, :],
            jnp.tile(gq[QK_NOPE + half:], n_heads)[None, :],
            gk[None, :QK_NOPE], jnp.tile(gk_pe, 2)[None, :], jnp.tile(gk_pe_sw, 2)[None, :],
            cosq, sinq, cosk, sink, n_heads=n_heads, tm=tk)

        bias = _bias_tiles(rel_bias, t)
        gmax = lambda g: jnp.max(jnp.abs(g))
        bound_d = (NORM_SLACK * LOG2E * DH_DIFF ** 0.5 * gmax(g_q_diff[l]) * gmax(g_k_diff[l])
                   + LOG2E * gmax(rel_bias))
        bound_m = NORM_SLACK * LOG2E * QK_HEAD ** 0.5 * gmax(g_q_mla[l]) * gmax(g_k_mla[l])
        diff_args = (rel_bias, lambda_vecs[l], g_subln[l][:, None], qt_d, kd, vt_d, bias)
        a_out = lax.cond(
            bound_d <= SCORE_BOUND,
            lambda *a: _diff_attn(*a, lambda_init=lambda_init, bounded=True),
            lambda *a: _diff_attn(*a, lambda_init=lambda_init, bounded=False), *diff_args)
        b_out = lax.cond(
            bound_m <= SCORE_BOUND,
            lambda *a: _mla_attn(*a, bounded=True),
            lambda *a: _mla_attn(*a, bounded=False), qt_m, k_m, vt_m)

        wo = w_out[l].astype(BF16)
        x, h2 = _out_proj(x, gt1, a_out, b_out, wo[:diff_width], wo[diff_width:], sc2, sh2,
                          g_norm2[l][None, :], tm=tm)
        x = _ffn(x, h2, gt2, w_gate[l].astype(BF16), w_up[l].astype(BF16), w_down[l].astype(BF16),
                 tm=_pick_tile(s, FFN_TM), tf=_pick_tile(w_gate.shape[-1], FFN_TF))
    return x
```

```python
import functools
import math

import jax
import jax.numpy as jnp
from jax import lax
from jax.experimental import pallas as pl
from jax.experimental.pallas import tpu as pltpu

F32 = jnp.float32
BF16 = jnp.bfloat16

DH_DIFF = 64
DV_DIFF = 2 * DH_DIFF
QK_NOPE = 128
QK_ROPE = 64
QK_HEAD = QK_NOPE + QK_ROPE
V_MLA = 128
Q_LORA = 512
KV_LORA = 256
ROPE_THETA = 10000.0
REL_BUCKETS = 32
REL_MAX_DIST = 128
EPS = 1e-6
LOG2E = 1.4426950408889634

LANES = 128
SUBLANES = 8
VMEM_LIMIT_BYTES = 56 * 1024 * 1024

OUT_PROJ_SPLIT = 4
IN_PROJ_SPLIT = 2
FFN_TM = 1024
FFN_TF = 512
FFN_COL_SPLIT = 2
ATT_TK = 512
MLA_TQ = 512
KEY_CHUNK = 256
BOUNDED_DEPTH = 2
DIFF_BOUNDED_Q_TILES = 8
MLA_BOUNDED_Q_TILES = 8
SCORE_BOUND = 40.0
NORM_SLACK = 1.02


def _cparams(sem):
    return pltpu.CompilerParams(dimension_semantics=sem, vmem_limit_bytes=VMEM_LIMIT_BYTES)


def _const_spec(shape):
    nd = len(shape)
    return pl.BlockSpec(shape, lambda *_: (0,) * nd, pipeline_mode=pl.Buffered(1))


def _ada_kernel(c_ref, w_ref, b_ref, o_ref):
    c = c_ref[...]
    ca = c * jax.nn.sigmoid(c)
    o_ref[...] = jnp.dot(ca.astype(BF16), w_ref[...].astype(BF16), preferred_element_type=F32) + b_ref[...]


def _ada(c_pad, w, b, tn=1024):
    m, d = c_pad.shape
    n = w.shape[1]
    return pl.pallas_call(
        _ada_kernel,
        grid=(n // tn,),
        in_specs=[pl.BlockSpec((m, d), lambda j: (0, 0)),
                  pl.BlockSpec((d, tn), lambda j: (0, j)),
                  pl.BlockSpec((1, tn), lambda j: (0, j))],
        out_specs=pl.BlockSpec((m, tn), lambda j: (0, j)),
        out_shape=jax.ShapeDtypeStruct((m, n), F32),
        compiler_params=_cparams(("arbitrary",)),
        name="ada",
    )(c_pad, w, b)


def _modulated_norm(x, g, sc, sh):
    ms = jnp.mean(x * x, axis=-1, keepdims=True)
    return (x * lax.rsqrt(ms + EPS) * g) * (1.0 + sc) + sh


def _half_lane_norm(blk, g2, lo_mask, out_scale):
    sq = blk * blk
    s_lo = jnp.sum(jnp.where(lo_mask, sq, 0.0), axis=-1, keepdims=True)
    s_hi = jnp.sum(jnp.where(lo_mask, 0.0, sq), axis=-1, keepdims=True)
    inv = jnp.where(lo_mask, lax.rsqrt(s_lo * (1.0 / DH_DIFF) + EPS),
                    lax.rsqrt(s_hi * (1.0 / DH_DIFF) + EPS))
    return blk * inv * (g2 * out_scale)


def _in_proj_kernel(x_ref, sc_ref, sh_ref, g1_ref, wqkv_ref, wlat_ref, gqd_ref, gkd_ref,
                    gqa_ref, wqb_ref, gkva_ref, wkvb_ref, gqn_ref, gq1_ref, gq2_ref,
                    gkn_ref, gkp_ref, gkps_ref, cosq_ref, sinq_ref, cosk_ref, sink_ref,
                    qtd_ref, kd_ref, vtd_ref, qtm_ref, km_ref, vtm_ref, *, n_heads):
    tm = x_ref.shape[1]
    rows = tm // IN_PROJ_SPLIT
    wd = n_heads * LANES
    half = QK_ROPE // 2
    wx = n_heads * half
    lo_mask = lax.broadcasted_iota(jnp.int32, (rows, LANES), 1) < DH_DIFF
    lane_x = lax.broadcasted_iota(jnp.int32, (rows, wx), 1) // half
    qscale = DH_DIFF ** -0.5 * LOG2E
    mscale = QK_HEAD ** -0.5 * LOG2E

    def rms(v, g):
        return v * lax.rsqrt(jnp.mean(v * v, axis=-1, keepdims=True) + EPS) * g

    for sb in range(IN_PROJ_SPLIT):
        rs = slice(sb * rows, (sb + 1) * rows)
        h = _modulated_norm(x_ref[0, rs, :], g1_ref[...], sc_ref[0], sh_ref[0]).astype(BF16)
        lat = jnp.dot(h, wqkv_ref[:, 3 * wd:3 * wd + Q_LORA + KV_LORA], preferred_element_type=F32)
        cq = lat[:, 0:Q_LORA]
        ckv = lat[:, Q_LORA:Q_LORA + KV_LORA]
        kpe4 = jnp.dot(h, wlat_ref[...], preferred_element_type=F32)
        kpe2 = kpe4[:, 0:LANES]
        kpes2 = kpe4[:, LANES:2 * LANES]

        vd = jnp.dot(h, wqkv_ref[:, 2 * wd:3 * wd], preferred_element_type=F32)
        for hh in range(n_heads):
            vtd_ref[0, hh, 0, :, rs] = vd[:, hh * LANES:(hh + 1) * LANES].T.astype(BF16)

        kv = jnp.dot(rms(ckv, gkva_ref[...]).astype(BF16), wkvb_ref[...], preferred_element_type=F32)
        for hh in range(n_heads):
            vtm_ref[0, hh, 0, :, rs] = kv[:, wd + hh * LANES:wd + (hh + 1) * LANES].T.astype(BF16)
        ss_pe = jnp.sum(jnp.where(lo_mask, kpe2 * kpe2, 0.0), axis=-1, keepdims=True)
        kr2 = kpe2 * gkp_ref[...] * cosk_ref[rs, :] + kpes2 * gkps_ref[...] * sink_ref[rs, :]
        for hh in range(n_heads):
            sl = slice(hh * LANES, (hh + 1) * LANES)
            kn = kv[:, sl]
            ss = jnp.sum(kn * kn, axis=-1, keepdims=True) + ss_pe
            r = lax.rsqrt(ss * (1.0 / QK_HEAD) + EPS)
            km_ref[0, hh, rs, 0:QK_NOPE] = (kn * r * gkn_ref[...]).astype(BF16)
            km_ref[0, hh, rs, QK_NOPE:QK_HEAD] = (kr2 * r)[:, 0:QK_ROPE].astype(BF16)

        qd = jnp.dot(h, wqkv_ref[:, 0:wd], preferred_element_type=F32)
        for hh in range(n_heads):
            sl = slice(hh * LANES, (hh + 1) * LANES)
            qtd_ref[0, hh, :, rs] = _half_lane_norm(qd[:, sl], gqd_ref[...], lo_mask, qscale).T.astype(BF16)

        qm = jnp.dot(rms(cq, gqa_ref[...]).astype(BF16), wqb_ref[...], preferred_element_type=F32)
        x1 = qm[:, wd:wd + wx]
        x2 = qm[:, wd + wx:wd + 2 * wx]
        sq_x = x1 * x1 + x2 * x2
        inv_x = jnp.zeros((rows, wx), F32)
        for hh in range(n_heads):
            sl = slice(hh * LANES, (hh + 1) * LANES)
            qn = qm[:, sl]
            ss = (jnp.sum(qn * qn, axis=-1, keepdims=True)
                  + jnp.sum(jnp.where(lane_x == hh, sq_x, 0.0), axis=-1, keepdims=True))
            r = lax.rsqrt(ss * (1.0 / QK_HEAD) + EPS) * mscale
            qtm_ref[0, hh, 0:QK_NOPE, rs] = (qn * r * gqn_ref[...]).T.astype(BF16)
            inv_x = jnp.where(lane_x == hh, r, inv_x)
        a1 = x1 * gq1_ref[...]
        a2 = x2 * gq2_ref[...]
        cq_t = cosq_ref[rs, :]
        sq_t = sinq_ref[rs, :]
        x1t = ((a1 * cq_t - a2 * sq_t) * inv_x).T.astype(BF16)
        x2t = ((a2 * cq_t + a1 * sq_t) * inv_x).T.astype(BF16)
        for hh in range(n_heads):
            qtm_ref[0, hh, QK_NOPE:QK_NOPE + half, rs] = x1t[hh * half:(hh + 1) * half]
            qtm_ref[0, hh, QK_NOPE + half:QK_HEAD, rs] = x2t[hh * half:(hh + 1) * half]

        kd = jnp.dot(h, wqkv_ref[:, wd:2 * wd], preferred_element_type=F32)
        for hh in range(n_heads):
            sl = slice(hh * LANES, (hh + 1) * LANES)
            kd_ref[0, rs, sl] = _half_lane_norm(kd[:, sl], gkd_ref[...], lo_mask, 1.0).astype(BF16)


def _in_proj(x, sc1, sh1, g1, wqkv, wlat, gqd, gkd, gqa, wqb, gkva, wkvb, gqn, gq1, gq2,
             gkn, gkp, gkps, cosq, sinq, cosk, sink, *, n_heads, tm):
    b, s, d = x.shape
    wd = n_heads * LANES
    wx = n_heads * (QK_ROPE // 2)
    row = lambda w: pl.BlockSpec((1, tm, w), lambda bi, i: (bi, i, 0))
    mod = pl.BlockSpec((1, 1, d), lambda bi, i: (bi, 0, 0))
    tab = lambda w: pl.BlockSpec((tm, w), lambda bi, i: (i, 0))
    consts = [g1, wqkv, wlat, gqd, gkd, gqa, wqb, gkva, wkvb, gqn, gq1, gq2, gkn, gkp, gkps]
    return pl.pallas_call(
        functools.partial(_in_proj_kernel, n_heads=n_heads),
        grid=(b, s // tm),
        in_specs=[row(d), mod, mod] + [_const_spec(a.shape) for a in consts]
                 + [tab(wx), tab(wx), tab(LANES), tab(LANES)],
        out_specs=[pl.BlockSpec((1, n_heads, LANES, tm), lambda bi, i: (bi, 0, 0, i)),
                   row(wd),
                   pl.BlockSpec((1, n_heads, 1, DV_DIFF, tm), lambda bi, i: (bi, 0, i, 0, 0)),
                   pl.BlockSpec((1, n_heads, QK_HEAD, tm), lambda bi, i: (bi, 0, 0, i)),
                   pl.BlockSpec((1, n_heads, tm, QK_HEAD), lambda bi, i: (bi, 0, i, 0)),
                   pl.BlockSpec((1, n_heads, 1, V_MLA, tm), lambda bi, i: (bi, 0, i, 0, 0))],
        out_shape=[jax.ShapeDtypeStruct((b, n_heads, LANES, s), BF16),
                   jax.ShapeDtypeStruct((b, s, wd), BF16),
                   jax.ShapeDtypeStruct((b, n_heads, s // tm, DV_DIFF, tm), BF16),
                   jax.ShapeDtypeStruct((b, n_heads, QK_HEAD, s), BF16),
                   jax.ShapeDtypeStruct((b, n_heads, s, QK_HEAD), BF16),
                   jax.ShapeDtypeStruct((b, n_heads, s // tm, V_MLA, tm), BF16)],
        compiler_params=_cparams(("parallel", "parallel")),
        name="in_proj",
    )(x, sc1, sh1, *consts, cosq, sinq, cosk, sink)


def _t5_bucket(rel):
    nb = REL_BUCKETS // 2
    max_exact = nb // 2
    base = jnp.where(rel > 0, nb, 0)
    n = jnp.abs(rel)
    nf = jnp.maximum(n, 1).astype(jnp.float32)
    large = max_exact + (jnp.log(nf / max_exact) / math.log(REL_MAX_DIST / max_exact)
                         * (nb - max_exact)).astype(jnp.int32)
    large = jnp.minimum(large, nb - 1)
    return base + jnp.where(n < max_exact, n, large)


def _bias_kernel(rb_ref, bucket_ref, o_ref):
    hh = pl.program_id(0)
    t = o_ref.shape[-1]
    far_left = rb_ref[REL_BUCKETS // 2 - 1, hh]
    far_right = rb_ref[REL_BUCKETS - 1, hh]
    o_ref[0, 0] = jnp.full((t, t), far_left, F32) * LOG2E
    o_ref[0, 4] = jnp.full((t, t), far_right, F32) * LOG2E
    bk_strip = bucket_ref[...]
    strip = jnp.zeros(bk_strip.shape, F32)
    for bk in range(REL_BUCKETS):
        strip = jnp.where(bk_strip == bk, rb_ref[bk, hh], strip)
    strip = strip * LOG2E
    for d in (-1, 0, 1):
        diag = jnp.concatenate([strip[:, (2 - d) * t:(3 - d) * t], strip[:, (1 - d) * t:(2 - d) * t]], axis=1)
        rows = jnp.broadcast_to(diag, (t, 2 * t))
        o_ref[0, d + 2] = pltpu.roll(rows, 0, 1, stride=1, stride_axis=0)[:, :t]


def _bias_tiles(rel_bias, t):
    n_heads = rel_bias.shape[1]
    buckets = _t5_bucket(2 * t - jnp.arange(4 * t, dtype=jnp.int32))[None, :]
    return pl.pallas_call(
        _bias_kernel,
        grid=(n_heads,),
        in_specs=[pl.BlockSpec(memory_space=pltpu.SMEM),
                  pl.BlockSpec((1, 4 * t), lambda hh: (0, 0))],
        out_specs=pl.BlockSpec((1, 5, t, t), lambda hh: (hh, 0, 0, 0)),
        out_shape=jax.ShapeDtypeStruct((n_heads, 5, t, t), F32),
        compiler_params=_cparams(("arbitrary",)),
        name="rel_bias_tiles",
    )(rel_bias, buckets)


ONES_ROWS = 16


def _softmax_state_step(s, offset, vt, state):
    m_new, alpha, p = _softmax_weights(s, offset, state[0])
    return m_new, _accumulate(alpha, state[1], vt, p)


def _softmax_weights(s, offset, m_old):
    m_new = jnp.maximum(m_old, jnp.max(s, axis=0, keepdims=True) + offset)
    alpha = jnp.exp2(m_old - m_new)
    p = jnp.exp2(s - (m_new - offset)).astype(BF16)
    return m_new, alpha, p


def _accumulate(alpha, acc_old, vt, p):
    return alpha * acc_old + jnp.dot(vt, p, preferred_element_type=F32)


def _init_state(dv, t):
    return (jnp.full((1, t), -jnp.inf, F32), jnp.zeros((dv + ONES_ROWS, t), F32))


def _with_ones(vt):
    return jnp.concatenate([vt, jnp.ones((ONES_ROWS, vt.shape[1]), vt.dtype)], axis=0)


def _normalised(state, dv):
    _, acc = state
    return acc[:dv] / acc[dv:dv + 1]


def _diff_attn_kernel(rb_ref, lamv_ref, gsub_ref, qt_ref, k_ref, vt_ref, bias_ref, o_ref, *,
                      lambda_init, bounded):
    hh = pl.program_id(0)
    t = bias_ref.shape[-1]
    nq = qt_ref.shape[-1] // t
    nkt = vt_ref.shape[2]

    off_left = rb_ref[REL_BUCKETS // 2 - 1, hh] * LOG2E
    off_right = rb_ref[REL_BUCKETS - 1, hh] * LOG2E
    lv = lamv_ref[...]
    lam = (jnp.exp(jnp.sum(lv[0:1] * lv[1:2], axis=-1, keepdims=True))
           - jnp.exp(jnp.sum(lv[2:3] * lv[3:4], axis=-1, keepdims=True)) + lambda_init)

    n_near = min(nkt, 3)
    row = lax.broadcasted_iota(jnp.int32, (LANES, t), 0)
    zero = jnp.zeros((LANES, t), BF16)

    def query_maps(j):
        qt = qt_ref[0, 0, :, j * t:(j + 1) * t]
        return jnp.where(row < DH_DIFF, qt, zero), jnp.where(row < DH_DIFF, zero, qt)

    def tile_index(j, r):
        return lax.rem(pl.program_id(2) * nq + j - 1 + r + nkt, nkt)

    def far_offset(j, r):
        return jnp.where(tile_index(j, r) < pl.program_id(2) * nq + j, off_left, off_right)

    def biased_scores(j, qmap, r, k0, ck):
        kj = tile_index(j, r)
        k_t = k_ref[0, pl.ds(pl.multiple_of(kj * t + k0, ck), ck), :]
        s = jnp.dot(k_t, qmap, preferred_element_type=F32)
        saturated = (r >= n_near or (nkt >= 3 and r == 0 and k0 + ck - 1 - t <= -REL_MAX_DIST)
                     or (nkt >= 3 and r == 2 and k0 + 1 >= REL_MAX_DIST))
        if not (bounded and saturated) and r < n_near:
            s = s + bias_ref[0, jnp.clip(kj - (pl.program_id(2) * nq + j), -2, 2) + 2, k0:k0 + ck, :]
        elif bounded:
            s = s + far_offset(j, r)
        return s

    def finish(j, outs):
        o = outs[0] - lam * outs[1]
        ms = jnp.mean(o * o, axis=0, keepdims=True)
        y = o * lax.rsqrt(ms + EPS) * (gsub_ref[...] * (1.0 - lambda_init))
        o_ref[0, j * t:(j + 1) * t, :] = y.T.astype(BF16)

    if bounded:
        ck = min(t, KEY_CHUNK)
        npt = t // ck
        qmaps = [query_maps(j) for j in range(nq)]
        items = [(j, r, c, mp) for j in range(nq) for r in range(nkt) for c in range(npt) for mp in range(2)]
        score = lambda it: biased_scores(it[0], qmaps[it[0]][it[3]], it[1], it[2] * ck, ck)
        depth = min(BOUNDED_DEPTH, len(items))
        pending = [score(it) for it in items[:depth]]
        for n, (j, r, c, mp) in enumerate(items):
            if (r, c, mp) == (0, 0, 0):
                accs = [jnp.zeros((DV_DIFF, t), F32) for _ in range(2)]
                dens = [jnp.zeros((SUBLANES, t), F32) for _ in range(2)]
            s = pending.pop(0)
            if n + depth < len(items):
                pending.append(score(items[n + depth]))
            p = jnp.exp2(s)
            dens[mp] = dens[mp] + jnp.sum(p.reshape(ck // SUBLANES, SUBLANES, t), axis=0)
            accs[mp] = accs[mp] + jnp.dot(vt_ref[0, 0, tile_index(j, r), :, c * ck:(c + 1) * ck],
                                          p.astype(BF16), preferred_element_type=F32)
            if (r, c, mp) == (nkt - 1, npt - 1, 1):
                finish(j, [acc / jnp.sum(den, axis=0, keepdims=True) for acc, den in zip(accs, dens)])
    else:
        for j in range(nq):
            qmap = query_maps(j)
            states = [_init_state(DV_DIFF, t), _init_state(DV_DIFF, t)]
            score = lambda r: [biased_scores(j, qmap[mp], r, 0, t) for mp in range(2)]
            s_next = score(0)
            for r in range(nkt):
                s_cur = s_next
                if r + 1 < nkt:
                    s_next = score(r + 1)
                vt = _with_ones(vt_ref[0, 0, tile_index(j, r)])
                offset = 0.0 if r < n_near else far_offset(j, r)
                for mp in range(2):
                    states[mp] = _softmax_state_step(s_cur[mp], offset, vt, states[mp])
            finish(j, [_normalised(st, DV_DIFF) for st in states])


def _diff_attn(rel_bias, lamv, gsub_col, qt, k, vt, bias, *, lambda_init, bounded):
    b, n_heads, _, s = qt.shape
    t = bias.shape[-1]
    nkt, tk = vt.shape[2], vt.shape[4]
    assert tk == t
    tq = _pick_tile(s, t * (DIFF_BOUNDED_Q_TILES if bounded else 1))
    return pl.pallas_call(
        functools.partial(_diff_attn_kernel, lambda_init=lambda_init, bounded=bounded),
        grid=(n_heads, b, s // tq),
        in_specs=[pl.BlockSpec(memory_space=pltpu.SMEM),
                  pl.BlockSpec(lamv.shape, lambda hh, bi, qi: (0, 0)),
                  pl.BlockSpec(gsub_col.shape, lambda hh, bi, qi: (0, 0)),
                  pl.BlockSpec((1, 1, LANES, tq), lambda hh, bi, qi: (bi, hh, 0, qi)),
                  pl.BlockSpec((1, s, LANES), lambda hh, bi, qi: (bi, 0, hh)),
                  pl.BlockSpec((1, 1, nkt, DV_DIFF, tk), lambda hh, bi, qi: (bi, hh, 0, 0, 0)),
                  pl.BlockSpec((1, 5, t, t), lambda hh, bi, qi: (hh, 0, 0, 0))],
        out_specs=pl.BlockSpec((1, tq, DV_DIFF), lambda hh, bi, qi: (bi, qi, hh)),
        out_shape=jax.ShapeDtypeStruct((b, s, n_heads * DV_DIFF), BF16),
        compiler_params=_cparams(("parallel", "parallel", "parallel")),
        name="diff_attn",
    )(rel_bias, lamv, gsub_col, qt, k, vt, bias)


def _mla_attn_kernel(qt_ref, k_ref, vt_ref, o_ref):
    nkt = vt_ref.shape[2]
    tk = vt_ref.shape[-1]
    qt = qt_ref[0, 0]
    m, acc = _init_state(V_MLA, qt.shape[-1])
    scores = lambda kj: jnp.dot(k_ref[0, 0, kj * tk:(kj + 1) * tk, :], qt, preferred_element_type=F32)
    depth = 2
    pending = [scores(kj) for kj in range(min(depth, nkt))]
    for kj in range(nkt):
        s = pending.pop(0)
        if kj + depth < nkt:
            pending.append(scores(kj + depth))
        m, alpha, p = _softmax_weights(s, 0.0, m)
        acc = _accumulate(alpha, acc, _with_ones(vt_ref[0, 0, kj]), p)
    o_ref[0] = _normalised((m, acc), V_MLA).T.astype(BF16)


def _mla_attn_bounded_kernel(qt_ref, k_ref, vt_ref, o_ref):
    nkt = vt_ref.shape[2]
    tk = vt_ref.shape[-1]
    t = min(qt_ref.shape[-1], MLA_TQ)
    nq = qt_ref.shape[-1] // t
    ck = min(tk, KEY_CHUNK)
    npt = tk // ck
    n_chunks = nkt * npt
    items = [(j, i) for j in range(nq) for i in range(n_chunks)]

    def scores(item):
        j, i = item
        return jnp.dot(k_ref[0, 0, i * ck:(i + 1) * ck, :], qt_ref[0, 0, :, j * t:(j + 1) * t],
                       preferred_element_type=F32)

    depth = min(BOUNDED_DEPTH, len(items))
    pending = [scores(it) for it in items[:depth]]
    for n, (j, i) in enumerate(items):
        if i == 0:
            acc = jnp.zeros((V_MLA, t), F32)
            den = jnp.zeros((SUBLANES, t), F32)
        s = pending.pop(0)
        if n + depth < len(items):
            pending.append(scores(items[n + depth]))
        c = i % npt
        p = jnp.exp2(s)
        den = den + jnp.sum(p.reshape(ck // SUBLANES, SUBLANES, t), axis=0)
        acc = acc + jnp.dot(vt_ref[0, 0, i // npt, :, c * ck:(c + 1) * ck], p.astype(BF16),
                            preferred_element_type=F32)
        if i == n_chunks - 1:
            o_ref[0, j * t:(j + 1) * t, :] = (acc / jnp.sum(den, axis=0, keepdims=True)).T.astype(BF16)


def _mla_attn(qt, k, vt, *, bounded):
    b, n_heads, _, s = qt.shape
    t = _pick_tile(s, MLA_TQ * (MLA_BOUNDED_Q_TILES if bounded else 1))
    nkt, tk = vt.shape[2], vt.shape[4]
    return pl.pallas_call(
        _mla_attn_bounded_kernel if bounded else _mla_attn_kernel,
        grid=(b, n_heads, s // t),
        in_specs=[pl.BlockSpec((1, 1, QK_HEAD, t), lambda bi, hh, qi: (bi, hh, 0, qi)),
                  pl.BlockSpec((1, 1, s, QK_HEAD), lambda bi, hh, qi: (bi, hh, 0, 0)),
                  pl.BlockSpec((1, 1, nkt, V_MLA, tk), lambda bi, hh, qi: (bi, hh, 0, 0, 0))],
        out_specs=pl.BlockSpec((1, t, V_MLA), lambda bi, hh, qi: (bi, qi, hh)),
        out_shape=jax.ShapeDtypeStruct((b, s, n_heads * V_MLA), BF16),
        compiler_params=_cparams(("parallel", "parallel", "parallel")),
        name="mla_attn",
    )(qt, k, vt)


def _out_proj_kernel(x_ref, gt_ref, a_ref, b_ref, wa_ref, wb_ref, sc_ref, sh_ref, g2_ref, o_ref, h_ref):
    tm = x_ref.shape[1]
    rows = tm // OUT_PROJ_SPLIT
    mixes = []
    for i in range(OUT_PROJ_SPLIT):
        sl = slice(i * rows, (i + 1) * rows)
        mixes.append(jnp.dot(a_ref[0, sl, :], wa_ref[...], preferred_element_type=F32)
                     + jnp.dot(b_ref[0, sl, :], wb_ref[...], preferred_element_type=F32))
    for i in range(OUT_PROJ_SPLIT):
        sl = slice(i * rows, (i + 1) * rows)
        x1 = x_ref[0, sl, :] + gt_ref[0] * mixes[i]
        o_ref[0, sl, :] = x1
        h_ref[0, sl, :] = _modulated_norm(x1, g2_ref[...], sc_ref[0], sh_ref[0]).astype(BF16)


def _out_proj(x, gt1, a, bb, wa, wb, sc2, sh2, g2, *, tm):
    b, s, d = x.shape
    row = lambda w: pl.BlockSpec((1, tm, w), lambda bi, i: (bi, i, 0))
    mod = pl.BlockSpec((1, 1, d), lambda bi, i: (bi, 0, 0))
    return pl.pallas_call(
        _out_proj_kernel,
        grid=(b, s // tm),
        in_specs=[row(d), mod, row(a.shape[-1]), row(bb.shape[-1]), _const_spec(wa.shape),
                  _const_spec(wb.shape), mod, mod, _const_spec(g2.shape)],
        out_specs=[row(d), row(d)],
        out_shape=[jax.ShapeDtypeStruct((b, s, d), F32), jax.ShapeDtypeStruct((b, s, d), BF16)],
        compiler_params=_cparams(("parallel", "parallel")),
        name="out_proj",
    )(x, gt1, a, bb, wa, wb, sc2, sh2, g2)


def _ffn_kernel(x_ref, h_ref, gt_ref, wg_ref, wu_ref, wd_ref, o_ref, xs_ref, *, n_x_chunks):
    j = pl.program_id(2)
    rows = x_ref.shape[1]

    @pl.when(j == 0)
    def _():
        o_ref[...] = jnp.zeros(o_ref.shape, F32)

    chunk = jnp.minimum(j, n_x_chunks - 1)
    xs_ref[pl.ds(pl.multiple_of(chunk * rows, rows), rows), :] = x_ref[0]

    h = h_ref[0]
    tf = wg_ref.shape[1]
    cf = tf // FFN_COL_SPLIT
    gates = []
    for c in range(FFN_COL_SPLIT):
        cs = slice(c * cf, (c + 1) * cf)
        gates.append((jnp.dot(h, wg_ref[:, cs], preferred_element_type=F32),
                      jnp.dot(h, wu_ref[:, cs], preferred_element_type=F32)))
    part = None
    for c, (g, u) in enumerate(gates):
        act = (g * jax.nn.sigmoid(g) * u).astype(BF16)
        d = jnp.dot(act, wd_ref[c * cf:(c + 1) * cf, :], preferred_element_type=F32)
        part = d if part is None else part + d
    o_ref[0] += part

    @pl.when(j == pl.num_programs(2) - 1)
    def _():
        o_ref[0] = xs_ref[...] + gt_ref[0] * o_ref[0]


def _ffn(x, h2, gt2, wg, wu, wd, *, tm, tf):
    b, s, d = x.shape
    dff = wg.shape[1]
    nj = dff // tf
    nxc = 1
    while nxc * 2 <= nj and (tm // (nxc * 2)) % SUBLANES == 0:
        nxc *= 2
    row = pl.BlockSpec((1, tm, d), lambda bi, i, j: (bi, i, 0))
    x_chunk = pl.BlockSpec((1, tm // nxc, d), lambda bi, i, j: (bi, i * nxc + jnp.minimum(j, nxc - 1), 0))
    return pl.pallas_call(
        functools.partial(_ffn_kernel, n_x_chunks=nxc),
        grid=(b, s // tm, nj),
        in_specs=[x_chunk, row, pl.BlockSpec((1, 1, d), lambda bi, i, j: (bi, 0, 0)),
                  pl.BlockSpec((d, tf), lambda bi, i, j: (0, j)),
                  pl.BlockSpec((d, tf), lambda bi, i, j: (0, j)),
                  pl.BlockSpec((tf, d), lambda bi, i, j: (j, 0))],
        out_specs=row,
        out_shape=jax.ShapeDtypeStruct((b, s, d), F32),
        scratch_shapes=[pltpu.VMEM((tm, d), F32)],
        compiler_params=_cparams(("parallel", "parallel", "arbitrary")),
        name="ffn",
    )(x, h2, gt2, wg, wu, wd)


def _rope_tables(s):
    pos = jnp.arange(s, dtype=jnp.float32)
    inv = 1.0 / (ROPE_THETA ** (jnp.arange(0, QK_ROPE, 2, dtype=jnp.float32) / QK_ROPE))
    ang = pos[:, None] * inv[None, :]
    return jnp.cos(ang), jnp.sin(ang)


def _pick_tile(n, pref):
    t = min(pref, n)
    while n % t:
        t //= 2
    return t


def kernel(x, c, rel_bias, w_ada, b_ada, g_norm1, w_in, g_q_diff, g_k_diff, lambda_vecs, g_subln, g_q_a, w_q_b, g_kv_a, w_kv_b, g_q_mla, g_k_mla, w_out, g_norm2, w_gate, w_up, w_down):
    b, s, d = x.shape
    depth = w_ada.shape[0]
    diff_width = d // 2
    n_hd = diff_width // DV_DIFF
    n_hm = (d - diff_width) // V_MLA
    assert n_hd == n_hm and n_hd % 2 == 0
    n_heads = n_hd
    wd = n_heads * LANES
    half = QK_ROPE // 2
    tk = _pick_tile(s, ATT_TK)
    nkt = s // tk
    t = tk
    assert t + 1 >= REL_MAX_DIST
    tm = _pick_tile(s, 512)

    cos, sin = _rope_tables(s)
    cosq, sinq = jnp.tile(cos, (1, n_heads)), jnp.tile(sin, (1, n_heads))
    cosk = jnp.tile(cos, (1, 4))
    sink = jnp.tile(jnp.concatenate([-sin, sin], axis=1), (1, 2))

    c_pad = jnp.pad(c, ((0, (-b) % 8), (0, 0)))

    for l in range(depth):
        lambda_init = 0.8 - 0.6 * math.exp(-0.3 * l)
        mod = _ada(c_pad, w_ada[l], b_ada[l][None, :])[:b]
        sh1, sc1, gt1, sh2, sc2, gt2 = [m[:, None, :] for m in jnp.split(mod, 6, axis=-1)]

        wi = w_in[l]
        o = 3 * wd + Q_LORA + KV_LORA
        w_kpe = wi[:, o:o + QK_ROPE]
        w_kpe_sw = jnp.concatenate([w_kpe[:, half:], w_kpe[:, :half]], axis=1)
        wqkv = wi.astype(BF16)
        wlat = jnp.concatenate([w_kpe, w_kpe, w_kpe_sw, w_kpe_sw], axis=1).astype(BF16)
        wq = w_q_b[l].reshape(Q_LORA, n_heads, QK_HEAD)
        wqb = jnp.concatenate([wq[:, :, :QK_NOPE].reshape(Q_LORA, -1),
                               wq[:, :, QK_NOPE:QK_NOPE + half].reshape(Q_LORA, -1),
                               wq[:, :, QK_NOPE + half:].reshape(Q_LORA, -1)], axis=1).astype(BF16)
        wkv = w_kv_b[l].reshape(KV_LORA, n_heads, QK_NOPE + V_MLA)
        wkvb = jnp.concatenate([wkv[:, :, :QK_NOPE].reshape(KV_LORA, -1),
                                wkv[:, :, QK_NOPE:].reshape(KV_LORA, -1)], axis=1).astype(BF16)
        gq, gk = g_q_mla[l], g_k_mla[l]
        gk_pe = gk[QK_NOPE:]
        gk_pe_sw = jnp.concatenate([gk_pe[half:], gk_pe[:half]])

        qt_d, kd, vt_d, qt_m, k_m, vt_m = _in_proj(
            x, sc1, sh1, g_norm1[l][None, :], wqkv, wlat,
            jnp.tile(g_q_diff[l], 2)[None, :], jnp.tile(g_k_diff[l], 2)[None, :],
            g_q_a[l][None, :], wqb, g_kv_a[l][None, :], wkvb,
            gq[None, :QK_NOPE], jnp.tile(gq[QK_NOPE:QK_NOPE + half], n_heads)[None, :],
            jnp.tile(gq[QK_NOPE + half:], n_heads)[None, :],
            gk[None, :QK_NOPE], jnp.tile(gk_pe, 2)[None, :], jnp.tile(gk_pe_sw, 2)[None, :],
            cosq, sinq, cosk, sink, n_heads=n_heads, tm=tk)

        bias = _bias_tiles(rel_bias, t)
        gmax = lambda g: jnp.max(jnp.abs(g))
        bound_d = (NORM_SLACK * LOG2E * DH_DIFF ** 0.5 * gmax(g_q_diff[l]) * gmax(g_k_diff[l])
                   + LOG2E * gmax(rel_bias))
        bound_m = NORM_SLACK * LOG2E * QK_HEAD ** 0.5 * gmax(g_q_mla[l]) * gmax(g_k_mla[l])
        diff_args = (rel_bias, lambda_vecs[l], g_subln[l][:, None], qt_d, kd, vt_d, bias)
        a_out = lax.cond(
            bound_d <= SCORE_BOUND,
            lambda *a: _diff_attn(*a, lambda_init=lambda_init, bounded=True),
            lambda *a: _diff_attn(*a, lambda_init=lambda_init, bounded=False), *diff_args)
        b_out = lax.cond(
            bound_m <= SCORE_BOUND,
            lambda *a: _mla_attn(*a, bounded=True),
            lambda *a: _mla_attn(*a, bounded=False), qt_m, k_m, vt_m)

        wo = w_out[l].astype(BF16)
        x, h2 = _out_proj(x, gt1, a_out, b_out, wo[:diff_width], wo[diff_width:], sc2, sh2,
                          g_norm2[l][None, :], tm=tm)
        x = _ffn(x, h2, gt2, w_gate[l].astype(BF16), w_up[l].astype(BF16), w_down[l].astype(BF16),
                 tm=_pick_tile(s, FFN_TM), tf=_pick_tile(w_gate.shape[-1], FFN_TF))
    return x
```

```python
import functools
import math

import jax
import jax.numpy as jnp
from jax import lax
from jax.experimental import pallas as pl
from jax.experimental.pallas import tpu as pltpu

F32 = jnp.float32
BF16 = jnp.bfloat16

DH_DIFF = 64
DV_DIFF = 2 * DH_DIFF
QK_NOPE = 128
QK_ROPE = 64
QK_HEAD = QK_NOPE + QK_ROPE
V_MLA = 128
Q_LORA = 512
KV_LORA = 256
ROPE_THETA = 10000.0
REL_BUCKETS = 32
REL_MAX_DIST = 128
EPS = 1e-6
LOG2E = 1.4426950408889634

LANES = 128
SUBLANES = 8
VMEM_LIMIT_BYTES = 56 * 1024 * 1024

OUT_PROJ_SPLIT = 4
IN_PROJ_SPLIT = 2
FFN_TM = 1024
FFN_TF = 512
FFN_COL_SPLIT = 2
ATT_TK = 512
MLA_TQ = 512
KEY_CHUNK = 256
BOUNDED_DEPTH = 2
DIFF_BOUNDED_Q_TILES = 8
MLA_BOUNDED_Q_TILES = 8
SCORE_BOUND = 40.0
NORM_SLACK = 1.02


def _cparams(sem):
    return pltpu.CompilerParams(dimension_semantics=sem, vmem_limit_bytes=VMEM_LIMIT_BYTES)


def _const_spec(shape):
    nd = len(shape)
    return pl.BlockSpec(shape, lambda *_: (0,) * nd, pipeline_mode=pl.Buffered(1))


def _ada_kernel(c_ref, w_ref, b_ref, o_ref):
    c = c_ref[...]
    ca = c * jax.nn.sigmoid(c)
    o_ref[...] = jnp.dot(ca.astype(BF16), w_ref[...].astype(BF16), preferred_element_type=F32) + b_ref[...]


def _ada(c_pad, w, b, tn=1024):
    m, d = c_pad.shape
    n = w.shape[1]
    return pl.pallas_call(
        _ada_kernel,
        grid=(n // tn,),
        in_specs=[pl.BlockSpec((m, d), lambda j: (0, 0)),
                  pl.BlockSpec((d, tn), lambda j: (0, j)),
                  pl.BlockSpec((1, tn), lambda j: (0, j))],
        out_specs=pl.BlockSpec((m, tn), lambda j: (0, j)),
        out_shape=jax.ShapeDtypeStruct((m, n), F32),
        compiler_params=_cparams(("arbitrary",)),
        name="ada",
    )(c_pad, w, b)


def _modulated_norm(x, g, sc, sh):
    ms = jnp.mean(x * x, axis=-1, keepdims=True)
    return (x * lax.rsqrt(ms + EPS) * g) * (1.0 + sc) + sh


def _half_lane_norm(blk, g2, lo_mask, out_scale):
    sq = blk * blk
    s_lo = jnp.sum(jnp.where(lo_mask, sq, 0.0), axis=-1, keepdims=True)
    s_hi = jnp.sum(jnp.where(lo_mask, 0.0, sq), axis=-1, keepdims=True)
    inv = jnp.where(lo_mask, lax.rsqrt(s_lo * (1.0 / DH_DIFF) + EPS),
                    lax.rsqrt(s_hi * (1.0 / DH_DIFF) + EPS))
    return blk * inv * (g2 * out_scale)


def _in_proj_kernel(x_ref, sc_ref, sh_ref, g1_ref, wqkv_ref, wlat_ref, gqd_ref, gkd_ref,
                    gqa_ref, wqb_ref, gkva_ref, wkvb_ref, gqn_ref, gq1_ref, gq2_ref,
                    gkn_ref, gkp_ref, gkps_ref, cosq_ref, sinq_ref, cosk_ref, sink_ref,
                    qtd_ref, kd_ref, vtd_ref, qtm_ref, km_ref, vtm_ref, *, n_heads):
    tm = x_ref.shape[1]
    rows = tm // IN_PROJ_SPLIT
    wd = n_heads * LANES
    half = QK_ROPE // 2
    wx = n_heads * half
    lo_mask = lax.broadcasted_iota(jnp.int32, (rows, LANES), 1) < DH_DIFF
    lane_x = lax.broadcasted_iota(jnp.int32, (rows, wx), 1) // half
    qscale = DH_DIFF ** -0.5 * LOG2E
    mscale = QK_HEAD ** -0.5 * LOG2E

    def rms(v, g):
        return v * lax.rsqrt(jnp.mean(v * v, axis=-1, keepdims=True) + EPS) * g

    for sb in range(IN_PROJ_SPLIT):
        rs = slice(sb * rows, (sb + 1) * rows)
        h = _modulated_norm(x_ref[0, rs, :], g1_ref[...], sc_ref[0], sh_ref[0]).astype(BF16)
        lat = jnp.dot(h, wqkv_ref[:, 3 * wd:3 * wd + Q_LORA + KV_LORA], preferred_element_type=F32)
        cq = lat[:, 0:Q_LORA]
        ckv = lat[:, Q_LORA:Q_LORA + KV_LORA]
        kpe4 = jnp.dot(h, wlat_ref[...], preferred_element_type=F32)
        kpe2 = kpe4[:, 0:LANES]
        kpes2 = kpe4[:, LANES:2 * LANES]

        vd = jnp.dot(h, wqkv_ref[:, 2 * wd:3 * wd], preferred_element_type=F32)
        for hh in range(n_heads):
            vtd_ref[0, hh, 0, :, rs] = vd[:, hh * LANES:(hh + 1) * LANES].T.astype(BF16)

        kv = jnp.dot(rms(ckv, gkva_ref[...]).astype(BF16), wkvb_ref[...], preferred_element_type=F32)
        for hh in range(n_heads):
            vtm_ref[0, hh, 0, :, rs] = kv[:, wd + hh * LANES:wd + (hh + 1) * LANES].T.astype(BF16)
        ss_pe = jnp.sum(jnp.where(lo_mask, kpe2 * kpe2, 0.0), axis=-1, keepdims=True)
        kr2 = kpe2 * gkp_ref[...] * cosk_ref[rs, :] + kpes2 * gkps_ref[...] * sink_ref[rs, :]
        for hh in range(n_heads):
            sl = slice(hh * LANES, (hh + 1) * LANES)
            kn = kv[:, sl]
            ss = jnp.sum(kn * kn, axis=-1, keepdims=True) + ss_pe
            r = lax.rsqrt(ss * (1.0 / QK_HEAD) + EPS)
            km_ref[0, hh, rs, 0:QK_NOPE] = (kn * r * gkn_ref[...]).astype(BF16)
            km_ref[0, hh, rs, QK_NOPE:QK_HEAD] = (kr2 * r)[:, 0:QK_ROPE].astype(BF16)

        qd = jnp.dot(h, wqkv_ref[:, 0:wd], preferred_element_type=F32)
        for hh in range(n_heads):
            sl = slice(hh * LANES, (hh + 1) * LANES)
            qtd_ref[0, hh, :, rs] = _half_lane_norm(qd[:, sl], gqd_ref[...], lo_mask, qscale).T.astype(BF16)

        qm = jnp.dot(rms(cq, gqa_ref[...]).astype(BF16), wqb_ref[...], preferred_element_type=F32)
        x1 = qm[:, wd:wd + wx]
        x2 = qm[:, wd + wx:wd + 2 * wx]
        sq_x = x1 * x1 + x2 * x2
        inv_x = jnp.zeros((rows, wx), F32)
        for hh in range(n_heads):
            sl = slice(hh * LANES, (hh + 1) * LANES)
            qn = qm[:, sl]
            ss = (jnp.sum(qn * qn, axis=-1, keepdims=True)
                  + jnp.sum(jnp.where(lane_x == hh, sq_x, 0.0), axis=-1, keepdims=True))
            r = lax.rsqrt(ss * (1.0 / QK_HEAD) + EPS) * mscale
            qtm_ref[0, hh, 0:QK_NOPE, rs] = (qn * r * gqn_ref[...]).T.astype(BF16)
            inv_x = jnp.where(lane_x == hh, r, inv_x)
        a1 = x1 * gq1_ref[...]
        a2 = x2 * gq2_ref[...]
        cq_t = cosq_ref[rs, :]
        sq_t = sinq_ref[rs, :]
        x1t = ((a1 * cq_t - a2 * sq_t) * inv_x).T.astype(BF16)
        x2t = ((a2 * cq_t + a1 * sq_t) * inv_x).T.astype(BF16)
        for hh in range(n_heads):
            qtm_ref[0, hh, QK_NOPE:QK_NOPE + half, rs] = x1t[hh * half:(hh + 1) * half]
            qtm_ref[0, hh, QK_NOPE + half:QK_HEAD, rs] = x2t[hh * half:(hh + 1) * half]

        kd = jnp.dot(h, wqkv_ref[:, wd:2 * wd], preferred_element_type=F32)
        for hh in range(n_heads):
            sl = slice(hh * LANES, (hh + 1) * LANES)
            kd_ref[0, rs, sl] = _half_lane_norm(kd[:, sl], gkd_ref[...], lo_mask, 1.0).astype(BF16)


def _in_proj(x, sc1, sh1, g1, wqkv, wlat, gqd, gkd, gqa, wqb, gkva, wkvb, gqn, gq1, gq2,
             gkn, gkp, gkps, cosq, sinq, cosk, sink, *, n_heads, tm):
    b, s, d = x.shape
    wd = n_heads * LANES
    wx = n_heads * (QK_ROPE // 2)
    row = lambda w: pl.BlockSpec((1, tm, w), lambda bi, i: (bi, i, 0))
    mod = pl.BlockSpec((1, 1, d), lambda bi, i: (bi, 0, 0))
    tab = lambda w: pl.BlockSpec((tm, w), lambda bi, i: (i, 0))
    consts = [g1, wqkv, wlat, gqd, gkd, gqa, wqb, gkva, wkvb, gqn, gq1, gq2, gkn, gkp, gkps]
    return pl.pallas_call(
        functools.partial(_in_proj_kernel, n_heads=n_heads),
        grid=(b, s // tm),
        in_specs=[row(d), mod, mod] + [_const_spec(a.shape) for a in consts]
                 + [tab(wx), tab(wx), tab(LANES), tab(LANES)],
        out_specs=[pl.BlockSpec((1, n_heads, LANES, tm), lambda bi, i: (bi, 0, 0, i)),
                   row(wd),
                   pl.BlockSpec((1, n_heads, 1, DV_DIFF, tm), lambda bi, i: (bi, 0, i, 0, 0)),
                   pl.BlockSpec((1, n_heads, QK_HEAD, tm), lambda bi, i: (bi, 0, 0, i)),
                   pl.BlockSpec((1, n_heads, tm, QK_HEAD), lambda bi, i: (bi, 0, i, 0)),
                   pl.BlockSpec((1, n_heads, 1, V_MLA, tm), lambda bi, i: (bi, 0, i, 0, 0))],
        out_shape=[jax.ShapeDtypeStruct((b, n_heads, LANES, s), BF16),
                   jax.ShapeDtypeStruct((b, s, wd), BF16),
                   jax.ShapeDtypeStruct((b, n_heads, s // tm, DV_DIFF, tm), BF16),
                   jax.ShapeDtypeStruct((b, n_heads, QK_HEAD, s), BF16),
                   jax.ShapeDtypeStruct((b, n_heads, s, QK_HEAD), BF16),
                   jax.ShapeDtypeStruct((b, n_heads, s // tm, V_MLA, tm), BF16)],
        compiler_params=_cparams(("parallel", "parallel")),
        name="in_proj",
    )(x, sc1, sh1, *consts, cosq, sinq, cosk, sink)


def _t5_bucket(rel):
    nb = REL_BUCKETS // 2
    max_exact = nb // 2
    base = jnp.where(rel > 0, nb, 0)
    n = jnp.abs(rel)
    nf = jnp.maximum(n, 1).astype(jnp.float32)
    large = max_exact + (jnp.log(nf / max_exact) / math.log(REL_MAX_DIST / max_exact)
                         * (nb - max_exact)).astype(jnp.int32)
    large = jnp.minimum(large, nb - 1)
    return base + jnp.where(n < max_exact, n, large)


def _bias_kernel(rb_ref, bucket_ref, o_ref):
    hh = pl.program_id(0)
    t = o_ref.shape[-1]
    far_left = rb_ref[REL_BUCKETS // 2 - 1, hh]
    far_right = rb_ref[REL_BUCKETS - 1, hh]
    o_ref[0, 0] = jnp.full((t, t), far_left, F32) * LOG2E
    o_ref[0, 4] = jnp.full((t, t), far_right, F32) * LOG2E
    bk_strip = bucket_ref[...]
    strip = jnp.zeros(bk_strip.shape, F32)
    for bk in range(REL_BUCKETS):
        strip = jnp.where(bk_strip == bk, rb_ref[bk, hh], strip)
    strip = strip * LOG2E
    for d in (-1, 0, 1):
        diag = jnp.concatenate([strip[:, (2 - d) * t:(3 - d) * t], strip[:, (1 - d) * t:(2 - d) * t]], axis=1)
        rows = jnp.broadcast_to(diag, (t, 2 * t))
        o_ref[0, d + 2] = pltpu.roll(rows, 0, 1, stride=1, stride_axis=0)[:, :t]


def _bias_tiles(rel_bias, t):
    n_heads = rel_bias.shape[1]
    buckets = _t5_bucket(2 * t - jnp.arange(4 * t, dtype=jnp.int32))[None, :]
    return pl.pallas_call(
        _bias_kernel,
        grid=(n_heads,),
        in_specs=[pl.BlockSpec(memory_space=pltpu.SMEM),
                  pl.BlockSpec((1, 4 * t), lambda hh: (0, 0))],
        out_specs=pl.BlockSpec((1, 5, t, t), lambda hh: (hh, 0, 0, 0)),
        out_shape=jax.ShapeDtypeStruct((n_heads, 5, t, t), F32),
        compiler_params=_cparams(("arbitrary",)),
        name="rel_bias_tiles",
    )(rel_bias, buckets)


ONES_ROWS = 16


def _softmax_state_step(s, offset, vt, state):
    m_new, alpha, p = _softmax_weights(s, offset, state[0])
    return m_new, _accumulate(alpha, state[1], vt, p)


def _softmax_weights(s, offset, m_old):
    m_new = jnp.maximum(m_old, jnp.max(s, axis=0, keepdims=True) + offset)
    alpha = jnp.exp2(m_old - m_new)
    p = jnp.exp2(s - (m_new - offset)).astype(BF16)
    return m_new, alpha, p


def _accumulate(alpha, acc_old, vt, p):
    return alpha * acc_old + jnp.dot(vt, p, preferred_element_type=F32)


def _init_state(dv, t):
    return (jnp.full((1, t), -jnp.inf, F32), jnp.zeros((dv + ONES_ROWS, t), F32))


def _with_ones(vt):
    return jnp.concatenate([vt, jnp.ones((ONES_ROWS, vt.shape[1]), vt.dtype)], axis=0)


def _normalised(state, dv):
    _, acc = state
    return acc[:dv] / acc[dv:dv + 1]


def _diff_attn_kernel(rb_ref, lamv_ref, gsub_ref, qt_ref, k_ref, vt_ref, bias_ref, o_ref, *,
                      lambda_init, bounded):
    hh = pl.program_id(0)
    t = bias_ref.shape[-1]
    nq = qt_ref.shape[-1] // t
    nkt = vt_ref.shape[2]

    off_left = rb_ref[REL_BUCKETS // 2 - 1, hh] * LOG2E
    off_right = rb_ref[REL_BUCKETS - 1, hh] * LOG2E
    lv = lamv_ref[...]
    lam = (jnp.exp(jnp.sum(lv[0:1] * lv[1:2], axis=-1, keepdims=True))
           - jnp.exp(jnp.sum(lv[2:3] * lv[3:4], axis=-1, keepdims=True)) + lambda_init)

    n_near = min(nkt, 3)
    row = lax.broadcasted_iota(jnp.int32, (LANES, t), 0)
    zero = jnp.zeros((LANES, t), BF16)

    def query_maps(j):
        qt = qt_ref[0, 0, :, j * t:(j + 1) * t]
        return jnp.where(row < DH_DIFF, qt, zero), jnp.where(row < DH_DIFF, zero, qt)

    def tile_index(j, r):
        return lax.rem(pl.program_id(2) * nq + j - 1 + r + nkt, nkt)

    def far_offset(j, r):
        return jnp.where(tile_index(j, r) < pl.program_id(2) * nq + j, off_left, off_right)

    def biased_scores(j, qmap, r, k0, ck):
        kj = tile_index(j, r)
        k_t = k_ref[0, pl.ds(pl.multiple_of(kj * t + k0, ck), ck), :]
        s = jnp.dot(k_t, qmap, preferred_element_type=F32)
        saturated = (r >= n_near or (nkt >= 3 and r == 0 and k0 + ck - 1 - t <= -REL_MAX_DIST)
                     or (nkt >= 3 and r == 2 and k0 + 1 >= REL_MAX_DIST))
        if not (bounded and saturated) and r < n_near:
            s = s + bias_ref[0, jnp.clip(kj - (pl.program_id(2) * nq + j), -2, 2) + 2, k0:k0 + ck, :]
        elif bounded:
            s = s + far_offset(j, r)
        return s

    def finish(j, outs):
        o = outs[0] - lam * outs[1]
        ms = jnp.mean(o * o, axis=0, keepdims=True)
        y = o * lax.rsqrt(ms + EPS) * (gsub_ref[...] * (1.0 - lambda_init))
        o_ref[0, j * t:(j + 1) * t, :] = y.T.astype(BF16)

    if bounded:
        ck = min(t, KEY_CHUNK)
        npt = t // ck
        qmaps = [query_maps(j) for j in range(nq)]
        items = [(j, r, c, mp) for j in range(nq) for r in range(nkt) for c in range(npt) for mp in range(2)]
        score = lambda it: biased_scores(it[0], qmaps[it[0]][it[3]], it[1], it[2] * ck, ck)
        depth = min(BOUNDED_DEPTH, len(items))
        pending = [score(it) for it in items[:depth]]
        for n, (j, r, c, mp) in enumerate(items):
            if (r, c, mp) == (0, 0, 0):
                accs = [jnp.zeros((DV_DIFF, t), F32) for _ in range(2)]
                dens = [jnp.zeros((SUBLANES, t), F32) for _ in range(2)]
            s = pending.pop(0)
            if n + depth < len(items):
                pending.append(score(items[n + depth]))
            p = jnp.exp2(s)
            dens[mp] = dens[mp] + jnp.sum(p.reshape(ck // SUBLANES, SUBLANES, t), axis=0)
            accs[mp] = accs[mp] + jnp.dot(vt_ref[0, 0, tile_index(j, r), :, c * ck:(c + 1) * ck],
                                          p.astype(BF16), preferred_element_type=F32)
            if (r, c, mp) == (nkt - 1, npt - 1, 1):
                finish(j, [acc / jnp.sum(den, axis=0, keepdims=True) for acc, den in zip(accs, dens)])
    else:
        for j in range(nq):
            qmap = query_maps(j)
            states = [_init_state(DV_DIFF, t), _init_state(DV_DIFF, t)]
            score = lambda r: [biased_scores(j, qmap[mp], r, 0, t) for mp in range(2)]
            s_next = score(0)
            for r in range(nkt):
                s_cur = s_next
                if r + 1 < nkt:
                    s_next = score(r + 1)
                vt = _with_ones(vt_ref[0, 0, tile_index(j, r)])
                offset = 0.0 if r < n_near else far_offset(j, r)
                for mp in range(2):
                    states[mp] = _softmax_state_step(s_cur[mp], offset, vt, states[mp])
            finish(j, [_normalised(st, DV_DIFF) for st in states])


def _diff_attn(rel_bias, lamv, gsub_col, qt, k, vt, bias, *, lambda_init, bounded):
    b, n_heads, _, s = qt.shape
    t = bias.shape[-1]
    nkt, tk = vt.shape[2], vt.shape[4]
    assert tk == t
    tq = _pick_tile(s, t * (DIFF_BOUNDED_Q_TILES if bounded else 1))
    return pl.pallas_call(
        functools.partial(_diff_attn_kernel, lambda_init=lambda_init, bounded=bounded),
        grid=(n_heads, b, s // tq),
        in_specs=[pl.BlockSpec(memory_space=pltpu.SMEM),
                  pl.BlockSpec(lamv.shape, lambda hh, bi, qi: (0, 0)),
                  pl.BlockSpec(gsub_col.shape, lambda hh, bi, qi: (0, 0)),
                  pl.BlockSpec((1, 1, LANES, tq), lambda hh, bi, qi: (bi, hh, 0, qi)),
                  pl.BlockSpec((1, s, LANES), lambda hh, bi, qi: (bi, 0, hh)),
                  pl.BlockSpec((1, 1, nkt, DV_DIFF, tk), lambda hh, bi, qi: (bi, hh, 0, 0, 0)),
                  pl.BlockSpec((1, 5, t, t), lambda hh, bi, qi: (hh, 0, 0, 0))],
        out_specs=pl.BlockSpec((1, tq, DV_DIFF), lambda hh, bi, qi: (bi, qi, hh)),
        out_shape=jax.ShapeDtypeStruct((b, s, n_heads * DV_DIFF), BF16),
        compiler_params=_cparams(("parallel", "parallel", "parallel")),
        name="diff_attn",
    )(rel_bias, lamv, gsub_col, qt, k, vt, bias)


def _mla_attn_kernel(qt_ref, k_ref, vt_ref, o_ref):
    nkt = vt_ref.shape[2]
    tk = vt_ref.shape[-1]
    qt = qt_ref[0, 0]
    m, acc = _init_state(V_MLA, qt.shape[-1])
    scores = lambda kj: jnp.dot(k_ref[0, 0, kj * tk:(kj + 1) * tk, :], qt, preferred_element_type=F32)
    depth = 2
    pending = [scores(kj) for kj in range(min(depth, nkt))]
    for kj in range(nkt):
        s = pending.pop(0)
        if kj + depth < nkt:
            pending.append(scores(kj + depth))
        m, alpha, p = _softmax_weights(s, 0.0, m)
        acc = _accumulate(alpha, acc, _with_ones(vt_ref[0, 0, kj]), p)
    o_ref[0] = _normalised((m, acc), V_MLA).T.astype(BF16)


def _mla_attn_bounded_kernel(qt_ref, k_ref, vt_ref, *rest):
    n_side = (len(rest) - 1) // 2
    o_ref = rest[n_side]
    for src_ref, dst_ref in zip(rest[:n_side], rest[n_side + 1:]):
        dst_ref[...] = src_ref[...].astype(BF16)
    nkt = vt_ref.shape[2]
    tk = vt_ref.shape[-1]
    t = min(qt_ref.shape[-1], MLA_TQ)
    nq = qt_ref.shape[-1] // t
    ck = min(tk, KEY_CHUNK)
    npt = tk // ck
    n_chunks = nkt * npt
    items = [(j, i) for j in range(nq) for i in range(n_chunks)]

    def scores(item):
        j, i = item
        return jnp.dot(k_ref[0, 0, i * ck:(i + 1) * ck, :], qt_ref[0, 0, :, j * t:(j + 1) * t],
                       preferred_element_type=F32)

    depth = min(BOUNDED_DEPTH, len(items))
    pending = [scores(it) for it in items[:depth]]
    for n, (j, i) in enumerate(items):
        if i == 0:
            acc = jnp.zeros((V_MLA, t), F32)
            den = jnp.zeros((SUBLANES, t), F32)
        s = pending.pop(0)
        if n + depth < len(items):
            pending.append(scores(items[n + depth]))
        c = i % npt
        p = jnp.exp2(s)
        den = den + jnp.sum(p.reshape(ck // SUBLANES, SUBLANES, t), axis=0)
        acc = acc + jnp.dot(vt_ref[0, 0, i // npt, :, c * ck:(c + 1) * ck], p.astype(BF16),
                            preferred_element_type=F32)
        if i == n_chunks - 1:
            o_ref[0, j * t:(j + 1) * t, :] = (acc / jnp.sum(den, axis=0, keepdims=True)).T.astype(BF16)


def _mla_attn(qt, k, vt, *side, bounded):
    b, n_heads, _, s = qt.shape
    t = _pick_tile(s, MLA_TQ * (MLA_BOUNDED_Q_TILES if bounded else 1))
    nkt, tk = vt.shape[2], vt.shape[4]
    q_steps = s // t
    steps = b * n_heads * q_steps
    bf16_rows = 2 * SUBLANES
    fused = bounded and all(w.shape[0] % (steps * bf16_rows) == 0 for w in side)
    slab = lambda w: pl.BlockSpec((w.shape[0] // steps, w.shape[1]),
                                  lambda bi, hh, qi: ((bi * n_heads + hh) * q_steps + qi, 0))
    attn_spec = pl.BlockSpec((1, t, V_MLA), lambda bi, hh, qi: (bi, qi, hh))
    attn_shape = jax.ShapeDtypeStruct((b, s, n_heads * V_MLA), BF16)
    out = pl.pallas_call(
        _mla_attn_bounded_kernel if bounded else _mla_attn_kernel,
        grid=(b, n_heads, q_steps),
        in_specs=[pl.BlockSpec((1, 1, QK_HEAD, t), lambda bi, hh, qi: (bi, hh, 0, qi)),
                  pl.BlockSpec((1, 1, s, QK_HEAD), lambda bi, hh, qi: (bi, hh, 0, 0)),
                  pl.BlockSpec((1, 1, nkt, V_MLA, tk), lambda bi, hh, qi: (bi, hh, 0, 0, 0))]
                 + ([slab(w) for w in side] if fused else []),
        out_specs=[attn_spec] + [slab(w) for w in side] if fused else attn_spec,
        out_shape=([attn_shape] + [jax.ShapeDtypeStruct(w.shape, BF16) for w in side]) if fused else attn_shape,
        compiler_params=_cparams(("parallel", "parallel", "parallel")),
        name="mla_attn",
    )(qt, k, vt, *(side if fused else ()))
    if fused:
        return tuple(out)
    return (out,) + tuple(w.astype(BF16) for w in side)


def _out_proj_kernel(x_ref, gt_ref, a_ref, b_ref, wa_ref, wb_ref, sc_ref, sh_ref, g2_ref, o_ref, h_ref):
    tm = x_ref.shape[1]
    rows = tm // OUT_PROJ_SPLIT
    mixes = []
    for i in range(OUT_PROJ_SPLIT):
        sl = slice(i * rows, (i + 1) * rows)
        mixes.append(jnp.dot(a_ref[0, sl, :], wa_ref[...], preferred_element_type=F32)
                     + jnp.dot(b_ref[0, sl, :], wb_ref[...], preferred_element_type=F32))
    for i in range(OUT_PROJ_SPLIT):
        sl = slice(i * rows, (i + 1) * rows)
        x1 = x_ref[0, sl, :] + gt_ref[0] * mixes[i]
        o_ref[0, sl, :] = x1
        h_ref[0, sl, :] = _modulated_norm(x1, g2_ref[...], sc_ref[0], sh_ref[0]).astype(BF16)


def _out_proj(x, gt1, a, bb, wa, wb, sc2, sh2, g2, *, tm):
    b, s, d = x.shape
    row = lambda w: pl.BlockSpec((1, tm, w), lambda bi, i: (bi, i, 0))
    mod = pl.BlockSpec((1, 1, d), lambda bi, i: (bi, 0, 0))
    return pl.pallas_call(
        _out_proj_kernel,
        grid=(b, s // tm),
        in_specs=[row(d), mod, row(a.shape[-1]), row(bb.shape[-1]), _const_spec(wa.shape),
                  _const_spec(wb.shape), mod, mod, _const_spec(g2.shape)],
        out_specs=[row(d), row(d)],
        out_shape=[jax.ShapeDtypeStruct((b, s, d), F32), jax.ShapeDtypeStruct((b, s, d), BF16)],
        compiler_params=_cparams(("parallel", "parallel")),
        name="out_proj",
    )(x, gt1, a, bb, wa, wb, sc2, sh2, g2)


def _ffn_kernel(x_ref, h_ref, gt_ref, wg_ref, wu_ref, wd_ref, o_ref, xs_ref, *, n_x_chunks):
    j = pl.program_id(2)
    rows = x_ref.shape[1]

    @pl.when(j == 0)
    def _():
        o_ref[...] = jnp.zeros(o_ref.shape, F32)

    chunk = jnp.minimum(j, n_x_chunks - 1)
    xs_ref[pl.ds(pl.multiple_of(chunk * rows, rows), rows), :] = x_ref[0]

    h = h_ref[0]
    tf = wg_ref.shape[1]
    cf = tf // FFN_COL_SPLIT
    gates = []
    for c in range(FFN_COL_SPLIT):
        cs = slice(c * cf, (c + 1) * cf)
        gates.append((jnp.dot(h, wg_ref[:, cs], preferred_element_type=F32),
                      jnp.dot(h, wu_ref[:, cs], preferred_element_type=F32)))
    part = None
    for c, (g, u) in enumerate(gates):
        act = (g * jax.nn.sigmoid(g) * u).astype(BF16)
        d = jnp.dot(act, wd_ref[c * cf:(c + 1) * cf, :], preferred_element_type=F32)
        part = d if part is None else part + d
    o_ref[0] += part

    @pl.when(j == pl.num_programs(2) - 1)
    def _():
        o_ref[0] = xs_ref[...] + gt_ref[0] * o_ref[0]


def _ffn(x, h2, gt2, wg, wu, wd, *, tm, tf):
    b, s, d = x.shape
    dff = wg.shape[1]
    nj = dff // tf
    nxc = 1
    while nxc * 2 <= nj and (tm // (nxc * 2)) % SUBLANES == 0:
        nxc *= 2
    row = pl.BlockSpec((1, tm, d), lambda bi, i, j: (bi, i, 0))
    x_chunk = pl.BlockSpec((1, tm // nxc, d), lambda bi, i, j: (bi, i * nxc + jnp.minimum(j, nxc - 1), 0))
    return pl.pallas_call(
        functools.partial(_ffn_kernel, n_x_chunks=nxc),
        grid=(b, s // tm, nj),
        in_specs=[x_chunk, row, pl.BlockSpec((1, 1, d), lambda bi, i, j: (bi, 0, 0)),
                  pl.BlockSpec((d, tf), lambda bi, i, j: (0, j)),
                  pl.BlockSpec((d, tf), lambda bi, i, j: (0, j)),
                  pl.BlockSpec((tf, d), lambda bi, i, j: (j, 0))],
        out_specs=row,
        out_shape=jax.ShapeDtypeStruct((b, s, d), F32),
        scratch_shapes=[pltpu.VMEM((tm, d), F32)],
        compiler_params=_cparams(("parallel", "parallel", "arbitrary")),
        name="ffn",
    )(x, h2, gt2, wg, wu, wd)


def _rope_tables(s):
    pos = jnp.arange(s, dtype=jnp.float32)
    inv = 1.0 / (ROPE_THETA ** (jnp.arange(0, QK_ROPE, 2, dtype=jnp.float32) / QK_ROPE))
    ang = pos[:, None] * inv[None, :]
    return jnp.cos(ang), jnp.sin(ang)


def _pick_tile(n, pref):
    t = min(pref, n)
    while n % t:
        t //= 2
    return t


def kernel(x, c, rel_bias, w_ada, b_ada, g_norm1, w_in, g_q_diff, g_k_diff, lambda_vecs, g_subln, g_q_a, w_q_b, g_kv_a, w_kv_b, g_q_mla, g_k_mla, w_out, g_norm2, w_gate, w_up, w_down):
    b, s, d = x.shape
    depth = w_ada.shape[0]
    diff_width = d // 2
    n_hd = diff_width // DV_DIFF
    n_hm = (d - diff_width) // V_MLA
    assert n_hd == n_hm and n_hd % 2 == 0
    n_heads = n_hd
    wd = n_heads * LANES
    half = QK_ROPE // 2
    tk = _pick_tile(s, ATT_TK)
    nkt = s // tk
    t = tk
    assert t + 1 >= REL_MAX_DIST
    tm = _pick_tile(s, 512)

    cos, sin = _rope_tables(s)
    cosq, sinq = jnp.tile(cos, (1, n_heads)), jnp.tile(sin, (1, n_heads))
    cosk = jnp.tile(cos, (1, 4))
    sink = jnp.tile(jnp.concatenate([-sin, sin], axis=1), (1, 2))

    c_pad = jnp.pad(c, ((0, (-b) % 8), (0, 0)))

    for l in range(depth):
        lambda_init = 0.8 - 0.6 * math.exp(-0.3 * l)
        mod = _ada(c_pad, w_ada[l], b_ada[l][None, :])[:b]
        sh1, sc1, gt1, sh2, sc2, gt2 = [m[:, None, :] for m in jnp.split(mod, 6, axis=-1)]

        wi = w_in[l]
        o = 3 * wd + Q_LORA + KV_LORA
        w_kpe = wi[:, o:o + QK_ROPE]
        w_kpe_sw = jnp.concatenate([w_kpe[:, half:], w_kpe[:, :half]], axis=1)
        wqkv = wi.astype(BF16)
        wlat = jnp.concatenate([w_kpe, w_kpe, w_kpe_sw, w_kpe_sw], axis=1).astype(BF16)
        wq = w_q_b[l].reshape(Q_LORA, n_heads, QK_HEAD)
        wqb = jnp.concatenate([wq[:, :, :QK_NOPE].reshape(Q_LORA, -1),
                               wq[:, :, QK_NOPE:QK_NOPE + half].reshape(Q_LORA, -1),
                               wq[:, :, QK_NOPE + half:].reshape(Q_LORA, -1)], axis=1).astype(BF16)
        wkv = w_kv_b[l].reshape(KV_LORA, n_heads, QK_NOPE + V_MLA)
        wkvb = jnp.concatenate([wkv[:, :, :QK_NOPE].reshape(KV_LORA, -1),
                                wkv[:, :, QK_NOPE:].reshape(KV_LORA, -1)], axis=1).astype(BF16)
        gq, gk = g_q_mla[l], g_k_mla[l]
        gk_pe = gk[QK_NOPE:]
        gk_pe_sw = jnp.concatenate([gk_pe[half:], gk_pe[:half]])

        qt_d, kd, vt_d, qt_m, k_m, vt_m = _in_proj(
            x, sc1, sh1, g_norm1[l][None, :], wqkv, wlat,
            jnp.tile(g_q_diff[l], 2)[None, :], jnp.tile(g_k_diff[l], 2)[None, :],
            g_q_a[l][None, :], wqb, g_kv_a[l][None, :], wkvb,
            gq[None, :QK_NOPE], jnp.tile(gq[QK_NOPE:QK_NOPE + half], n_heads)[None, :],
            jnp.tile(gq[QK_NOPE + half:], n_heads)[None, :],
            gk[None, :QK_NOPE], jnp.tile(gk_pe, 2)[None, :], jnp.tile(gk_pe_sw, 2)[None, :],
            cosq, sinq, cosk, sink, n_heads=n_heads, tm=tk)

        bias = _bias_tiles(rel_bias, t)
        gmax = lambda g: jnp.max(jnp.abs(g))
        bound_d = (NORM_SLACK * LOG2E * DH_DIFF ** 0.5 * gmax(g_q_diff[l]) * gmax(g_k_diff[l])
                   + LOG2E * gmax(rel_bias))
        bound_m = NORM_SLACK * LOG2E * QK_HEAD ** 0.5 * gmax(g_q_mla[l]) * gmax(g_k_mla[l])
        diff_args = (rel_bias, lambda_vecs[l], g_subln[l][:, None], qt_d, kd, vt_d, bias)
        a_out = lax.cond(
            bound_d <= SCORE_BOUND,
            lambda *a: _diff_attn(*a, lambda_init=lambda_init, bounded=True),
            lambda *a: _diff_attn(*a, lambda_init=lambda_init, bounded=False), *diff_args)
        b_out, wg, wu, wdn = lax.cond(
            bound_m <= SCORE_BOUND,
            lambda *a: _mla_attn(*a, bounded=True),
            lambda *a: _mla_attn(*a, bounded=False), qt_m, k_m, vt_m, w_gate[l], w_up[l], w_down[l])

        wo = w_out[l].astype(BF16)
        x, h2 = _out_proj(x, gt1, a_out, b_out, wo[:diff_width], wo[diff_width:], sc2, sh2,
                          g_norm2[l][None, :], tm=tm)
        x = _ffn(x, h2, gt2, wg, wu, wdn, tm=_pick_tile(s, FFN_TM), tf=_pick_tile(w_gate.shape[-1], FFN_TF))
    return x
```

```python
import functools
import math

import jax
import jax.numpy as jnp
from jax import lax
from jax.experimental import pallas as pl
from jax.experimental.pallas import tpu as pltpu

F32 = jnp.float32
BF16 = jnp.bfloat16

DH_DIFF = 64
DV_DIFF = 2 * DH_DIFF
QK_NOPE = 128
QK_ROPE = 64
QK_HEAD = QK_NOPE + QK_ROPE
V_MLA = 128
Q_LORA = 512
KV_LORA = 256
ROPE_THETA = 10000.0
REL_BUCKETS = 32
REL_MAX_DIST = 128
EPS = 1e-6
LOG2E = 1.4426950408889634

LANES = 128
SUBLANES = 8
VMEM_LIMIT_BYTES = 56 * 1024 * 1024

OUT_PROJ_SPLIT = 4
IN_PROJ_SPLIT = 2
FFN_TM = 1024
FFN_TF = 512
FFN_COL_SPLIT = 2
ATT_TK = 512
MLA_TQ = 512
KEY_CHUNK = 256
BOUNDED_DEPTH = 2
DIFF_BOUNDED_Q_TILES = 8
MLA_BOUNDED_Q_TILES = 8
SCORE_BOUND = 40.0
NORM_SLACK = 1.02


def _cparams(sem):
    return pltpu.CompilerParams(dimension_semantics=sem, vmem_limit_bytes=VMEM_LIMIT_BYTES)


def _const_spec(shape):
    nd = len(shape)
    return pl.BlockSpec(shape, lambda *_: (0,) * nd, pipeline_mode=pl.Buffered(1))


def _ada_kernel(c_ref, w_ref, b_ref, o_ref):
    c = c_ref[...]
    ca = c * jax.nn.sigmoid(c)
    o_ref[...] = jnp.dot(ca.astype(BF16), w_ref[...].astype(BF16), preferred_element_type=F32) + b_ref[...]


def _ada(c_pad, w, b, tn=1024):
    m, d = c_pad.shape
    n = w.shape[1]
    return pl.pallas_call(
        _ada_kernel,
        grid=(n // tn,),
        in_specs=[pl.BlockSpec((m, d), lambda j: (0, 0)),
                  pl.BlockSpec((d, tn), lambda j: (0, j)),
                  pl.BlockSpec((1, tn), lambda j: (0, j))],
        out_specs=pl.BlockSpec((m, tn), lambda j: (0, j)),
        out_shape=jax.ShapeDtypeStruct((m, n), F32),
        compiler_params=_cparams(("arbitrary",)),
        name="ada",
    )(c_pad, w, b)


def _modulated_norm(x, g, sc, sh):
    ms = jnp.mean(x * x, axis=-1, keepdims=True)
    return (x * lax.rsqrt(ms + EPS) * g) * (1.0 + sc) + sh


def _half_lane_norm(blk, g2, lo_mask, out_scale):
    sq = blk * blk
    s_lo = jnp.sum(jnp.where(lo_mask, sq, 0.0), axis=-1, keepdims=True)
    s_hi = jnp.sum(jnp.where(lo_mask, 0.0, sq), axis=-1, keepdims=True)
    inv = jnp.where(lo_mask, lax.rsqrt(s_lo * (1.0 / DH_DIFF) + EPS),
                    lax.rsqrt(s_hi * (1.0 / DH_DIFF) + EPS))
    return blk * inv * (g2 * out_scale)


def _in_proj_kernel(x_ref, sc_ref, sh_ref, g1_ref, wqkv_ref, wlat_ref, gqd_ref, gkd_ref,
                    gqa_ref, wqb_ref, gkva_ref, wkvb_ref, gqn_ref, gq1_ref, gq2_ref,
                    gkn_ref, gkp_ref, gkps_ref, cosq_ref, sinq_ref, cosk_ref, sink_ref,
                    qtd_ref, kd_ref, vtd_ref, qtm_ref, km_ref, vtm_ref, *, n_heads):
    tm = x_ref.shape[1]
    rows = tm // IN_PROJ_SPLIT
    wd = n_heads * LANES
    half = QK_ROPE // 2
    wx = n_heads * half
    lo_mask = lax.broadcasted_iota(jnp.int32, (rows, LANES), 1) < DH_DIFF
    lane_x = lax.broadcasted_iota(jnp.int32, (rows, wx), 1) // half
    qscale = DH_DIFF ** -0.5 * LOG2E
    mscale = QK_HEAD ** -0.5 * LOG2E

    def rms(v, g):
        return v * lax.rsqrt(jnp.mean(v * v, axis=-1, keepdims=True) + EPS) * g

    for sb in range(IN_PROJ_SPLIT):
        rs = slice(sb * rows, (sb + 1) * rows)
        h = _modulated_norm(x_ref[0, rs, :], g1_ref[...], sc_ref[0], sh_ref[0]).astype(BF16)
        lat = jnp.dot(h, wqkv_ref[:, 3 * wd:3 * wd + Q_LORA + KV_LORA], preferred_element_type=F32)
        cq = lat[:, 0:Q_LORA]
        ckv = lat[:, Q_LORA:Q_LORA + KV_LORA]
        kpe4 = jnp.dot(h, wlat_ref[...], preferred_element_type=F32)
        kpe2 = kpe4[:, 0:LANES]
        kpes2 = kpe4[:, LANES:2 * LANES]

        vd = jnp.dot(h, wqkv_ref[:, 2 * wd:3 * wd], preferred_element_type=F32)
        for hh in range(n_heads):
            vtd_ref[0, hh, 0, :, rs] = vd[:, hh * LANES:(hh + 1) * LANES].T.astype(BF16)

        kv = jnp.dot(rms(ckv, gkva_ref[...]).astype(BF16), wkvb_ref[...], preferred_element_type=F32)
        for hh in range(n_heads):
            vtm_ref[0, hh, 0, :, rs] = kv[:, wd + hh * LANES:wd + (hh + 1) * LANES].T.astype(BF16)
        ss_pe = jnp.sum(jnp.where(lo_mask, kpe2 * kpe2, 0.0), axis=-1, keepdims=True)
        kr2 = kpe2 * gkp_ref[...] * cosk_ref[rs, :] + kpes2 * gkps_ref[...] * sink_ref[rs, :]
        for hh in range(n_heads):
            sl = slice(hh * LANES, (hh + 1) * LANES)
            kn = kv[:, sl]
            ss = jnp.sum(kn * kn, axis=-1, keepdims=True) + ss_pe
            r = lax.rsqrt(ss * (1.0 / QK_HEAD) + EPS)
            km_ref[0, hh, rs, 0:QK_NOPE] = (kn * r * gkn_ref[...]).astype(BF16)
            km_ref[0, hh, rs, QK_NOPE:QK_HEAD] = (kr2 * r)[:, 0:QK_ROPE].astype(BF16)

        qd = jnp.dot(h, wqkv_ref[:, 0:wd], preferred_element_type=F32)
        for hh in range(n_heads):
            sl = slice(hh * LANES, (hh + 1) * LANES)
            qtd_ref[0, hh, :, rs] = _half_lane_norm(qd[:, sl], gqd_ref[...], lo_mask, qscale).T.astype(BF16)

        qm = jnp.dot(rms(cq, gqa_ref[...]).astype(BF16), wqb_ref[...], preferred_element_type=F32)
        x1 = qm[:, wd:wd + wx]
        x2 = qm[:, wd + wx:wd + 2 * wx]
        sq_x = x1 * x1 + x2 * x2
        inv_x = jnp.zeros((rows, wx), F32)
        for hh in range(n_heads):
            sl = slice(hh * LANES, (hh + 1) * LANES)
            qn = qm[:, sl]
            ss = (jnp.sum(qn * qn, axis=-1, keepdims=True)
                  + jnp.sum(jnp.where(lane_x == hh, sq_x, 0.0), axis=-1, keepdims=True))
            r = lax.rsqrt(ss * (1.0 / QK_HEAD) + EPS) * mscale
            qtm_ref[0, hh, 0:QK_NOPE, rs] = (qn * r * gqn_ref[...]).T.astype(BF16)
            inv_x = jnp.where(lane_x == hh, r, inv_x)
        a1 = x1 * gq1_ref[...]
        a2 = x2 * gq2_ref[...]
        cq_t = cosq_ref[rs, :]
        sq_t = sinq_ref[rs, :]
        x1t = ((a1 * cq_t - a2 * sq_t) * inv_x).T.astype(BF16)
        x2t = ((a2 * cq_t + a1 * sq_t) * inv_x).T.astype(BF16)
        for hh in range(n_heads):
            qtm_ref[0, hh, QK_NOPE:QK_NOPE + half, rs] = x1t[hh * half:(hh + 1) * half]
            qtm_ref[0, hh, QK_NOPE + half:QK_HEAD, rs] = x2t[hh * half:(hh + 1) * half]

        kd = jnp.dot(h, wqkv_ref[:, wd:2 * wd], preferred_element_type=F32)
        for hh in range(n_heads):
            sl = slice(hh * LANES, (hh + 1) * LANES)
            kd_ref[0, rs, sl] = _half_lane_norm(kd[:, sl], gkd_ref[...], lo_mask, 1.0).astype(BF16)


def _in_proj(x, sc1, sh1, g1, wqkv, wlat, gqd, gkd, gqa, wqb, gkva, wkvb, gqn, gq1, gq2,
             gkn, gkp, gkps, cosq, sinq, cosk, sink, *, n_heads, tm):
    b, s, d = x.shape
    wd = n_heads * LANES
    wx = n_heads * (QK_ROPE // 2)
    row = lambda w: pl.BlockSpec((1, tm, w), lambda bi, i: (bi, i, 0))
    mod = pl.BlockSpec((1, 1, d), lambda bi, i: (bi, 0, 0))
    tab = lambda w: pl.BlockSpec((tm, w), lambda bi, i: (i, 0))
    consts = [g1, wqkv, wlat, gqd, gkd, gqa, wqb, gkva, wkvb, gqn, gq1, gq2, gkn, gkp, gkps]
    return pl.pallas_call(
        functools.partial(_in_proj_kernel, n_heads=n_heads),
        grid=(b, s // tm),
        in_specs=[row(d), mod, mod] + [_const_spec(a.shape) for a in consts]
                 + [tab(wx), tab(wx), tab(LANES), tab(LANES)],
        out_specs=[pl.BlockSpec((1, n_heads, LANES, tm), lambda bi, i: (bi, 0, 0, i)),
                   row(wd),
                   pl.BlockSpec((1, n_heads, 1, DV_DIFF, tm), lambda bi, i: (bi, 0, i, 0, 0)),
                   pl.BlockSpec((1, n_heads, QK_HEAD, tm), lambda bi, i: (bi, 0, 0, i)),
                   pl.BlockSpec((1, n_heads, tm, QK_HEAD), lambda bi, i: (bi, 0, i, 0)),
                   pl.BlockSpec((1, n_heads, 1, V_MLA, tm), lambda bi, i: (bi, 0, i, 0, 0))],
        out_shape=[jax.ShapeDtypeStruct((b, n_heads, LANES, s), BF16),
                   jax.ShapeDtypeStruct((b, s, wd), BF16),
                   jax.ShapeDtypeStruct((b, n_heads, s // tm, DV_DIFF, tm), BF16),
                   jax.ShapeDtypeStruct((b, n_heads, QK_HEAD, s), BF16),
                   jax.ShapeDtypeStruct((b, n_heads, s, QK_HEAD), BF16),
                   jax.ShapeDtypeStruct((b, n_heads, s // tm, V_MLA, tm), BF16)],
        compiler_params=_cparams(("parallel", "parallel")),
        name="in_proj",
    )(x, sc1, sh1, *consts, cosq, sinq, cosk, sink)


def _t5_bucket(rel):
    nb = REL_BUCKETS // 2
    max_exact = nb // 2
    base = jnp.where(rel > 0, nb, 0)
    n = jnp.abs(rel)
    nf = jnp.maximum(n, 1).astype(jnp.float32)
    large = max_exact + (jnp.log(nf / max_exact) / math.log(REL_MAX_DIST / max_exact)
                         * (nb - max_exact)).astype(jnp.int32)
    large = jnp.minimum(large, nb - 1)
    return base + jnp.where(n < max_exact, n, large)


def _bias_kernel(rb_ref, bucket_ref, o_ref):
    hh = pl.program_id(0)
    t = o_ref.shape[-1]
    far_left = rb_ref[REL_BUCKETS // 2 - 1, hh]
    far_right = rb_ref[REL_BUCKETS - 1, hh]
    o_ref[0, 0] = jnp.full((t, t), far_left, F32) * LOG2E
    o_ref[0, 4] = jnp.full((t, t), far_right, F32) * LOG2E
    bk_strip = bucket_ref[...]
    strip = jnp.zeros(bk_strip.shape, F32)
    for bk in range(REL_BUCKETS):
        strip = jnp.where(bk_strip == bk, rb_ref[bk, hh], strip)
    strip = strip * LOG2E
    for d in (-1, 0, 1):
        diag = jnp.concatenate([strip[:, (2 - d) * t:(3 - d) * t], strip[:, (1 - d) * t:(2 - d) * t]], axis=1)
        rows = jnp.broadcast_to(diag, (t, 2 * t))
        o_ref[0, d + 2] = pltpu.roll(rows, 0, 1, stride=1, stride_axis=0)[:, :t]


def _bias_tiles(rel_bias, t):
    n_heads = rel_bias.shape[1]
    buckets = _t5_bucket(2 * t - jnp.arange(4 * t, dtype=jnp.int32))[None, :]
    return pl.pallas_call(
        _bias_kernel,
        grid=(n_heads,),
        in_specs=[pl.BlockSpec(memory_space=pltpu.SMEM),
                  pl.BlockSpec((1, 4 * t), lambda hh: (0, 0))],
        out_specs=pl.BlockSpec((1, 5, t, t), lambda hh: (hh, 0, 0, 0)),
        out_shape=jax.ShapeDtypeStruct((n_heads, 5, t, t), F32),
        compiler_params=_cparams(("arbitrary",)),
        name="rel_bias_tiles",
    )(rel_bias, buckets)


ONES_ROWS = 16


def _softmax_state_step(s, offset, vt, state):
    m_new, alpha, p = _softmax_weights(s, offset, state[0])
    return m_new, _accumulate(alpha, state[1], vt, p)


def _softmax_weights(s, offset, m_old):
    m_new = jnp.maximum(m_old, jnp.max(s, axis=0, keepdims=True) + offset)
    alpha = jnp.exp2(m_old - m_new)
    p = jnp.exp2(s - (m_new - offset)).astype(BF16)
    return m_new, alpha, p


def _accumulate(alpha, acc_old, vt, p):
    return alpha * acc_old + jnp.dot(vt, p, preferred_element_type=F32)


def _init_state(dv, t):
    return (jnp.full((1, t), -jnp.inf, F32), jnp.zeros((dv + ONES_ROWS, t), F32))


def _with_ones(vt):
    return jnp.concatenate([vt, jnp.ones((ONES_ROWS, vt.shape[1]), vt.dtype)], axis=0)


def _normalised(state, dv):
    _, acc = state
    return acc[:dv] / acc[dv:dv + 1]


def _diff_attn_kernel(rb_ref, lamv_ref, gsub_ref, qt_ref, k_ref, vt_ref, bias_ref, o_ref, *,
                      lambda_init, bounded):
    hh = pl.program_id(0)
    t = bias_ref.shape[-1]
    nq = qt_ref.shape[-1] // t
    nkt = vt_ref.shape[2]

    off_left = rb_ref[REL_BUCKETS // 2 - 1, hh] * LOG2E
    off_right = rb_ref[REL_BUCKETS - 1, hh] * LOG2E
    lv = lamv_ref[...]
    lam = (jnp.exp(jnp.sum(lv[0:1] * lv[1:2], axis=-1, keepdims=True))
           - jnp.exp(jnp.sum(lv[2:3] * lv[3:4], axis=-1, keepdims=True)) + lambda_init)

    n_near = min(nkt, 3)
    row = lax.broadcasted_iota(jnp.int32, (LANES, t), 0)
    zero = jnp.zeros((LANES, t), BF16)

    def query_maps(j):
        qt = qt_ref[0, 0, :, j * t:(j + 1) * t]
        return jnp.where(row < DH_DIFF, qt, zero), jnp.where(row < DH_DIFF, zero, qt)

    def tile_index(j, r):
        return lax.rem(pl.program_id(2) * nq + j - 1 + r + nkt, nkt)

    def far_offset(j, r):
        return jnp.where(tile_index(j, r) < pl.program_id(2) * nq + j, off_left, off_right)

    def biased_scores(j, qmap, r, k0, ck):
        kj = tile_index(j, r)
        k_t = k_ref[0, pl.ds(pl.multiple_of(kj * t + k0, ck), ck), :]
        s = jnp.dot(k_t, qmap, preferred_element_type=F32)
        saturated = (r >= n_near or (nkt >= 3 and r == 0 and k0 + ck - 1 - t <= -REL_MAX_DIST)
                     or (nkt >= 3 and r == 2 and k0 + 1 >= REL_MAX_DIST))
        if not (bounded and saturated) and r < n_near:
            s = s + bias_ref[0, jnp.clip(kj - (pl.program_id(2) * nq + j), -2, 2) + 2, k0:k0 + ck, :]
        elif bounded:
            s = s + far_offset(j, r)
        return s

    def finish(j, outs):
        o = outs[0] - lam * outs[1]
        ms = jnp.mean(o * o, axis=0, keepdims=True)
        y = o * lax.rsqrt(ms + EPS) * (gsub_ref[...] * (1.0 - lambda_init))
        o_ref[0, j * t:(j + 1) * t, :] = y.T.astype(BF16)

    if bounded:
        ck = min(t, KEY_CHUNK)
        npt = t // ck
        qmaps = [query_maps(j) for j in range(nq)]
        items = [(j, r, c, mp) for j in range(nq) for r in range(nkt) for c in range(npt) for mp in range(2)]
        score = lambda it: biased_scores(it[0], qmaps[it[0]][it[3]], it[1], it[2] * ck, ck)
        depth = min(BOUNDED_DEPTH, len(items))
        pending = [score(it) for it in items[:depth]]
        for n, (j, r, c, mp) in enumerate(items):
            if (r, c, mp) == (0, 0, 0):
                accs = [jnp.zeros((DV_DIFF, t), F32) for _ in range(2)]
                dens = [jnp.zeros((SUBLANES, t), F32) for _ in range(2)]
            s = pending.pop(0)
            if n + depth < len(items):
                pending.append(score(items[n + depth]))
            p = jnp.exp2(s)
            dens[mp] = dens[mp] + jnp.sum(p.reshape(ck // SUBLANES, SUBLANES, t), axis=0)
            accs[mp] = accs[mp] + jnp.dot(vt_ref[0, 0, tile_index(j, r), :, c * ck:(c + 1) * ck],
                                          p.astype(BF16), preferred_element_type=F32)
            if (r, c, mp) == (nkt - 1, npt - 1, 1):
                finish(j, [acc / jnp.sum(den, axis=0, keepdims=True) for acc, den in zip(accs, dens)])
    else:
        for j in range(nq):
            qmap = query_maps(j)
            states = [_init_state(DV_DIFF, t), _init_state(DV_DIFF, t)]
            score = lambda r: [biased_scores(j, qmap[mp], r, 0, t) for mp in range(2)]
            s_next = score(0)
            for r in range(nkt):
                s_cur = s_next
                if r + 1 < nkt:
                    s_next = score(r + 1)
                vt = _with_ones(vt_ref[0, 0, tile_index(j, r)])
                offset = 0.0 if r < n_near else far_offset(j, r)
                for mp in range(2):
                    states[mp] = _softmax_state_step(s_cur[mp], offset, vt, states[mp])
            finish(j, [_normalised(st, DV_DIFF) for st in states])


def _diff_attn(rel_bias, lamv, gsub_col, qt, k, vt, bias, *, lambda_init, bounded):
    b, n_heads, _, s = qt.shape
    t = bias.shape[-1]
    nkt, tk = vt.shape[2], vt.shape[4]
    assert tk == t
    tq = _pick_tile(s, t * (DIFF_BOUNDED_Q_TILES if bounded else 1))
    return pl.pallas_call(
        functools.partial(_diff_attn_kernel, lambda_init=lambda_init, bounded=bounded),
        grid=(n_heads, b, s // tq),
        in_specs=[pl.BlockSpec(memory_space=pltpu.SMEM),
                  pl.BlockSpec(lamv.shape, lambda hh, bi, qi: (0, 0)),
                  pl.BlockSpec(gsub_col.shape, lambda hh, bi, qi: (0, 0)),
                  pl.BlockSpec((1, 1, LANES, tq), lambda hh, bi, qi: (bi, hh, 0, qi)),
                  pl.BlockSpec((1, s, LANES), lambda hh, bi, qi: (bi, 0, hh)),
                  pl.BlockSpec((1, 1, nkt, DV_DIFF, tk), lambda hh, bi, qi: (bi, hh, 0, 0, 0)),
                  pl.BlockSpec((1, 5, t, t), lambda hh, bi, qi: (hh, 0, 0, 0))],
        out_specs=pl.BlockSpec((1, tq, DV_DIFF), lambda hh, bi, qi: (bi, qi, hh)),
        out_shape=jax.ShapeDtypeStruct((b, s, n_heads * DV_DIFF), BF16),
        compiler_params=_cparams(("parallel", "parallel", "parallel")),
        name="diff_attn",
    )(rel_bias, lamv, gsub_col, qt, k, vt, bias)


def _mla_attn_kernel(qt_ref, k_ref, vt_ref, o_ref):
    nkt = vt_ref.shape[2]
    tk = vt_ref.shape[-1]
    qt = qt_ref[0, 0]
    m, acc = _init_state(V_MLA, qt.shape[-1])
    scores = lambda kj: jnp.dot(k_ref[0, 0, kj * tk:(kj + 1) * tk, :], qt, preferred_element_type=F32)
    depth = 2
    pending = [scores(kj) for kj in range(min(depth, nkt))]
    for kj in range(nkt):
        s = pending.pop(0)
        if kj + depth < nkt:
            pending.append(scores(kj + depth))
        m, alpha, p = _softmax_weights(s, 0.0, m)
        acc = _accumulate(alpha, acc, _with_ones(vt_ref[0, 0, kj]), p)
    o_ref[0] = _normalised((m, acc), V_MLA).T.astype(BF16)


def _mla_attn_bounded_kernel(qt_ref, k_ref, vt_ref, *rest):
    n_side = (len(rest) - 1) // 2
    o_ref = rest[n_side]
    for src_ref, dst_ref in zip(rest[:n_side], rest[n_side + 1:]):
        dst_ref[...] = src_ref[...].astype(BF16)
    nkt = vt_ref.shape[2]
    tk = vt_ref.shape[-1]
    t = min(qt_ref.shape[-1], MLA_TQ)
    nq = qt_ref.shape[-1] // t
    ck = min(tk, KEY_CHUNK)
    npt = tk // ck
    n_chunks = nkt * npt
    items = [(j, i) for j in range(nq) for i in range(n_chunks)]

    def scores(item):
        j, i = item
        return jnp.dot(k_ref[0, 0, i * ck:(i + 1) * ck, :], qt_ref[0, 0, :, j * t:(j + 1) * t],
                       preferred_element_type=F32)

    depth = min(BOUNDED_DEPTH, len(items))
    pending = [scores(it) for it in items[:depth]]
    for n, (j, i) in enumerate(items):
        if i == 0:
            acc = jnp.zeros((V_MLA, t), F32)
            den = jnp.zeros((SUBLANES, t), F32)
        s = pending.pop(0)
        if n + depth < len(items):
            pending.append(scores(items[n + depth]))
        c = i % npt
        p = jnp.exp2(s)
        den = den + jnp.sum(p.reshape(ck // SUBLANES, SUBLANES, t), axis=0)
        acc = acc + jnp.dot(vt_ref[0, 0, i // npt, :, c * ck:(c + 1) * ck], p.astype(BF16),
                            preferred_element_type=F32)
        if i == n_chunks - 1:
            o_ref[0, j * t:(j + 1) * t, :] = (acc / jnp.sum(den, axis=0, keepdims=True)).T.astype(BF16)


def _mla_attn(qt, k, vt, *side, bounded):
    b, n_heads, _, s = qt.shape
    t = _pick_tile(s, MLA_TQ * (MLA_BOUNDED_Q_TILES if bounded else 1))
    nkt, tk = vt.shape[2], vt.shape[4]
    q_steps = s // t
    steps = b * n_heads * q_steps
    bf16_rows = 2 * SUBLANES
    fused = bounded and all(w.shape[0] % (steps * bf16_rows) == 0 for w in side)
    slab = lambda w: pl.BlockSpec((w.shape[0] // steps, w.shape[1]),
                                  lambda bi, hh, qi: ((bi * n_heads + hh) * q_steps + qi, 0))
    attn_spec = pl.BlockSpec((1, t, V_MLA), lambda bi, hh, qi: (bi, qi, hh))
    attn_shape = jax.ShapeDtypeStruct((b, s, n_heads * V_MLA), BF16)
    out = pl.pallas_call(
        _mla_attn_bounded_kernel if bounded else _mla_attn_kernel,
        grid=(b, n_heads, q_steps),
        in_specs=[pl.BlockSpec((1, 1, QK_HEAD, t), lambda bi, hh, qi: (bi, hh, 0, qi)),
                  pl.BlockSpec((1, 1, s, QK_HEAD), lambda bi, hh, qi: (bi, hh, 0, 0)),
                  pl.BlockSpec((1, 1, nkt, V_MLA, tk), lambda bi, hh, qi: (bi, hh, 0, 0, 0))]
                 + ([slab(w) for w in side] if fused else []),
        out_specs=[attn_spec] + [slab(w) for w in side] if fused else attn_spec,
        out_shape=([attn_shape] + [jax.ShapeDtypeStruct(w.shape, BF16) for w in side]) if fused else attn_shape,
        compiler_params=_cparams(("parallel", "parallel", "parallel")),
        name="mla_attn",
    )(qt, k, vt, *(side if fused else ()))
    if fused:
        return tuple(out)
    return (out,) + tuple(w.astype(BF16) for w in side)


def _out_proj_kernel(x_ref, gt_ref, a_ref, b_ref, wa_ref, wb_ref, sc_ref, sh_ref, g2_ref, o_ref, h_ref):
    tm = x_ref.shape[1]
    rows = tm // OUT_PROJ_SPLIT
    mixes = []
    for i in range(OUT_PROJ_SPLIT):
        sl = slice(i * rows, (i + 1) * rows)
        mixes.append(jnp.dot(a_ref[0, sl, :], wa_ref[...], preferred_element_type=F32)
                     + jnp.dot(b_ref[0, sl, :], wb_ref[...], preferred_element_type=F32))
    for i in range(OUT_PROJ_SPLIT):
        sl = slice(i * rows, (i + 1) * rows)
        x1 = x_ref[0, sl, :] + gt_ref[0] * mixes[i]
        o_ref[0, sl, :] = x1
        h_ref[0, sl, :] = _modulated_norm(x1, g2_ref[...], sc_ref[0], sh_ref[0]).astype(BF16)


def _out_proj(x, gt1, a, bb, wa, wb, sc2, sh2, g2, *, tm):
    b, s, d = x.shape
    row = lambda w: pl.BlockSpec((1, tm, w), lambda bi, i: (bi, i, 0))
    mod = pl.BlockSpec((1, 1, d), lambda bi, i: (bi, 0, 0))
    return pl.pallas_call(
        _out_proj_kernel,
        grid=(b, s // tm),
        in_specs=[row(d), mod, row(a.shape[-1]), row(bb.shape[-1]), _const_spec(wa.shape),
                  _const_spec(wb.shape), mod, mod, _const_spec(g2.shape)],
        out_specs=[row(d), row(d)],
        out_shape=[jax.ShapeDtypeStruct((b, s, d), F32), jax.ShapeDtypeStruct((b, s, d), BF16)],
        compiler_params=_cparams(("parallel", "parallel")),
        name="out_proj",
    )(x, gt1, a, bb, wa, wb, sc2, sh2, g2)


def _ffn_kernel(x_ref, h_ref, gt_ref, wg_ref, wu_ref, wd_ref, o_ref, *, n_x_chunks):
    j = pl.program_id(2)
    rows = x_ref.shape[1]

    @pl.when(j == 0)
    def _():
        o_ref[...] = jnp.zeros(o_ref.shape, F32)

    chunk = jnp.minimum(j, n_x_chunks - 1)
    o_ref[0, pl.ds(pl.multiple_of(chunk * rows, rows), rows), :] += jnp.where(j < n_x_chunks, x_ref[0], 0.0)

    h = h_ref[0]
    tf = wg_ref.shape[1]
    cf = tf // FFN_COL_SPLIT
    gates = []
    for c in range(FFN_COL_SPLIT):
        cs = slice(c * cf, (c + 1) * cf)
        gates.append((jnp.dot(h, wg_ref[:, cs], preferred_element_type=F32),
                      jnp.dot(h, wu_ref[:, cs], preferred_element_type=F32)))
    part = None
    for c, (g, u) in enumerate(gates):
        act = (g * jax.nn.sigmoid(g) * u).astype(BF16)
        d = jnp.dot(act, wd_ref[c * cf:(c + 1) * cf, :], preferred_element_type=F32)
        part = d if part is None else part + d
    o_ref[0] += gt_ref[0] * part


def _ffn(x, h2, gt2, wg, wu, wd, *, tm, tf):
    b, s, d = x.shape
    dff = wg.shape[1]
    nj = dff // tf
    nxc = 1
    while nxc * 2 <= nj and (tm // (nxc * 2)) % SUBLANES == 0:
        nxc *= 2
    row = pl.BlockSpec((1, tm, d), lambda bi, i, j: (bi, i, 0))
    x_chunk = pl.BlockSpec((1, tm // nxc, d), lambda bi, i, j: (bi, i * nxc + jnp.minimum(j, nxc - 1), 0))
    return pl.pallas_call(
        functools.partial(_ffn_kernel, n_x_chunks=nxc),
        grid=(b, s // tm, nj),
        in_specs=[x_chunk, row, pl.BlockSpec((1, 1, d), lambda bi, i, j: (bi, 0, 0)),
                  pl.BlockSpec((d, tf), lambda bi, i, j: (0, j)),
                  pl.BlockSpec((d, tf), lambda bi, i, j: (0, j)),
                  pl.BlockSpec((tf, d), lambda bi, i, j: (j, 0))],
        out_specs=row,
        out_shape=jax.ShapeDtypeStruct((b, s, d), F32),
        compiler_params=_cparams(("parallel", "parallel", "arbitrary")),
        name="ffn",
    )(x, h2, gt2, wg, wu, wd)


def _rope_tables(s):
    pos = jnp.arange(s, dtype=jnp.float32)
    inv = 1.0 / (ROPE_THETA ** (jnp.arange(0, QK_ROPE, 2, dtype=jnp.float32) / QK_ROPE))
    ang = pos[:, None] * inv[None, :]
    return jnp.cos(ang), jnp.sin(ang)


def _pick_tile(n, pref):
    t = min(pref, n)
    while n % t:
        t //= 2
    return t


def kernel(x, c, rel_bias, w_ada, b_ada, g_norm1, w_in, g_q_diff, g_k_diff, lambda_vecs, g_subln, g_q_a, w_q_b, g_kv_a, w_kv_b, g_q_mla, g_k_mla, w_out, g_norm2, w_gate, w_up, w_down):
    b, s, d = x.shape
    depth = w_ada.shape[0]
    diff_width = d // 2
    n_hd = diff_width // DV_DIFF
    n_hm = (d - diff_width) // V_MLA
    assert n_hd == n_hm and n_hd % 2 == 0
    n_heads = n_hd
    wd = n_heads * LANES
    half = QK_ROPE // 2
    tk = _pick_tile(s, ATT_TK)
    nkt = s // tk
    t = tk
    assert t + 1 >= REL_MAX_DIST
    tm = _pick_tile(s, 512)

    cos, sin = _rope_tables(s)
    cosq, sinq = jnp.tile(cos, (1, n_heads)), jnp.tile(sin, (1, n_heads))
    cosk = jnp.tile(cos, (1, 4))
    sink = jnp.tile(jnp.concatenate([-sin, sin], axis=1), (1, 2))

    c_pad = jnp.pad(c, ((0, (-b) % 8), (0, 0)))

    for l in range(depth):
        lambda_init = 0.8 - 0.6 * math.exp(-0.3 * l)
        mod = _ada(c_pad, w_ada[l], b_ada[l][None, :])[:b]
        sh1, sc1, gt1, sh2, sc2, gt2 = [m[:, None, :] for m in jnp.split(mod, 6, axis=-1)]

        wi = w_in[l]
        o = 3 * wd + Q_LORA + KV_LORA
        w_kpe = wi[:, o:o + QK_ROPE]
        w_kpe_sw = jnp.concatenate([w_kpe[:, half:], w_kpe[:, :half]], axis=1)
        wqkv = wi.astype(BF16)
        wlat = jnp.concatenate([w_kpe, w_kpe, w_kpe_sw, w_kpe_sw], axis=1).astype(BF16)
        wq = w_q_b[l].reshape(Q_LORA, n_heads, QK_HEAD)
        wqb = jnp.concatenate([wq[:, :, :QK_NOPE].reshape(Q_LORA, -1),
                               wq[:, :, QK_NOPE:QK_NOPE + half].reshape(Q_LORA, -1),
                               wq[:, :, QK_NOPE + half:].reshape(Q_LORA, -1)], axis=1).astype(BF16)
        wkv = w_kv_b[l].reshape(KV_LORA, n_heads, QK_NOPE + V_MLA)
        wkvb = jnp.concatenate([wkv[:, :, :QK_NOPE].reshape(KV_LORA, -1),
                                wkv[:, :, QK_NOPE:].reshape(KV_LORA, -1)], axis=1).astype(BF16)
        gq, gk = g_q_mla[l], g_k_mla[l]
        gk_pe = gk[QK_NOPE:]
        gk_pe_sw = jnp.concatenate([gk_pe[half:], gk_pe[:half]])

        qt_d, kd, vt_d, qt_m, k_m, vt_m = _in_proj(
            x, sc1, sh1, g_norm1[l][None, :], wqkv, wlat,
            jnp.tile(g_q_diff[l], 2)[None, :], jnp.tile(g_k_diff[l], 2)[None, :],
            g_q_a[l][None, :], wqb, g_kv_a[l][None, :], wkvb,
            gq[None, :QK_NOPE], jnp.tile(gq[QK_NOPE:QK_NOPE + half], n_heads)[None, :],
            jnp.tile(gq[QK_NOPE + half:], n_heads)[None, :],
            gk[None, :QK_NOPE], jnp.tile(gk_pe, 2)[None, :], jnp.tile(gk_pe_sw, 2)[None, :],
            cosq, sinq, cosk, sink, n_heads=n_heads, tm=tk)

        bias = _bias_tiles(rel_bias, t)
        gmax = lambda g: jnp.max(jnp.abs(g))
        bound_d = (NORM_SLACK * LOG2E * DH_DIFF ** 0.5 * gmax(g_q_diff[l]) * gmax(g_k_diff[l])
                   + LOG2E * gmax(rel_bias))
        bound_m = NORM_SLACK * LOG2E * QK_HEAD ** 0.5 * gmax(g_q_mla[l]) * gmax(g_k_mla[l])
        diff_args = (rel_bias, lambda_vecs[l], g_subln[l][:, None], qt_d, kd, vt_d, bias)
        a_out = lax.cond(
            bound_d <= SCORE_BOUND,
            lambda *a: _diff_attn(*a, lambda_init=lambda_init, bounded=True),
            lambda *a: _diff_attn(*a, lambda_init=lambda_init, bounded=False), *diff_args)
        b_out, wg, wu, wdn = lax.cond(
            bound_m <= SCORE_BOUND,
            lambda *a: _mla_attn(*a, bounded=True),
            lambda *a: _mla_attn(*a, bounded=False), qt_m, k_m, vt_m, w_gate[l], w_up[l], w_down[l])

        wo = w_out[l].astype(BF16)
        x, h2 = _out_proj(x, gt1, a_out, b_out, wo[:diff_width], wo[diff_width:], sc2, sh2,
                          g_norm2[l][None, :], tm=tm)
        x = _ffn(x, h2, gt2, wg, wu, wdn, tm=_pick_tile(s, FFN_TM), tf=_pick_tile(w_gate.shape[-1], FFN_TF))
    return x
```
